```python
import math, functools
import jax, jax.numpy as jnp
from jax import lax
import numpy as np

D_MODEL = 2048
BATCH = 2
SEQ = 4096
DEPTH = 4
DEC_BATCH = 8
DEC_SEQ = 1
PAST_LEN = 16384
PAGE_SIZE = 128

N_AB = DEPTH // 2
N_C = DEPTH - N_AB

A_WIDTH = D_MODEL // 2
A_HEAD = 64
A_HEADS = A_WIDTH // A_HEAD
W_LORA = 64
AA_LORA = 64
G_LORA = 128
A_PROJ = 3 * A_WIDTH + W_LORA + AA_LORA + G_LORA
RWKV_GN_EPS = 64e-5

B_WIDTH = D_MODEL - A_WIDTH
B_HEAD = 128
B_HEADS = B_WIDTH // B_HEAD
IDX_HEADS = 16
IDX_DIM = 64
B_PROJ = 3 * B_WIDTH + IDX_HEADS * IDX_DIM + IDX_DIM + IDX_HEADS
TOPK_MAX = 256
Q_BLOCK = 64
T5_BUCKETS = 32
T5_MAX_DIST = 128
AB_PROJ = A_PROJ + B_PROJ

C_WIDTH = D_MODEL
C_EXPAND = 128
C_HEADS = C_WIDTH // C_EXPAND
C_DK = C_EXPAND
C_DV = C_WIDTH // C_HEADS
C_PROJ = 2 * C_HEADS * C_DK + 2 * C_WIDTH
GLA_CHUNK = 64
RMS_EPS = 1e-5

D_FF = 5632
N_EXPERTS = 8
TOP_EXPERTS = 2
D_FF_E = 2816

ALPHA = (2 * DEPTH) ** 0.25
BETA = (8 * DEPTH) ** -0.25
LN_EPS = 1e-5

kernel_name = 'rwkv7_dsa_hgrn2_deepnorm_step'


def split_cols(a, sizes):
    return jnp.split(a, np.cumsum(sizes)[:-1].tolist(), axis=-1)


def gather_rows(a, idx):
    return jax.vmap(lambda a_b, i_b: a_b[i_b])(a, idx)


def layer_norm(x, g, b):
    xf = x.astype(jnp.float32)
    mu = jnp.mean(xf, axis=-1, keepdims=True)
    var = jnp.mean(jnp.square(xf - mu), axis=-1, keepdims=True)
    return ((xf - mu) * lax.rsqrt(var + LN_EPS) * g.astype(jnp.float32) + b.astype(jnp.float32)).astype(x.dtype)


def swiglu(x, wg, wu, wd):
    return (jax.nn.silu(x @ wg) * (x @ wu)) @ wd


def moe_swiglu(x, w_router, wg, wu, wd):
    probs = jax.nn.softmax((x @ w_router).astype(jnp.float32), axis=-1)
    top_p, top_i = lax.top_k(probs, TOP_EXPERTS)
    top_p = top_p / jnp.sum(top_p, axis=-1, keepdims=True)
    gates = jnp.sum(jax.nn.one_hot(top_i, N_EXPERTS, dtype=jnp.float32) * top_p[..., None], axis=-2).astype(x.dtype)
    y = jnp.zeros_like(x)
    for e in range(N_EXPERTS):
        y = y + gates[..., e:e + 1] * swiglu(x, wg[e], wu[e], wd[e])
    return y


def t5_bucket(dist):
    dist = jnp.maximum(dist, 0)
    max_exact = T5_BUCKETS // 2
    d = jnp.maximum(dist, 1).astype(jnp.float32)
    large = max_exact + (jnp.log(d / max_exact) / math.log(T5_MAX_DIST / max_exact) * (T5_BUCKETS - max_exact)).astype(jnp.int32)
    large = jnp.minimum(large, T5_BUCKETS - 1)
    return jnp.where(dist < max_exact, dist, large)


def dsa_attend(q, qi, wi, q_pos, k_idx, gather_kv, rel_bias, topk):
    B, T = q.shape[:2]
    L = k_idx.shape[1]
    blk = min(Q_BLOCK, T)
    nb = -(-T // blk)
    pad = nb * blk - T

    def to_blocks(a):
        a = jnp.pad(a, [(0, 0), (0, pad)] + [(0, 0)] * (a.ndim - 2))
        return jnp.moveaxis(a.reshape((B, nb, blk) + a.shape[2:]), 1, 0)

    pos_b = jnp.pad(q_pos, (0, pad)).reshape(nb, blk)
    k_pos = jnp.arange(L, dtype=jnp.int32)
    kidx_f = k_idx.astype(jnp.float32)

    def one_block(args):
        qb, qib, wib, pb = args
        s = jnp.einsum('bthd,bsd->bths', qib.astype(jnp.float32), kidx_f) * IDX_DIM ** -0.5
        score = jnp.einsum('bth,bths->bts', wib.astype(jnp.float32), jax.nn.relu(s))
        score = jnp.where(k_pos[None, None, :] <= pb[None, :, None], score, -jnp.inf)
        _, sel = lax.top_k(score, topk)
        valid = sel <= pb[None, :, None]
        k_sel, v_sel = gather_kv(sel)
        logits = jnp.einsum('bthd,btkhd->bhtk', qb, k_sel).astype(jnp.float32) * B_HEAD ** -0.5
        bias = rel_bias[t5_bucket(pb[None, :, None] - sel)].astype(jnp.float32)
        logits = logits + jnp.transpose(bias, (0, 3, 1, 2))
        logits = jnp.where(valid[:, None], logits, -jnp.inf)
        p = jax.nn.softmax(logits, axis=-1).astype(v_sel.dtype)
        return jnp.einsum('bhtk,btkhd->bthd', p, v_sel)

    out = lax.map(one_block, (to_blocks(q), to_blocks(qi), to_blocks(wi), pos_b))
    return jnp.moveaxis(out, 0, 1).reshape((B, nb * blk) + q.shape[2:])[:, :T]


def attend_self(q, k, v, qi, ki, wi, rel_bias):
    T = q.shape[1]
    gather = lambda sel: (gather_rows(k, sel), gather_rows(v, sel))
    return dsa_attend(q, qi, wi, jnp.arange(T, dtype=jnp.int32), ki, gather, rel_bias, min(TOPK_MAX, T // 4))


def attend_paged(q, k, v, qi, ki, wi, pool_k, pool_v, pool_kidx, page_table, rel_bias):
    B, T = q.shape[:2]
    past = page_table.shape[1] * PAGE_SIZE
    kidx_past = pool_kidx[page_table].reshape(B, past, IDX_DIM).astype(ki.dtype)
    kidx_all = jnp.concatenate([kidx_past, ki], axis=1)
    flat_k = pool_k.reshape((-1,) + pool_k.shape[2:])
    flat_v = pool_v.reshape((-1,) + pool_v.shape[2:])

    def gather(sel):
        ps = jnp.minimum(sel, past - 1)
        row = gather_rows(page_table, ps // PAGE_SIZE) * PAGE_SIZE + ps % PAGE_SIZE
        is_new = (sel >= past)[..., None, None]
        ns = jnp.clip(sel - past, 0, T - 1)
        k_sel = jnp.where(is_new, gather_rows(k, ns), flat_k[row].astype(k.dtype))
        v_sel = jnp.where(is_new, gather_rows(v, ns), flat_v[row].astype(v.dtype))
        return k_sel, v_sel

    q_pos = past + jnp.arange(T, dtype=jnp.int32)
    return dsa_attend(q, qi, wi, q_pos, kidx_all, gather, rel_bias, min(TOPK_MAX, (past + T) // 4))


def rwkv7_mix(p_a, shift_state, wkv_state, mu, w0, w2, a0, a2, g2, kk_s, ka_s, rk_s, lnx_g, lnx_b):
    B, T, _ = p_a.shape
    prev = jnp.concatenate([shift_state[:, None, :].astype(p_a.dtype), p_a[:, :-1]], axis=1)
    xm = p_a + (prev - p_a) * mu
    r, k, v, wd, ad, gd = split_cols(xm, [A_WIDTH, A_WIDTH, A_WIDTH, W_LORA, AA_LORA, G_LORA])
    w_pre = (w0 + jnp.tanh(wd) @ w2).astype(jnp.float32)
    decay = jnp.exp(-jnp.exp(-jax.nn.softplus(-w_pre) - 0.5))
    a = jax.nn.sigmoid((a0 + ad @ a2).astype(jnp.float32))
    g = jax.nn.sigmoid(gd) @ g2
    heads = lambda t: t.astype(jnp.float32).reshape(B, T, A_HEADS, A_HEAD)
    kk = heads(k * kk_s)
    kk = kk / jnp.maximum(jnp.sqrt(jnp.sum(jnp.square(kk), axis=-1, keepdims=True)), 1e-12)
    k_mod = heads(k.astype(jnp.float32) * (1.0 + (a - 1.0) * ka_s.astype(jnp.float32)))
    r_h, w_h, v_h, a_h = heads(r), heads(decay), heads(v), heads(a)
    tm = lambda t: jnp.moveaxis(t, 1, 0)

    def step(S, inp):
        r_t, w_t, k_t, v_t, kk_t, a_t = inp
        s_kk = jnp.einsum('bhvk,bhk->bhv', S, kk_t)
        S = S * w_t[:, :, None, :] - s_kk[..., None] * (kk_t * a_t)[:, :, None, :] + v_t[..., None] * k_t[:, :, None, :]
        return S, jnp.einsum('bhvk,bhk->bhv', S, r_t)

    S, o = lax.scan(step, wkv_state.astype(jnp.float32), (tm(r_h), tm(w_h), tm(k_mod), tm(v_h), tm(kk), tm(a_h)))
    o = jnp.moveaxis(o, 0, 1)
    mu_o = jnp.mean(o, axis=-1, keepdims=True)
    var_o = jnp.mean(jnp.square(o - mu_o), axis=-1, keepdims=True)
    o = ((o - mu_o) * lax.rsqrt(var_o + RWKV_GN_EPS)).reshape(B, T, A_WIDTH) * lnx_g.astype(jnp.float32) + lnx_b.astype(jnp.float32)
    bonus = jnp.sum(r_h * k_mod * rk_s.astype(jnp.float32).reshape(A_HEADS, A_HEAD), axis=-1, keepdims=True) * v_h
    y = (o + bonus.reshape(B, T, A_WIDTH)) * g.astype(jnp.float32)
    return y.astype(p_a.dtype), p_a[:, -1], S


def ab_mixer(x, shift_state, wkv_state, attend, w_in, rw, w_out):
    B, T, _ = x.shape
    proj = x @ w_in
    y_a, new_shift, new_wkv = rwkv7_mix(proj[..., :A_PROJ], shift_state, wkv_state, *rw)
    q, k, v, qi, ki, wi = split_cols(proj[..., A_PROJ:], [B_WIDTH, B_WIDTH, B_WIDTH, IDX_HEADS * IDX_DIM, IDX_DIM, IDX_HEADS])
    heads = lambda t: t.reshape(B, T, B_HEADS, B_HEAD)
    q, k, v = heads(q), heads(k), heads(v)
    y_b = attend(q, k, v, qi.reshape(B, T, IDX_HEADS, IDX_DIM), ki, wi * IDX_HEADS ** -0.5)
    y = jnp.concatenate([y_a, y_b.reshape(B, T, B_WIDTH)], axis=-1) @ w_out
    return y, new_shift, new_wkv, k, v, ki


def gla_chunked(q, k, v, log_f, S0):
    B, T, H, DK = q.shape
    C = min(GLA_CHUNK, T)
    n = -(-T // C)
    pad = n * C - T

    def prep(a):
        a = jnp.pad(a, ((0, 0), (0, pad), (0, 0), (0, 0)))
        return jnp.moveaxis(a.reshape((B, n, C) + a.shape[2:]), 1, 0)

    tril = jnp.tril(jnp.ones((C, C), dtype=bool))

    def step(S, inp):
        qb, kb, vb, fb = inp
        b = jnp.cumsum(fb, axis=1)
        rel = b[:, :, None] - b[:, None, :]
        decay = jnp.exp(jnp.where(tril[None, :, :, None, None], rel, -jnp.inf))
        A = jnp.einsum('bthd,bshd,btshd->bhts', qb, kb, decay)
        o = jnp.einsum('bhts,bshv->bthv', A, vb) + jnp.einsum('bthd,bhdv->bthv', qb * jnp.exp(b), S)
        b_end = b[:, -1]
        S = jnp.exp(b_end)[..., None] * S + jnp.einsum('bshd,bshv->bhdv', kb * jnp.exp(b_end[:, None] - b), vb)
        return S, o

    S, o = lax.scan(step, S0, (prep(q), prep(k), prep(v), prep(log_f)))
    o = jnp.moveaxis(o, 0, 1).reshape(B, n * C, H, v.shape[-1])[:, :T]
    return o, S


def hgrn2_mixer(x, state, w_in, lb, gn_g, w_out):
    B, T, _ = x.shape
    q, f_raw, i, g = split_cols(x @ w_in, [C_HEADS * C_DK, C_HEADS * C_DK, C_WIDTH, C_WIDTH])
    f_raw = f_raw.astype(jnp.float32)
    log_f = jnp.logaddexp(jnp.log(lb), jnp.log1p(-lb) + jax.nn.log_sigmoid(f_raw))
    k_in = (1.0 - lb) * jax.nn.sigmoid(-f_raw)
    hk = lambda t: t.astype(jnp.float32).reshape(B, T, C_HEADS, C_DK)
    o, new_state = gla_chunked(hk(jax.nn.silu(q)), hk(k_in), i.astype(jnp.float32).reshape(B, T, C_HEADS, C_DV), hk(log_f), state.astype(jnp.float32))
    o = o * lax.rsqrt(jnp.mean(jnp.square(o), axis=-1, keepdims=True) + RMS_EPS) * gn_g.astype(jnp.float32)
    y = o.reshape(B, T, C_WIDTH) * jax.nn.silu(g.astype(jnp.float32))
    return y.astype(x.dtype) @ w_out, new_state


def setup_inputs(seed: int = 0) -> dict:
    key = jax.random.key(seed)
    ks = iter(jax.random.split(key, 64))
    nrm = lambda shape, scale: scale * jax.random.normal(next(ks), shape, jnp.float32)
    n_pages = PAST_LEN // PAGE_SIZE
    n_pool = (5 * DEC_BATCH * n_pages) // 4
    inp = {}
    inp['x_prompt'] = nrm((BATCH, SEQ, D_MODEL), 1.0)
    inp['x_sample'] = nrm((DEC_BATCH, DEC_SEQ, D_MODEL), 1.0)
    inp['cache_k'] = nrm((N_AB, n_pool, PAGE_SIZE, B_HEADS, B_HEAD), 1.0)
    inp['cache_v'] = nrm((N_AB, n_pool, PAGE_SIZE, B_HEADS, B_HEAD), 1.0)
    inp['cache_kidx'] = nrm((N_AB, n_pool, PAGE_SIZE, IDX_DIM), 1.0)
    inp['state_wkv'] = nrm((N_AB, DEC_BATCH, A_HEADS, A_HEAD, A_HEAD), 0.5)
    inp['state_shift'] = nrm((N_AB, DEC_BATCH, A_PROJ), 1.0)
    inp['state_hgrn'] = nrm((N_C, DEC_BATCH, C_HEADS, C_DK, C_DV), 0.5)
    perm = jax.random.permutation(next(ks), n_pool)
    inp['page_table'] = perm[:DEC_BATCH * n_pages].reshape(DEC_BATCH, n_pages).astype(jnp.int32)
    inp['w_in_ab'] = nrm((N_AB, D_MODEL, AB_PROJ), D_MODEL ** -0.5)
    inp['rwkv_mu'] = jax.random.uniform(next(ks), (N_AB, A_PROJ), jnp.float32)
    inp['rwkv_w0'] = nrm((N_AB, A_WIDTH), 0.5)
    inp['rwkv_w2'] = nrm((N_AB, W_LORA, A_WIDTH), W_LORA ** -0.5)
    inp['rwkv_a0'] = nrm((N_AB, A_WIDTH), 0.5)
    inp['rwkv_a2'] = nrm((N_AB, AA_LORA, A_WIDTH), AA_LORA ** -0.5)
    inp['rwkv_g2'] = nrm((N_AB, G_LORA, A_WIDTH), G_LORA ** -0.5)
    inp['rwkv_kk'] = 0.85 + nrm((N_AB, A_WIDTH), 0.05)
    inp['rwkv_ka'] = 1.0 + nrm((N_AB, A_WIDTH), 0.05)
    inp['rwkv_rk'] = nrm((N_AB, A_WIDTH), 0.1)
    inp['rwkv_lnx_g'] = 1.0 + nrm((N_AB, A_WIDTH), 0.1)
    inp['rwkv_lnx_b'] = nrm((N_AB, A_WIDTH), 0.01)
    inp['w_out_ab'] = nrm((N_AB, A_WIDTH + B_WIDTH, D_MODEL), BETA * (A_WIDTH + B_WIDTH) ** -0.5)
    inp['rel_bias'] = nrm((T5_BUCKETS, B_HEADS), 0.5)
    inp['ffn_w_gate'] = nrm((N_AB, D_MODEL, D_FF), D_MODEL ** -0.5)
    inp['ffn_w_up'] = nrm((N_AB, D_MODEL, D_FF), D_MODEL ** -0.5)
    inp['ffn_w_down'] = nrm((N_AB, D_FF, D_MODEL), BETA * D_FF ** -0.5)
    inp['w_in_c'] = nrm((N_C, D_MODEL, C_PROJ), D_MODEL ** -0.5)
    inp['hgrn_lb'] = nrm((N_C, C_HEADS * C_DK), 0.5)
    inp['hgrn_gnorm'] = 1.0 + nrm((N_C, C_DV), 0.1)
    inp['w_out_c'] = nrm((N_C, C_WIDTH, D_MODEL), BETA * C_WIDTH ** -0.5)
    inp['router'] = nrm((N_C, D_MODEL, N_EXPERTS), D_MODEL ** -0.5)
    inp['moe_w_gate'] = nrm((N_C, N_EXPERTS, D_MODEL, D_FF_E), D_MODEL ** -0.5)
    inp['moe_w_up'] = nrm((N_C, N_EXPERTS, D_MODEL, D_FF_E), D_MODEL ** -0.5)
    inp['moe_w_down'] = nrm((N_C, N_EXPERTS, D_FF_E, D_MODEL), BETA * D_FF_E ** -0.5)
    inp['ln1_g'] = 1.0 + nrm((DEPTH, D_MODEL), 0.1)
    inp['ln1_b'] = nrm((DEPTH, D_MODEL), 0.01)
    inp['ln2_g'] = 1.0 + nrm((DEPTH, D_MODEL), 0.1)
    inp['ln2_b'] = nrm((DEPTH, D_MODEL), 0.01)
    return inp


def reference(x_prompt, x_sample, cache_k, cache_v, cache_kidx, state_wkv, state_shift, state_hgrn, page_table,
              w_in_ab, rwkv_mu, rwkv_w0, rwkv_w2, rwkv_a0, rwkv_a2, rwkv_g2, rwkv_kk, rwkv_ka, rwkv_rk,
              rwkv_lnx_g, rwkv_lnx_b, w_out_ab, rel_bias, ffn_w_gate, ffn_w_up, ffn_w_down,
              w_in_c, hgrn_lb, hgrn_gnorm, w_out_c, router, moe_w_gate, moe_w_up, moe_w_down,
              ln1_g, ln1_b, ln2_g, ln2_b):
    lb_all = jnp.cumsum(jax.nn.softmax(hgrn_lb.astype(jnp.float32), axis=0), axis=0)
    lb_all = lb_all - lb_all[0:1]
    attend_p = functools.partial(attend_self, rel_bias=rel_bias)
    xp, xs = x_prompt, x_sample
    kp_l, vp_l, kip_l, wkvp_l, shp_l, hgp_l = [], [], [], [], [], []
    ks_l, vs_l, kis_l, wkvs_l, shs_l, hgs_l = [], [], [], [], [], []
    for l in range(DEPTH):
        j = l // 2
        if l % 2 == 0:
            rw = (rwkv_mu[j], rwkv_w0[j], rwkv_w2[j], rwkv_a0[j], rwkv_a2[j], rwkv_g2[j],
                  rwkv_kk[j], rwkv_ka[j], rwkv_rk[j], rwkv_lnx_g[j], rwkv_lnx_b[j])
            attend_s = functools.partial(attend_paged, pool_k=cache_k[j], pool_v=cache_v[j], pool_kidx=cache_kidx[j],
                                         page_table=page_table, rel_bias=rel_bias)
            shift0 = jnp.zeros((xp.shape[0], A_PROJ), xp.dtype)
            wkv0 = jnp.zeros((xp.shape[0], A_HEADS, A_HEAD, A_HEAD), jnp.float32)
            yp, shp, wkvp, kp, vp, kip = ab_mixer(xp, shift0, wkv0, attend_p, w_in_ab[j], rw, w_out_ab[j])
            ys, shs, wkvs, kss, vss, kis = ab_mixer(xs, state_shift[j], state_wkv[j], attend_s, w_in_ab[j], rw, w_out_ab[j])
            kp_l.append(kp); vp_l.append(vp); kip_l.append(kip); wkvp_l.append(wkvp); shp_l.append(shp)
            ks_l.append(kss); vs_l.append(vss); kis_l.append(kis); wkvs_l.append(wkvs); shs_l.append(shs)
        else:
            hg0 = jnp.zeros((xp.shape[0], C_HEADS, C_DK, C_DV), jnp.float32)
            yp, hgp = hgrn2_mixer(xp, hg0, w_in_c[j], lb_all[j], hgrn_gnorm[j], w_out_c[j])
            ys, hgs = hgrn2_mixer(xs, state_hgrn[j], w_in_c[j], lb_all[j], hgrn_gnorm[j], w_out_c[j])
            hgp_l.append(hgp); hgs_l.append(hgs)
        xp = layer_norm(ALPHA * xp + yp, ln1_g[l], ln1_b[l])
        xs = layer_norm(ALPHA * xs + ys, ln1_g[l], ln1_b[l])
        if l % 2 == 0:
            fp = swiglu(xp, ffn_w_gate[j], ffn_w_up[j], ffn_w_down[j])
            fs = swiglu(xs, ffn_w_gate[j], ffn_w_up[j], ffn_w_down[j])
        else:
            fp = moe_swiglu(xp, router[j], moe_w_gate[j], moe_w_up[j], moe_w_down[j])
            fs = moe_swiglu(xs, router[j], moe_w_gate[j], moe_w_up[j], moe_w_down[j])
        xp = layer_norm(ALPHA * xp + fp, ln2_g[l], ln2_b[l])
        xs = layer_norm(ALPHA * xs + fs, ln2_g[l], ln2_b[l])
    new_k_prompt = jnp.stack(kp_l).astype(cache_k.dtype)
    new_v_prompt = jnp.stack(vp_l).astype(cache_v.dtype)
    new_kidx_prompt = jnp.stack(kip_l).astype(cache_kidx.dtype)
    new_wkv_prompt = jnp.stack(wkvp_l).astype(state_wkv.dtype)
    new_shift_prompt = jnp.stack(shp_l).astype(state_shift.dtype)
    new_hgrn_prompt = jnp.stack(hgp_l).astype(state_hgrn.dtype)
    new_k_sample = jnp.stack(ks_l).astype(cache_k.dtype)
    new_v_sample = jnp.stack(vs_l).astype(cache_v.dtype)
    new_kidx_sample = jnp.stack(kis_l).astype(cache_kidx.dtype)
    new_wkv_sample = jnp.stack(wkvs_l).astype(state_wkv.dtype)
    new_shift_sample = jnp.stack(shs_l).astype(state_shift.dtype)
    new_hgrn_sample = jnp.stack(hgs_l).astype(state_hgrn.dtype)
    return (xp, xs, new_k_prompt, new_v_prompt, new_kidx_prompt, new_wkv_prompt, new_shift_prompt, new_hgrn_prompt,
            new_k_sample, new_v_sample, new_kidx_sample, new_wkv_sample, new_shift_sample, new_hgrn_sample)
```

```python
import functools
import math

import jax
import jax.numpy as jnp
import numpy as np
from jax import lax
from jax.experimental import pallas as pl
from jax.experimental.pallas import tpu as pltpu

A_HEAD = 64
W_LORA = 64
AA_LORA = 64
G_LORA = 128
RWKV_GN_EPS = 64e-5
B_HEAD = 128
IDX_HEADS = 16
IDX_DIM = 64
TOPK_MAX = 256
Q_BLOCK = 64
T5_BUCKETS = 32
T5_MAX_DIST = 128
C_EXPAND = 128
GLA_CHUNK = 64
RMS_EPS = 1e-5
N_EXPERTS = 8
TOP_EXPERTS = 2
PAGE_SIZE = 128
LN_EPS = 1e-5

VMEM_LIMIT_BYTES = 56 * 1024 * 1024


def _pick_tile(n, target, align):
    if n <= target:
        return n
    t = (target // align) * align
    while t >= align:
        if n % t == 0:
            return t
        t -= align
    return n


def _matmul_kernel(x_ref, w_ref, o_ref):
    @pl.when(pl.program_id(2) == 0)
    def _():
        o_ref[...] = jnp.zeros_like(o_ref)

    o_ref[...] += jnp.dot(x_ref[...].astype(jnp.bfloat16), w_ref[...].astype(jnp.bfloat16),
                          preferred_element_type=jnp.float32)


def matmul(x, w, tm=512, tn=512, tk=2048):
    m, k = x.shape
    k2, n = w.shape
    assert k == k2
    tm = _pick_tile(m, tm, 8)
    tk = _pick_tile(k, tk, 128)
    tn = min(tn, n) if n % 128 == 0 or n < 128 else tn
    grid = (pl.cdiv(n, tn), pl.cdiv(m, tm), k // tk)
    return pl.pallas_call(
        _matmul_kernel,
        grid=grid,
        in_specs=[pl.BlockSpec((tm, tk), lambda j, i, kk: (i, kk)),
                  pl.BlockSpec((tk, tn), lambda j, i, kk: (kk, j))],
        out_specs=pl.BlockSpec((tm, tn), lambda j, i, kk: (i, j)),
        out_shape=jax.ShapeDtypeStruct((m, n), jnp.float32),
        compiler_params=pltpu.CompilerParams(
            dimension_semantics=("parallel", "parallel", "arbitrary"),
            vmem_limit_bytes=VMEM_LIMIT_BYTES),
        name="matmul",
    )(x, w)


def _swiglu_up_kernel(x_ref, wg_ref, wu_ref, o_ref):
    xb = x_ref[...].astype(jnp.bfloat16)
    g = jnp.dot(xb, wg_ref[...].astype(jnp.bfloat16), preferred_element_type=jnp.float32)
    u = jnp.dot(xb, wu_ref[...].astype(jnp.bfloat16), preferred_element_type=jnp.float32)
    o_ref[...] = g * jax.nn.sigmoid(g) * u


def swiglu_up(x, wg, wu, tm=512, tn=256):
    m, k = x.shape
    n = wg.shape[1]
    tm = _pick_tile(m, tm, 8)
    tn = _pick_tile(n, tn, 128)
    return pl.pallas_call(
        _swiglu_up_kernel,
        grid=(n // tn, m // tm),
        in_specs=[pl.BlockSpec((tm, k), lambda j, i: (i, 0)),
                  pl.BlockSpec((k, tn), lambda j, i: (0, j)),
                  pl.BlockSpec((k, tn), lambda j, i: (0, j))],
        out_specs=pl.BlockSpec((tm, tn), lambda j, i: (i, j)),
        out_shape=jax.ShapeDtypeStruct((m, n), jnp.float32),
        compiler_params=pltpu.CompilerParams(
            dimension_semantics=("parallel", "parallel"),
            vmem_limit_bytes=VMEM_LIMIT_BYTES),
        name="swiglu_up",
    )(x, wg, wu)


def mm(x, w):
    lead = x.shape[:-1]
    return matmul(x.reshape(-1, x.shape[-1]), w).reshape(lead + (w.shape[1],))


def swiglu(x, wg, wu, wd):
    lead = x.shape[:-1]
    h = swiglu_up(x.reshape(-1, x.shape[-1]), wg, wu)
    return matmul(h, wd).reshape(lead + (wd.shape[1],))


def split_cols(a, sizes):
    return jnp.split(a, np.cumsum(sizes)[:-1].tolist(), axis=-1)


def gather_rows(a, idx):
    return jax.vmap(lambda a_b, i_b: a_b[i_b])(a, idx)


def layer_norm(x, g, b):
    mu = jnp.mean(x, axis=-1, keepdims=True)
    var = jnp.mean(jnp.square(x - mu), axis=-1, keepdims=True)
    return (x - mu) * lax.rsqrt(var + LN_EPS) * g + b


def moe_swiglu(x, w_router, wg, wu, wd):
    probs = jax.nn.softmax(mm(x, w_router), axis=-1)
    top_p, top_i = lax.top_k(probs, TOP_EXPERTS)
    top_p = top_p / jnp.sum(top_p, axis=-1, keepdims=True)
    gates = jnp.sum(jax.nn.one_hot(top_i, N_EXPERTS, dtype=jnp.float32) * top_p[..., None], axis=-2)
    y = jnp.zeros_like(x)
    for e in range(N_EXPERTS):
        y = y + gates[..., e:e + 1] * swiglu(x, wg[e], wu[e], wd[e])
    return y


def t5_bucket(dist):
    dist = jnp.maximum(dist, 0)
    max_exact = T5_BUCKETS // 2
    d = jnp.maximum(dist, 1).astype(jnp.float32)
    large = max_exact + (jnp.log(d / max_exact) / math.log(T5_MAX_DIST / max_exact) * (T5_BUCKETS - max_exact)).astype(jnp.int32)
    large = jnp.minimum(large, T5_BUCKETS - 1)
    return jnp.where(dist < max_exact, dist, large)


def dsa_attend(q, qi, wi, q_pos, k_idx, gather_kv, rel_bias, topk):
    B, T = q.shape[:2]
    L = k_idx.shape[1]
    blk = min(Q_BLOCK, T)
    nb = -(-T // blk)
    pad = nb * blk - T

    def to_blocks(a):
        a = jnp.pad(a, [(0, 0), (0, pad)] + [(0, 0)] * (a.ndim - 2))
        return jnp.moveaxis(a.reshape((B, nb, blk) + a.shape[2:]), 1, 0)

    pos_b = jnp.pad(q_pos, (0, pad)).reshape(nb, blk)
    k_pos = jnp.arange(L, dtype=jnp.int32)

    def one_block(args):
        qb, qib, wib, pb = args
        s = jnp.einsum('bthd,bsd->bths', qib, k_idx) * IDX_DIM ** -0.5
        score = jnp.einsum('bth,bths->bts', wib, jax.nn.relu(s))
        score = jnp.where(k_pos[None, None, :] <= pb[None, :, None], score, -jnp.inf)
        _, sel = lax.top_k(score, topk)
        valid = sel <= pb[None, :, None]
        k_sel, v_sel = gather_kv(sel)
        logits = jnp.einsum('bthd,btkhd->bhtk', qb, k_sel) * B_HEAD ** -0.5
        bias = rel_bias[t5_bucket(pb[None, :, None] - sel)]
        logits = logits + jnp.transpose(bias, (0, 3, 1, 2))
        logits = jnp.where(valid[:, None], logits, -jnp.inf)
        p = jax.nn.softmax(logits, axis=-1)
        return jnp.einsum('bhtk,btkhd->bthd', p, v_sel)

    out = lax.map(one_block, (to_blocks(q), to_blocks(qi), to_blocks(wi), pos_b))
    return jnp.moveaxis(out, 0, 1).reshape((B, nb * blk) + q.shape[2:])[:, :T]


def attend_self(q, k, v, qi, ki, wi, rel_bias):
    T = q.shape[1]
    gather = lambda sel: (gather_rows(k, sel), gather_rows(v, sel))
    return dsa_attend(q, qi, wi, jnp.arange(T, dtype=jnp.int32), ki, gather, rel_bias, min(TOPK_MAX, T // 4))


def attend_paged(q, k, v, qi, ki, wi, pool_k, pool_v, pool_kidx, page_table, rel_bias):
    B, T = q.shape[:2]
    past = page_table.shape[1] * PAGE_SIZE
    kidx_past = pool_kidx[page_table].reshape(B, past, IDX_DIM)
    kidx_all = jnp.concatenate([kidx_past, ki], axis=1)
    flat_k = pool_k.reshape((-1,) + pool_k.shape[2:])
    flat_v = pool_v.reshape((-1,) + pool_v.shape[2:])

    def gather(sel):
        ps = jnp.minimum(sel, past - 1)
        row = gather_rows(page_table, ps // PAGE_SIZE) * PAGE_SIZE + ps % PAGE_SIZE
        is_new = (sel >= past)[..., None, None]
        ns = jnp.clip(sel - past, 0, T - 1)
        k_sel = jnp.where(is_new, gather_rows(k, ns), flat_k[row])
        v_sel = jnp.where(is_new, gather_rows(v, ns), flat_v[row])
        return k_sel, v_sel

    q_pos = past + jnp.arange(T, dtype=jnp.int32)
    return dsa_attend(q, qi, wi, q_pos, kidx_all, gather, rel_bias, min(TOPK_MAX, (past + T) // 4))


def rwkv7_mix(p_a, shift_state, wkv_state, mu, w0, w2, a0, a2, g2, kk_s, ka_s, rk_s, lnx_g, lnx_b):
    B, T, _ = p_a.shape
    a_width = w0.shape[0]
    a_heads = a_width // A_HEAD
    prev = jnp.concatenate([shift_state[:, None, :], p_a[:, :-1]], axis=1)
    xm = p_a + (prev - p_a) * mu
    r, k, v, wd, ad, gd = split_cols(xm, [a_width, a_width, a_width, W_LORA, AA_LORA, G_LORA])
    w_pre = w0 + mm(jnp.tanh(wd), w2)
    decay = jnp.exp(-jnp.exp(-jax.nn.softplus(-w_pre) - 0.5))
    a = jax.nn.sigmoid(a0 + mm(ad, a2))
    g = mm(jax.nn.sigmoid(gd), g2)
    heads = lambda t: t.reshape(B, T, a_heads, A_HEAD)
    kk = heads(k * kk_s)
    kk = kk / jnp.maximum(jnp.sqrt(jnp.sum(jnp.square(kk), axis=-1, keepdims=True)), 1e-12)
    k_mod = heads(k * (1.0 + (a - 1.0) * ka_s))
    r_h, w_h, v_h, a_h = heads(r), heads(decay), heads(v), heads(a)
    tm = lambda t: jnp.moveaxis(t, 1, 0)

    def step(S, inp):
        r_t, w_t, k_t, v_t, kk_t, a_t = inp
        s_kk = jnp.einsum('bhvk,bhk->bhv', S, kk_t)
        S = S * w_t[:, :, None, :] - s_kk[..., None] * (kk_t * a_t)[:, :, None, :] + v_t[..., None] * k_t[:, :, None, :]
        return S, jnp.einsum('bhvk,bhk->bhv', S, r_t)

    S, o = lax.scan(step, wkv_state, (tm(r_h), tm(w_h), tm(k_mod), tm(v_h), tm(kk), tm(a_h)))
    o = jnp.moveaxis(o, 0, 1)
    mu_o = jnp.mean(o, axis=-1, keepdims=True)
    var_o = jnp.mean(jnp.square(o - mu_o), axis=-1, keepdims=True)
    o = ((o - mu_o) * lax.rsqrt(var_o + RWKV_GN_EPS)).reshape(B, T, a_width) * lnx_g + lnx_b
    bonus = jnp.sum(r_h * k_mod * rk_s.reshape(a_heads, A_HEAD), axis=-1, keepdims=True) * v_h
    y = (o + bonus.reshape(B, T, a_width)) * g
    return y, p_a[:, -1], S


def ab_mixer(x, shift_state, wkv_state, attend, w_in, rw, w_out):
    B, T, d_model = x.shape
    a_width = rw[1].shape[0]
    a_proj = 3 * a_width + W_LORA + AA_LORA + G_LORA
    b_width = d_model - a_width
    b_heads = b_width // B_HEAD
    proj = mm(x, w_in)
    y_a, new_shift, new_wkv = rwkv7_mix(proj[..., :a_proj], shift_state, wkv_state, *rw)
    q, k, v, qi, ki, wi = split_cols(proj[..., a_proj:], [b_width, b_width, b_width, IDX_HEADS * IDX_DIM, IDX_DIM, IDX_HEADS])
    heads = lambda t: t.reshape(B, T, b_heads, B_HEAD)
    q, k, v = heads(q), heads(k), heads(v)
    y_b = attend(q, k, v, qi.reshape(B, T, IDX_HEADS, IDX_DIM), ki, wi * IDX_HEADS ** -0.5)
    y = mm(jnp.concatenate([y_a, y_b.reshape(B, T, b_width)], axis=-1), w_out)
    return y, new_shift, new_wkv, k, v, ki


def gla_chunked(q, k, v, log_f, S0):
    B, T, H, DK = q.shape
    C = min(GLA_CHUNK, T)
    n = -(-T // C)
    pad = n * C - T

    def prep(a):
        a = jnp.pad(a, ((0, 0), (0, pad), (0, 0), (0, 0)))
        return jnp.moveaxis(a.reshape((B, n, C) + a.shape[2:]), 1, 0)

    tril = jnp.tril(jnp.ones((C, C), dtype=bool))

    def step(S, inp):
        qb, kb, vb, fb = inp
        b = jnp.cumsum(fb, axis=1)
        rel = b[:, :, None] - b[:, None, :]
        decay = jnp.exp(jnp.where(tril[None, :, :, None, None], rel, -jnp.inf))
        A = jnp.einsum('bthd,bshd,btshd->bhts', qb, kb, decay)
        o = jnp.einsum('bhts,bshv->bthv', A, vb) + jnp.einsum('bthd,bhdv->bthv', qb * jnp.exp(b), S)
        b_end = b[:, -1]
        S = jnp.exp(b_end)[..., None] * S + jnp.einsum('bshd,bshv->bhdv', kb * jnp.exp(b_end[:, None] - b), vb)
        return S, o

    S, o = lax.scan(step, S0, (prep(q), prep(k), prep(v), prep(log_f)))
    o = jnp.moveaxis(o, 0, 1).reshape(B, n * C, H, v.shape[-1])[:, :T]
    return o, S


def hgrn2_mixer(x, state, w_in, lb, gn_g, w_out):
    B, T, c_width = x.shape
    c_heads = c_width // C_EXPAND
    c_dk = C_EXPAND
    c_dv = c_width // c_heads
    q, f_raw, i, g = split_cols(mm(x, w_in), [c_heads * c_dk, c_heads * c_dk, c_width, c_width])
    log_f = jnp.logaddexp(jnp.log(lb), jnp.log1p(-lb) + jax.nn.log_sigmoid(f_raw))
    k_in = (1.0 - lb) * jax.nn.sigmoid(-f_raw)
    hk = lambda t: t.reshape(B, T, c_heads, c_dk)
    o, new_state = gla_chunked(hk(jax.nn.silu(q)), hk(k_in), i.reshape(B, T, c_heads, c_dv), hk(log_f), state)
    o = o * lax.rsqrt(jnp.mean(jnp.square(o), axis=-1, keepdims=True) + RMS_EPS) * gn_g
    y = o.reshape(B, T, c_width) * jax.nn.silu(g)
    return mm(y, w_out), new_state


def kernel(x_prompt, x_sample, cache_k, cache_v, cache_kidx, state_wkv, state_shift, state_hgrn, page_table, w_in_ab, rwkv_mu, rwkv_w0, rwkv_w2, rwkv_a0, rwkv_a2, rwkv_g2, rwkv_kk, rwkv_ka, rwkv_rk, rwkv_lnx_g, rwkv_lnx_b, w_out_ab, rel_bias, ffn_w_gate, ffn_w_up, ffn_w_down, w_in_c, hgrn_lb, hgrn_gnorm, w_out_c, router, moe_w_gate, moe_w_up, moe_w_down, ln1_g, ln1_b, ln2_g, ln2_b):
    depth = ln1_g.shape[0]
    alpha = (2 * depth) ** 0.25
    a_width = rwkv_w0.shape[1]
    a_heads = a_width // A_HEAD
    a_proj = state_shift.shape[-1]
    c_heads, c_dk, c_dv = state_hgrn.shape[2:]
    lb_all = jnp.cumsum(jax.nn.softmax(hgrn_lb, axis=0), axis=0)
    lb_all = lb_all - lb_all[0:1]
    attend_p = functools.partial(attend_self, rel_bias=rel_bias)
    xp, xs = x_prompt, x_sample
    kp_l, vp_l, kip_l, wkvp_l, shp_l, hgp_l = [], [], [], [], [], []
    ks_l, vs_l, kis_l, wkvs_l, shs_l, hgs_l = [], [], [], [], [], []
    for l in range(depth):
        j = l // 2
        if l % 2 == 0:
            rw = (rwkv_mu[j], rwkv_w0[j], rwkv_w2[j], rwkv_a0[j], rwkv_a2[j], rwkv_g2[j],
                  rwkv_kk[j], rwkv_ka[j], rwkv_rk[j], rwkv_lnx_g[j], rwkv_lnx_b[j])
            attend_s = functools.partial(attend_paged, pool_k=cache_k[j], pool_v=cache_v[j], pool_kidx=cache_kidx[j],
                                         page_table=page_table, rel_bias=rel_bias)
            shift0 = jnp.zeros((xp.shape[0], a_proj), xp.dtype)
            wkv0 = jnp.zeros((xp.shape[0], a_heads, A_HEAD, A_HEAD), jnp.float32)
            yp, shp, wkvp, kp, vp, kip = ab_mixer(xp, shift0, wkv0, attend_p, w_in_ab[j], rw, w_out_ab[j])
            ys, shs, wkvs, kss, vss, kis = ab_mixer(xs, state_shift[j], state_wkv[j], attend_s, w_in_ab[j], rw, w_out_ab[j])
            kp_l.append(kp); vp_l.append(vp); kip_l.append(kip); wkvp_l.append(wkvp); shp_l.append(shp)
            ks_l.append(kss); vs_l.append(vss); kis_l.append(kis); wkvs_l.append(wkvs); shs_l.append(shs)
        else:
            hg0 = jnp.zeros((xp.shape[0], c_heads, c_dk, c_dv), jnp.float32)
            yp, hgp = hgrn2_mixer(xp, hg0, w_in_c[j], lb_all[j], hgrn_gnorm[j], w_out_c[j])
            ys, hgs = hgrn2_mixer(xs, state_hgrn[j], w_in_c[j], lb_all[j], hgrn_gnorm[j], w_out_c[j])
            hgp_l.append(hgp); hgs_l.append(hgs)
        xp = layer_norm(alpha * xp + yp, ln1_g[l], ln1_b[l])
        xs = layer_norm(alpha * xs + ys, ln1_g[l], ln1_b[l])
        if l % 2 == 0:
            fp = swiglu(xp, ffn_w_gate[j], ffn_w_up[j], ffn_w_down[j])
            fs = swiglu(xs, ffn_w_gate[j], ffn_w_up[j], ffn_w_down[j])
        else:
            fp = moe_swiglu(xp, router[j], moe_w_gate[j], moe_w_up[j], moe_w_down[j])
            fs = moe_swiglu(xs, router[j], moe_w_gate[j], moe_w_up[j], moe_w_down[j])
        xp = layer_norm(alpha * xp + fp, ln2_g[l], ln2_b[l])
        xs = layer_norm(alpha * xs + fs, ln2_g[l], ln2_b[l])
    return (xp, xs, jnp.stack(kp_l), jnp.stack(vp_l), jnp.stack(kip_l), jnp.stack(wkvp_l), jnp.stack(shp_l),
            jnp.stack(hgp_l), jnp.stack(ks_l), jnp.stack(vs_l), jnp.stack(kis_l), jnp.stack(wkvs_l),
            jnp.stack(shs_l), jnp.stack(hgs_l))
```

```python
import functools
import math

import jax
import jax.numpy as jnp
import numpy as np
from jax import lax
from jax.experimental import pallas as pl
from jax.experimental.pallas import tpu as pltpu

A_HEAD = 64
W_LORA = 64
AA_LORA = 64
G_LORA = 128
RWKV_GN_EPS = 64e-5
B_HEAD = 128
IDX_HEADS = 16
IDX_DIM = 64
TOPK_MAX = 256
Q_BLOCK = 64
T5_BUCKETS = 32
T5_MAX_DIST = 128
C_EXPAND = 128
GLA_CHUNK = 64
RMS_EPS = 1e-5
N_EXPERTS = 8
TOP_EXPERTS = 2
PAGE_SIZE = 128
LN_EPS = 1e-5

VMEM_LIMIT_BYTES = 56 * 1024 * 1024


def _pick_tile(n, target, align):
    if n <= target:
        return n
    t = (target // align) * align
    while t >= align:
        if n % t == 0:
            return t
        t -= align
    return n


def _matmul_kernel(x_ref, w_ref, o_ref):
    @pl.when(pl.program_id(2) == 0)
    def _():
        o_ref[...] = jnp.zeros_like(o_ref)

    o_ref[...] += jnp.dot(x_ref[...].astype(jnp.bfloat16), w_ref[...].astype(jnp.bfloat16),
                          preferred_element_type=jnp.float32)


def matmul(x, w, tm=512, tn=512, tk=2048):
    m, k = x.shape
    k2, n = w.shape
    assert k == k2
    tm = _pick_tile(m, tm, 8)
    tk = _pick_tile(k, tk, 128)
    tn = min(tn, n) if n % 128 == 0 or n < 128 else tn
    grid = (pl.cdiv(n, tn), pl.cdiv(m, tm), k // tk)
    return pl.pallas_call(
        _matmul_kernel,
        grid=grid,
        in_specs=[pl.BlockSpec((tm, tk), lambda j, i, kk: (i, kk)),
                  pl.BlockSpec((tk, tn), lambda j, i, kk: (kk, j))],
        out_specs=pl.BlockSpec((tm, tn), lambda j, i, kk: (i, j)),
        out_shape=jax.ShapeDtypeStruct((m, n), jnp.float32),
        compiler_params=pltpu.CompilerParams(
            dimension_semantics=("parallel", "parallel", "arbitrary"),
            vmem_limit_bytes=VMEM_LIMIT_BYTES),
        name="matmul",
    )(x, w)


def _swiglu_up_kernel(x_ref, wg_ref, wu_ref, o_ref):
    xb = x_ref[...].astype(jnp.bfloat16)
    g = jnp.dot(xb, wg_ref[...].astype(jnp.bfloat16), preferred_element_type=jnp.float32)
    u = jnp.dot(xb, wu_ref[...].astype(jnp.bfloat16), preferred_element_type=jnp.float32)
    o_ref[...] = g * jax.nn.sigmoid(g) * u


def swiglu_up(x, wg, wu, tm=512, tn=256):
    m, k = x.shape
    n = wg.shape[1]
    tm = _pick_tile(m, tm, 8)
    tn = _pick_tile(n, tn, 128)
    return pl.pallas_call(
        _swiglu_up_kernel,
        grid=(n // tn, m // tm),
        in_specs=[pl.BlockSpec((tm, k), lambda j, i: (i, 0)),
                  pl.BlockSpec((k, tn), lambda j, i: (0, j)),
                  pl.BlockSpec((k, tn), lambda j, i: (0, j))],
        out_specs=pl.BlockSpec((tm, tn), lambda j, i: (i, j)),
        out_shape=jax.ShapeDtypeStruct((m, n), jnp.float32),
        compiler_params=pltpu.CompilerParams(
            dimension_semantics=("parallel", "parallel"),
            vmem_limit_bytes=VMEM_LIMIT_BYTES),
        name="swiglu_up",
    )(x, wg, wu)


def mm(x, w):
    lead = x.shape[:-1]
    return matmul(x.reshape(-1, x.shape[-1]), w).reshape(lead + (w.shape[1],))


def swiglu(x, wg, wu, wd):
    lead = x.shape[:-1]
    h = swiglu_up(x.reshape(-1, x.shape[-1]), wg, wu)
    return matmul(h, wd).reshape(lead + (wd.shape[1],))


INT_MIN = -2 ** 31
NEG_BIG = -1e30


def _sortable_key(x):
    bits = pltpu.bitcast(jnp.where(x == 0.0, 0.0, x), jnp.int32)
    return bits ^ ((bits >> 31) & 0x7FFFFFFF)


def _dsa_prompt_kernel(qi_ref, kit_ref, wi_ref, q_ref, k_ref, v_ref, bias_ref, o_ref,
                       keys_ref, tau_ref, qib_ref, qb_ref, m_ref, l_ref, acc_ref, *, topk, n_idx_bits):
    i = pl.program_id(1)
    j = pl.program_id(2)
    tq = q_ref.shape[1]
    n_heads = q_ref.shape[2] // B_HEAD
    n_lane_tiles = tq // 128
    kf = float(topk)

    @pl.when(j == 0)
    def _select():
        qib_ref[...] = (qi_ref[0] * IDX_DIM ** -0.5).astype(jnp.bfloat16)
        qb_ref[...] = q_ref[0].astype(jnp.bfloat16)
        m_ref[...] = jnp.full_like(m_ref, NEG_BIG)
        l_ref[...] = jnp.zeros_like(l_ref)
        acc_ref[...] = jnp.zeros_like(acc_ref)
        wb = (wi_ref[0] * IDX_HEADS ** -0.5).astype(jnp.bfloat16).astype(jnp.float32)
        row = i * tq + lax.broadcasted_iota(jnp.int32, (tq, tq), 0)
        col = lax.broadcasted_iota(jnp.int32, (tq, tq), 1)

        def score_chunk(c, carry):
            kc = kit_ref[0, :, pl.ds(pl.multiple_of(c * tq, tq), tq)]
            score = jnp.zeros((tq, tq), jnp.float32)
            for h in range(IDX_HEADS):
                s = jnp.dot(qib_ref[:, h * IDX_DIM:(h + 1) * IDX_DIM], kc, preferred_element_type=jnp.float32)
                r = jnp.maximum(s, 0.0).astype(jnp.bfloat16).astype(jnp.float32)
                score = score + wb[:, h:h + 1] * r
            keys_ref[c] = jnp.where(c * tq + col <= row, _sortable_key(score), INT_MIN)
            return carry

        lax.fori_loop(0, i + 1, score_chunk, 0)

        def count(pred):
            def body(c, acc):
                hit = jnp.where(pred(keys_ref[c], c), 1.0, 0.0)
                part = hit[:, :128]
                for u in range(1, n_lane_tiles):
                    part = part + hit[:, u * 128:(u + 1) * 128]
                return acc + part
            acc = lax.fori_loop(0, i + 1, body, jnp.zeros((tq, 128), jnp.float32))
            return jnp.sum(acc, axis=1, keepdims=True)

        tau = jnp.where(count(lambda kc, c: kc >= 0) >= kf, 0, INT_MIN).astype(jnp.int32)

        def bit_step(n, tau):
            cand = tau | jnp.left_shift(jnp.int32(1), 30 - n)
            return jnp.where(count(lambda kc, c: kc >= cand) >= kf, cand, tau)

        tau = lax.fori_loop(0, 31, bit_step, tau)
        tau = jnp.maximum(tau, INT_MIN + 1)
        tau_ref[...] = tau
        n_ge = count(lambda kc, c: kc >= tau)

        @pl.when(jnp.max(n_ge) > kf)
        def _ties():
            need = kf - count(lambda kc, c: kc > tau)

            def idx_step(n, ans):
                cand = ans + jnp.left_shift(jnp.int32(1), n_idx_bits - 1 - n)
                f = count(lambda kc, c: (kc == tau) & (c * tq + col < cand))
                return jnp.where(f < need, cand, ans)

            last = lax.fori_loop(0, n_idx_bits, idx_step, jnp.zeros((tq, 1), jnp.int32))

            def demote(c, carry):
                kc = keys_ref[c]
                keys_ref[c] = jnp.where((kc == tau) & (c * tq + col > last), INT_MIN, kc)
                return carry

            lax.fori_loop(0, i + 1, demote, 0)

    @pl.when(j <= i)
    def _attend():
        sel = keys_ref[j] >= tau_ref[...]
        for h in range(n_heads):
            hs = slice(h * B_HEAD, (h + 1) * B_HEAD)
            kh = k_ref[0, :, hs].astype(jnp.bfloat16)
            vh = v_ref[0, :, hs].astype(jnp.bfloat16)
            logits = lax.dot_general(qb_ref[:, hs], kh, (((1,), (1,)), ((), ())),
                                     preferred_element_type=jnp.float32)
            logits = logits * B_HEAD ** -0.5 + bias_ref[0, h]
            logits = jnp.where(sel, logits, NEG_BIG)
            m_old = m_ref[h]
            m_new = jnp.maximum(m_old, jnp.max(logits, axis=1, keepdims=True))
            alpha = jnp.exp(m_old - m_new)
            p = jnp.exp(logits - m_new)
            l_ref[h] = alpha * l_ref[h] + jnp.sum(p, axis=1, keepdims=True)
            acc_ref[:, hs] = alpha * acc_ref[:, hs] + jnp.dot(p.astype(jnp.bfloat16), vh,
                                                              preferred_element_type=jnp.float32)
            m_ref[h] = m_new

    @pl.when(j == i)
    def _finish():
        for h in range(n_heads):
            hs = slice(h * B_HEAD, (h + 1) * B_HEAD)
            o_ref[0, :, hs] = acc_ref[:, hs] / l_ref[h]


def dsa_prompt(q, k, v, qi, ki, wi, rel_bias, tq=256):
    B, T, b_width = q.shape
    n_heads = b_width // B_HEAD
    tq = min(tq, T)
    assert T % tq == 0 and tq % 128 == 0 and tq + 1 >= T5_MAX_DIST
    nq = T // tq
    topk = min(TOPK_MAX, T // 4)
    kit = jnp.swapaxes(ki, 1, 2).astype(jnp.bfloat16)
    r = jnp.arange(tq, dtype=jnp.int32)
    dist = jnp.arange(3, dtype=jnp.int32)[:, None, None] * tq + r[None, :, None] - r[None, None, :]
    bias3 = jnp.transpose(rel_bias[t5_bucket(dist)], (0, 3, 1, 2))
    kern = functools.partial(_dsa_prompt_kernel, topk=topk, n_idx_bits=max(1, (T - 1).bit_length()))
    return pl.pallas_call(
        kern,
        grid=(B, nq, nq),
        in_specs=[pl.BlockSpec((1, tq, qi.shape[2]), lambda b, i, j: (b, i, 0)),
                  pl.BlockSpec((1, IDX_DIM, T), lambda b, i, j: (b, 0, 0)),
                  pl.BlockSpec((1, tq, IDX_HEADS), lambda b, i, j: (b, i, 0)),
                  pl.BlockSpec((1, tq, b_width), lambda b, i, j: (b, i, 0)),
                  pl.BlockSpec((1, tq, b_width), lambda b, i, j: (b, jnp.minimum(j, i), 0)),
                  pl.BlockSpec((1, tq, b_width), lambda b, i, j: (b, jnp.minimum(j, i), 0)),
                  pl.BlockSpec((1, n_heads, tq, tq), lambda b, i, j: (jnp.clip(i - j, 0, 2), 0, 0, 0))],
        out_specs=pl.BlockSpec((1, tq, b_width), lambda b, i, j: (b, i, 0)),
        out_shape=jax.ShapeDtypeStruct((B, T, b_width), jnp.float32),
        scratch_shapes=[pltpu.VMEM((nq, tq, tq), jnp.int32),
                        pltpu.VMEM((tq, 1), jnp.int32),
                        pltpu.VMEM((tq, qi.shape[2]), jnp.bfloat16),
                        pltpu.VMEM((tq, b_width), jnp.bfloat16),
                        pltpu.VMEM((n_heads, tq, 1), jnp.float32),
                        pltpu.VMEM((n_heads, tq, 1), jnp.float32),
                        pltpu.VMEM((tq, b_width), jnp.float32)],
        compiler_params=pltpu.CompilerParams(
            dimension_semantics=("parallel", "arbitrary", "arbitrary"),
            vmem_limit_bytes=VMEM_LIMIT_BYTES),
        name="dsa_prompt",
    )(qi, kit, wi, q, k, v, bias3)


SCAN_CHUNK = A_HEAD
LANES = 128
SUBLANES = 8


def _rwkv_scan_kernel(v_ref, kk_ref, kka_ref, w_ref, km_ref, r_ref, s0_ref, o_ref, st_ref, s_scr, *, n_steps):
    c = pl.program_id(0)
    n_batch = v_ref.shape[0]
    n_tiles = v_ref.shape[2] // LANES

    @pl.when(c == 0)
    def _():
        s_scr[...] = s0_ref[...]

    if n_steps < SCAN_CHUNK:
        o_ref[...] = jnp.zeros_like(o_ref)
    seg_base = lax.broadcasted_iota(jnp.int32, (A_HEAD, LANES), 1) & (LANES - A_HEAD)

    group = min(n_steps, SUBLANES)

    def step_group(g, carry):
        t0 = g * group if isinstance(g, int) else pl.multiple_of(g * group, group)
        for b in range(n_batch):
            for p in range(n_tiles):
                ls = slice(p * LANES, (p + 1) * LANES)
                s = s_scr[b, :, ls]
                v_rows = v_ref[b, pl.ds(t0, group), ls]
                o_rows = []
                for u in range(group):
                    idx = seg_base + (t0 + u)
                    col = lambda ref: jnp.take_along_axis(ref[b, 0, :, ls], idx, axis=1)
                    s_kk = jnp.sum(s * col(kk_ref), axis=0, keepdims=True)
                    s = s * col(w_ref) - s_kk * col(kka_ref) + v_rows[u:u + 1] * col(km_ref)
                    o_rows.append(jnp.sum(s * col(r_ref), axis=0, keepdims=True))
                o_ref[b, pl.ds(t0, group), ls] = jnp.concatenate(o_rows, axis=0)
                s_scr[b, :, ls] = s
        return carry

    if n_steps == group:
        step_group(0, 0)
    else:
        lax.fori_loop(0, n_steps // group, step_group, 0)

    @pl.when(c == pl.num_programs(0) - 1)
    def _():
        st_ref[...] = s_scr[...]


def _time_transposed(x, n_chunks):
    B, _, width = x.shape
    x = x.reshape(B, n_chunks, SCAN_CHUNK, width // A_HEAD, A_HEAD)
    return jnp.transpose(x, (0, 1, 4, 3, 2)).reshape(B, n_chunks, A_HEAD, width)


def rwkv_scan(r, w, k, v, kk, a, state):
    B, T, width = r.shape
    n_heads = width // A_HEAD
    assert SCAN_CHUNK == A_HEAD and (T % SCAN_CHUNK == 0 or T < SCAN_CHUNK)
    n_chunks = pl.cdiv(T, SCAN_CHUNK)
    t_pad = n_chunks * SCAN_CHUNK
    pad = lambda x: jnp.pad(x, ((0, 0), (0, t_pad - T), (0, 0)))
    cols = [_time_transposed(pad(x), n_chunks) for x in (kk, kk * a, w, k, r)]
    st0 = jnp.transpose(state, (0, 3, 1, 2)).reshape(B, A_HEAD, width)
    row_spec = pl.BlockSpec((B, SCAN_CHUNK, width), lambda c: (0, c, 0))
    col_spec = pl.BlockSpec((B, 1, A_HEAD, width), lambda c: (0, c, 0, 0))
    st_spec = pl.BlockSpec((B, A_HEAD, width), lambda c: (0, 0, 0))
    o, st = pl.pallas_call(
        functools.partial(_rwkv_scan_kernel, n_steps=min(T, SCAN_CHUNK)),
        grid=(n_chunks,),
        in_specs=[row_spec] + [col_spec] * 5 + [st_spec],
        out_specs=[row_spec, st_spec],
        out_shape=[jax.ShapeDtypeStruct((B, t_pad, width), jnp.float32),
                   jax.ShapeDtypeStruct((B, A_HEAD, width), jnp.float32)],
        scratch_shapes=[pltpu.VMEM((B, A_HEAD, width), jnp.float32)],
        compiler_params=pltpu.CompilerParams(
            dimension_semantics=("arbitrary",),
            vmem_limit_bytes=VMEM_LIMIT_BYTES),
        name="rwkv_scan",
    )(pad(v), *cols, st0)
    new_state = jnp.transpose(st.reshape(B, A_HEAD, n_heads, A_HEAD), (0, 2, 3, 1))
    return o[:, :T], new_state


def split_cols(a, sizes):
    return jnp.split(a, np.cumsum(sizes)[:-1].tolist(), axis=-1)


def gather_rows(a, idx):
    return jax.vmap(lambda a_b, i_b: a_b[i_b])(a, idx)


def layer_norm(x, g, b):
    mu = jnp.mean(x, axis=-1, keepdims=True)
    var = jnp.mean(jnp.square(x - mu), axis=-1, keepdims=True)
    return (x - mu) * lax.rsqrt(var + LN_EPS) * g + b


def moe_swiglu(x, w_router, wg, wu, wd):
    probs = jax.nn.softmax(mm(x, w_router), axis=-1)
    top_p, top_i = lax.top_k(probs, TOP_EXPERTS)
    top_p = top_p / jnp.sum(top_p, axis=-1, keepdims=True)
    gates = jnp.sum(jax.nn.one_hot(top_i, N_EXPERTS, dtype=jnp.float32) * top_p[..., None], axis=-2)
    y = jnp.zeros_like(x)
    for e in range(N_EXPERTS):
        y = y + gates[..., e:e + 1] * swiglu(x, wg[e], wu[e], wd[e])
    return y


def t5_bucket(dist):
    dist = jnp.maximum(dist, 0)
    max_exact = T5_BUCKETS // 2
    d = jnp.maximum(dist, 1).astype(jnp.float32)
    large = max_exact + (jnp.log(d / max_exact) / math.log(T5_MAX_DIST / max_exact) * (T5_BUCKETS - max_exact)).astype(jnp.int32)
    large = jnp.minimum(large, T5_BUCKETS - 1)
    return jnp.where(dist < max_exact, dist, large)


def dsa_attend(q, qi, wi, q_pos, k_idx, gather_kv, rel_bias, topk):
    B, T = q.shape[:2]
    L = k_idx.shape[1]
    blk = min(Q_BLOCK, T)
    nb = -(-T // blk)
    pad = nb * blk - T

    def to_blocks(a):
        a = jnp.pad(a, [(0, 0), (0, pad)] + [(0, 0)] * (a.ndim - 2))
        return jnp.moveaxis(a.reshape((B, nb, blk) + a.shape[2:]), 1, 0)

    pos_b = jnp.pad(q_pos, (0, pad)).reshape(nb, blk)
    k_pos = jnp.arange(L, dtype=jnp.int32)

    def one_block(args):
        qb, qib, wib, pb = args
        s = jnp.einsum('bthd,bsd->bths', qib, k_idx) * IDX_DIM ** -0.5
        score = jnp.einsum('bth,bths->bts', wib, jax.nn.relu(s))
        score = jnp.where(k_pos[None, None, :] <= pb[None, :, None], score, -jnp.inf)
        _, sel = lax.top_k(score, topk)
        valid = sel <= pb[None, :, None]
        k_sel, v_sel = gather_kv(sel)
        logits = jnp.einsum('bthd,btkhd->bhtk', qb, k_sel) * B_HEAD ** -0.5
        bias = rel_bias[t5_bucket(pb[None, :, None] - sel)]
        logits = logits + jnp.transpose(bias, (0, 3, 1, 2))
        logits = jnp.where(valid[:, None], logits, -jnp.inf)
        p = jax.nn.softmax(logits, axis=-1)
        return jnp.einsum('bhtk,btkhd->bthd', p, v_sel)

    out = lax.map(one_block, (to_blocks(q), to_blocks(qi), to_blocks(wi), pos_b))
    return jnp.moveaxis(out, 0, 1).reshape((B, nb * blk) + q.shape[2:])[:, :T]


def attend_self(q, k, v, qi, ki, wi, rel_bias):
    T = q.shape[1]
    gather = lambda sel: (gather_rows(k, sel), gather_rows(v, sel))
    return dsa_attend(q, qi, wi, jnp.arange(T, dtype=jnp.int32), ki, gather, rel_bias, min(TOPK_MAX, T // 4))


def attend_paged(q, k, v, qi, ki, wi, pool_k, pool_v, pool_kidx, page_table, rel_bias):
    B, T = q.shape[:2]
    past = page_table.shape[1] * PAGE_SIZE
    kidx_past = pool_kidx[page_table].reshape(B, past, IDX_DIM)
    kidx_all = jnp.concatenate([kidx_past, ki], axis=1)
    flat_k = pool_k.reshape((-1,) + pool_k.shape[2:])
    flat_v = pool_v.reshape((-1,) + pool_v.shape[2:])

    def gather(sel):
        ps = jnp.minimum(sel, past - 1)
        row = gather_rows(page_table, ps // PAGE_SIZE) * PAGE_SIZE + ps % PAGE_SIZE
        is_new = (sel >= past)[..., None, None]
        ns = jnp.clip(sel - past, 0, T - 1)
        k_sel = jnp.where(is_new, gather_rows(k, ns), flat_k[row])
        v_sel = jnp.where(is_new, gather_rows(v, ns), flat_v[row])
        return k_sel, v_sel

    q_pos = past + jnp.arange(T, dtype=jnp.int32)
    return dsa_attend(q, qi, wi, q_pos, kidx_all, gather, rel_bias, min(TOPK_MAX, (past + T) // 4))


def rwkv7_mix(p_a, shift_state, wkv_state, mu, w0, w2, a0, a2, g2, kk_s, ka_s, rk_s, lnx_g, lnx_b):
    B, T, _ = p_a.shape
    a_width = w0.shape[0]
    a_heads = a_width // A_HEAD
    prev = jnp.concatenate([shift_state[:, None, :], p_a[:, :-1]], axis=1)
    xm = p_a + (prev - p_a) * mu
    r, k, v, wd, ad, gd = split_cols(xm, [a_width, a_width, a_width, W_LORA, AA_LORA, G_LORA])
    w_pre = w0 + mm(jnp.tanh(wd), w2)
    decay = jnp.exp(-jnp.exp(-jax.nn.softplus(-w_pre) - 0.5))
    a = jax.nn.sigmoid(a0 + mm(ad, a2))
    g = mm(jax.nn.sigmoid(gd), g2)
    heads = lambda t: t.reshape(B, T, a_heads, A_HEAD)
    kk = heads(k * kk_s)
    kk = kk / jnp.maximum(jnp.sqrt(jnp.sum(jnp.square(kk), axis=-1, keepdims=True)), 1e-12)
    k_mod = heads(k * (1.0 + (a - 1.0) * ka_s))
    r_h, v_h = heads(r), heads(v)
    o, S = rwkv_scan(r, decay, k_mod.reshape(B, T, a_width), v, kk.reshape(B, T, a_width), a, wkv_state)
    o = heads(o)
    mu_o = jnp.mean(o, axis=-1, keepdims=True)
    var_o = jnp.mean(jnp.square(o - mu_o), axis=-1, keepdims=True)
    o = ((o - mu_o) * lax.rsqrt(var_o + RWKV_GN_EPS)).reshape(B, T, a_width) * lnx_g + lnx_b
    bonus = jnp.sum(r_h * k_mod * rk_s.reshape(a_heads, A_HEAD), axis=-1, keepdims=True) * v_h
    y = (o + bonus.reshape(B, T, a_width)) * g
    return y, p_a[:, -1], S


def ab_mixer(x, shift_state, wkv_state, attend, w_in, rw, w_out):
    B, T, d_model = x.shape
    a_width = rw[1].shape[0]
    a_proj = 3 * a_width + W_LORA + AA_LORA + G_LORA
    b_width = d_model - a_width
    b_heads = b_width // B_HEAD
    proj = mm(x, w_in)
    y_a, new_shift, new_wkv = rwkv7_mix(proj[..., :a_proj], shift_state, wkv_state, *rw)
    q, k, v, qi, ki, wi = split_cols(proj[..., a_proj:], [b_width, b_width, b_width, IDX_HEADS * IDX_DIM, IDX_DIM, IDX_HEADS])
    y_b = attend(q, k, v, qi, ki, wi)
    y = mm(jnp.concatenate([y_a, y_b], axis=-1), w_out)
    heads = lambda t: t.reshape(B, T, b_heads, B_HEAD)
    return y, new_shift, new_wkv, heads(k), heads(v), ki


def attend_sample(q, k, v, qi, ki, wi, **kw):
    B, T, b_width = q.shape
    heads = lambda t: t.reshape(B, T, b_width // B_HEAD, B_HEAD)
    y = attend_paged(heads(q), heads(k), heads(v), qi.reshape(B, T, IDX_HEADS, IDX_DIM), ki, wi * IDX_HEADS ** -0.5, **kw)
    return y.reshape(B, T, b_width)


def gla_chunked(q, k, v, log_f, S0):
    B, T, H, DK = q.shape
    C = min(GLA_CHUNK, T)
    n = -(-T // C)
    pad = n * C - T

    def prep(a):
        a = jnp.pad(a, ((0, 0), (0, pad), (0, 0), (0, 0)))
        return jnp.moveaxis(a.reshape((B, n, C) + a.shape[2:]), 1, 0)

    tril = jnp.tril(jnp.ones((C, C), dtype=bool))

    def step(S, inp):
        qb, kb, vb, fb = inp
        b = jnp.cumsum(fb, axis=1)
        rel = b[:, :, None] - b[:, None, :]
        decay = jnp.exp(jnp.where(tril[None, :, :, None, None], rel, -jnp.inf))
        A = jnp.einsum('bthd,bshd,btshd->bhts', qb, kb, decay)
        o = jnp.einsum('bhts,bshv->bthv', A, vb) + jnp.einsum('bthd,bhdv->bthv', qb * jnp.exp(b), S)
        b_end = b[:, -1]
        S = jnp.exp(b_end)[..., None] * S + jnp.einsum('bshd,bshv->bhdv', kb * jnp.exp(b_end[:, None] - b), vb)
        return S, o

    S, o = lax.scan(step, S0, (prep(q), prep(k), prep(v), prep(log_f)))
    o = jnp.moveaxis(o, 0, 1).reshape(B, n * C, H, v.shape[-1])[:, :T]
    return o, S


def hgrn2_mixer(x, state, w_in, lb, gn_g, w_out):
    B, T, c_width = x.shape
    c_heads = c_width // C_EXPAND
    c_dk = C_EXPAND
    c_dv = c_width // c_heads
    q, f_raw, i, g = split_cols(mm(x, w_in), [c_heads * c_dk, c_heads * c_dk, c_width, c_width])
    log_f = jnp.logaddexp(jnp.log(lb), jnp.log1p(-lb) + jax.nn.log_sigmoid(f_raw))
    k_in = (1.0 - lb) * jax.nn.sigmoid(-f_raw)
    hk = lambda t: t.reshape(B, T, c_heads, c_dk)
    o, new_state = gla_chunked(hk(jax.nn.silu(q)), hk(k_in), i.reshape(B, T, c_heads, c_dv), hk(log_f), state)
    o = o * lax.rsqrt(jnp.mean(jnp.square(o), axis=-1, keepdims=True) + RMS_EPS) * gn_g
    y = o.reshape(B, T, c_width) * jax.nn.silu(g)
    return mm(y, w_out), new_state


def kernel(x_prompt, x_sample, cache_k, cache_v, cache_kidx, state_wkv, state_shift, state_hgrn, page_table, w_in_ab, rwkv_mu, rwkv_w0, rwkv_w2, rwkv_a0, rwkv_a2, rwkv_g2, rwkv_kk, rwkv_ka, rwkv_rk, rwkv_lnx_g, rwkv_lnx_b, w_out_ab, rel_bias, ffn_w_gate, ffn_w_up, ffn_w_down, w_in_c, hgrn_lb, hgrn_gnorm, w_out_c, router, moe_w_gate, moe_w_up, moe_w_down, ln1_g, ln1_b, ln2_g, ln2_b):
    depth = ln1_g.shape[0]
    alpha = (2 * depth) ** 0.25
    a_width = rwkv_w0.shape[1]
    a_heads = a_width // A_HEAD
    a_proj = state_shift.shape[-1]
    c_heads, c_dk, c_dv = state_hgrn.shape[2:]
    lb_all = jnp.cumsum(jax.nn.softmax(hgrn_lb, axis=0), axis=0)
    lb_all = lb_all - lb_all[0:1]
    attend_p = functools.partial(dsa_prompt, rel_bias=rel_bias)
    xp, xs = x_prompt, x_sample
    kp_l, vp_l, kip_l, wkvp_l, shp_l, hgp_l = [], [], [], [], [], []
    ks_l, vs_l, kis_l, wkvs_l, shs_l, hgs_l = [], [], [], [], [], []
    for l in range(depth):
        j = l // 2
        if l % 2 == 0:
            rw = (rwkv_mu[j], rwkv_w0[j], rwkv_w2[j], rwkv_a0[j], rwkv_a2[j], rwkv_g2[j],
                  rwkv_kk[j], rwkv_ka[j], rwkv_rk[j], rwkv_lnx_g[j], rwkv_lnx_b[j])
            attend_s = functools.partial(attend_sample, pool_k=cache_k[j], pool_v=cache_v[j], pool_kidx=cache_kidx[j],
                                         page_table=page_table, rel_bias=rel_bias)
            shift0 = jnp.zeros((xp.shape[0], a_proj), xp.dtype)
            wkv0 = jnp.zeros((xp.shape[0], a_heads, A_HEAD, A_HEAD), jnp.float32)
            yp, shp, wkvp, kp, vp, kip = ab_mixer(xp, shift0, wkv0, attend_p, w_in_ab[j], rw, w_out_ab[j])
            ys, shs, wkvs, kss, vss, kis = ab_mixer(xs, state_shift[j], state_wkv[j], attend_s, w_in_ab[j], rw, w_out_ab[j])
            kp_l.append(kp); vp_l.append(vp); kip_l.append(kip); wkvp_l.append(wkvp); shp_l.append(shp)
            ks_l.append(kss); vs_l.append(vss); kis_l.append(kis); wkvs_l.append(wkvs); shs_l.append(shs)
        else:
            hg0 = jnp.zeros((xp.shape[0], c_heads, c_dk, c_dv), jnp.float32)
            yp, hgp = hgrn2_mixer(xp, hg0, w_in_c[j], lb_all[j], hgrn_gnorm[j], w_out_c[j])
            ys, hgs = hgrn2_mixer(xs, state_hgrn[j], w_in_c[j], lb_all[j], hgrn_gnorm[j], w_out_c[j])
            hgp_l.append(hgp); hgs_l.append(hgs)
        xp = layer_norm(alpha * xp + yp, ln1_g[l], ln1_b[l])
        xs = layer_norm(alpha * xs + ys, ln1_g[l], ln1_b[l])
        if l % 2 == 0:
            fp = swiglu(xp, ffn_w_gate[j], ffn_w_up[j], ffn_w_down[j])
            fs = swiglu(xs, ffn_w_gate[j], ffn_w_up[j], ffn_w_down[j])
        else:
            fp = moe_swiglu(xp, router[j], moe_w_gate[j], moe_w_up[j], moe_w_down[j])
            fs = moe_swiglu(xs, router[j], moe_w_gate[j], moe_w_up[j], moe_w_down[j])
        xp = layer_norm(alpha * xp + fp, ln2_g[l], ln2_b[l])
        xs = layer_norm(alpha * xs + fs, ln2_g[l], ln2_b[l])
    return (xp, xs, jnp.stack(kp_l), jnp.stack(vp_l), jnp.stack(kip_l), jnp.stack(wkvp_l), jnp.stack(shp_l),
            jnp.stack(hgp_l), jnp.stack(ks_l), jnp.stack(vs_l), jnp.stack(kis_l), jnp.stack(wkvs_l),
            jnp.stack(shs_l), jnp.stack(hgs_l))
```

```python
import functools
import math

import jax
import jax.numpy as jnp
import numpy as np
from jax import lax
from jax.experimental import pallas as pl
from jax.experimental.pallas import tpu as pltpu

A_HEAD = 64
W_LORA = 64
AA_LORA = 64
G_LORA = 128
RWKV_GN_EPS = 64e-5
B_HEAD = 128
IDX_HEADS = 16
IDX_DIM = 64
TOPK_MAX = 256
Q_BLOCK = 64
T5_BUCKETS = 32
T5_MAX_DIST = 128
C_EXPAND = 128
GLA_CHUNK = 64
RMS_EPS = 1e-5
N_EXPERTS = 8
TOP_EXPERTS = 2
PAGE_SIZE = 128
LN_EPS = 1e-5

VMEM_LIMIT_BYTES = 56 * 1024 * 1024


def _pick_tile(n, target, align):
    if n <= target:
        return n
    t = (target // align) * align
    while t >= align:
        if n % t == 0:
            return t
        t -= align
    return n


def _matmul_kernel(x_ref, w_ref, o_ref):
    @pl.when(pl.program_id(2) == 0)
    def _():
        o_ref[...] = jnp.zeros_like(o_ref)

    o_ref[...] += jnp.dot(x_ref[...].astype(jnp.bfloat16), w_ref[...].astype(jnp.bfloat16),
                          preferred_element_type=jnp.float32)


def matmul(x, w, tm=512, tn=512, tk=2048):
    m, k = x.shape
    k2, n = w.shape
    assert k == k2
    tm = _pick_tile(m, tm, 8)
    tk = _pick_tile(k, tk, 128)
    tn = min(tn, n) if n % 128 == 0 or n < 128 else tn
    grid = (pl.cdiv(n, tn), pl.cdiv(m, tm), k // tk)
    return pl.pallas_call(
        _matmul_kernel,
        grid=grid,
        in_specs=[pl.BlockSpec((tm, tk), lambda j, i, kk: (i, kk)),
                  pl.BlockSpec((tk, tn), lambda j, i, kk: (kk, j))],
        out_specs=pl.BlockSpec((tm, tn), lambda j, i, kk: (i, j)),
        out_shape=jax.ShapeDtypeStruct((m, n), jnp.float32),
        compiler_params=pltpu.CompilerParams(
            dimension_semantics=("parallel", "parallel", "arbitrary"),
            vmem_limit_bytes=VMEM_LIMIT_BYTES),
        name="matmul",
    )(x, w)


def _swiglu_up_kernel(x_ref, wg_ref, wu_ref, o_ref):
    xb = x_ref[...].astype(jnp.bfloat16)
    g = jnp.dot(xb, wg_ref[...].astype(jnp.bfloat16), preferred_element_type=jnp.float32)
    u = jnp.dot(xb, wu_ref[...].astype(jnp.bfloat16), preferred_element_type=jnp.float32)
    o_ref[...] = g * jax.nn.sigmoid(g) * u


def swiglu_up(x, wg, wu, tm=512, tn=256):
    m, k = x.shape
    n = wg.shape[1]
    tm = _pick_tile(m, tm, 8)
    tn = _pick_tile(n, tn, 128)
    return pl.pallas_call(
        _swiglu_up_kernel,
        grid=(n // tn, m // tm),
        in_specs=[pl.BlockSpec((tm, k), lambda j, i: (i, 0)),
                  pl.BlockSpec((k, tn), lambda j, i: (0, j)),
                  pl.BlockSpec((k, tn), lambda j, i: (0, j))],
        out_specs=pl.BlockSpec((tm, tn), lambda j, i: (i, j)),
        out_shape=jax.ShapeDtypeStruct((m, n), jnp.float32),
        compiler_params=pltpu.CompilerParams(
            dimension_semantics=("parallel", "parallel"),
            vmem_limit_bytes=VMEM_LIMIT_BYTES),
        name="swiglu_up",
    )(x, wg, wu)


def mm(x, w):
    lead = x.shape[:-1]
    return matmul(x.reshape(-1, x.shape[-1]), w).reshape(lead + (w.shape[1],))


def swiglu(x, wg, wu, wd):
    lead = x.shape[:-1]
    h = swiglu_up(x.reshape(-1, x.shape[-1]), wg, wu)
    return matmul(h, wd).reshape(lead + (wd.shape[1],))


INT_MIN = -2 ** 31
NEG_BIG = -1e30


def _sortable_key(x):
    bits = pltpu.bitcast(jnp.where(x == 0.0, 0.0, x), jnp.int32)
    return bits ^ ((bits >> 31) & 0x7FFFFFFF)


def _dsa_prompt_kernel(qi_ref, kit_ref, wi_ref, q_ref, k_ref, v_ref, bias_ref, o_ref,
                       keys_ref, tau_ref, qib_ref, qb_ref, m_ref, l_ref, acc_ref, *, topk, n_idx_bits):
    i = pl.program_id(1)
    j = pl.program_id(2)
    tq = q_ref.shape[1]
    n_heads = q_ref.shape[2] // B_HEAD
    n_lane_tiles = tq // 128
    kf = float(topk)

    @pl.when(j == 0)
    def _select():
        qib_ref[...] = (qi_ref[0] * IDX_DIM ** -0.5).astype(jnp.bfloat16)
        qb_ref[...] = q_ref[0].astype(jnp.bfloat16)
        m_ref[...] = jnp.full_like(m_ref, NEG_BIG)
        l_ref[...] = jnp.zeros_like(l_ref)
        acc_ref[...] = jnp.zeros_like(acc_ref)
        wb = (wi_ref[0] * IDX_HEADS ** -0.5).astype(jnp.bfloat16).astype(jnp.float32)
        row = i * tq + lax.broadcasted_iota(jnp.int32, (tq, tq), 0)
        col = lax.broadcasted_iota(jnp.int32, (tq, tq), 1)

        def score_chunk(c, carry):
            kc = kit_ref[0, :, pl.ds(pl.multiple_of(c * tq, tq), tq)]
            score = jnp.zeros((tq, tq), jnp.float32)
            for h in range(IDX_HEADS):
                s = jnp.dot(qib_ref[:, h * IDX_DIM:(h + 1) * IDX_DIM], kc, preferred_element_type=jnp.float32)
                r = jnp.maximum(s, 0.0).astype(jnp.bfloat16).astype(jnp.float32)
                score = score + wb[:, h:h + 1] * r
            keys_ref[c] = jnp.where(c * tq + col <= row, _sortable_key(score), INT_MIN)
            return carry

        lax.fori_loop(0, i + 1, score_chunk, 0)

        def count(pred):
            def body(c, acc):
                hit = jnp.where(pred(keys_ref[c], c), 1.0, 0.0)
                part = hit[:, :128]
                for u in range(1, n_lane_tiles):
                    part = part + hit[:, u * 128:(u + 1) * 128]
                return acc + part
            acc = lax.fori_loop(0, i + 1, body, jnp.zeros((tq, 128), jnp.float32))
            return jnp.sum(acc, axis=1, keepdims=True)

        tau = jnp.where(count(lambda kc, c: kc >= 0) >= kf, 0, INT_MIN).astype(jnp.int32)

        def bit_step(n, tau):
            cand = tau | jnp.left_shift(jnp.int32(1), 30 - n)
            return jnp.where(count(lambda kc, c: kc >= cand) >= kf, cand, tau)

        tau = lax.fori_loop(0, 31, bit_step, tau)
        tau = jnp.maximum(tau, INT_MIN + 1)
        tau_ref[...] = tau
        n_ge = count(lambda kc, c: kc >= tau)

        @pl.when(jnp.max(n_ge) > kf)
        def _ties():
            need = kf - count(lambda kc, c: kc > tau)

            def idx_step(n, ans):
                cand = ans + jnp.left_shift(jnp.int32(1), n_idx_bits - 1 - n)
                f = count(lambda kc, c: (kc == tau) & (c * tq + col < cand))
                return jnp.where(f < need, cand, ans)

            last = lax.fori_loop(0, n_idx_bits, idx_step, jnp.zeros((tq, 1), jnp.int32))

            def demote(c, carry):
                kc = keys_ref[c]
                keys_ref[c] = jnp.where((kc == tau) & (c * tq + col > last), INT_MIN, kc)
                return carry

            lax.fori_loop(0, i + 1, demote, 0)

    @pl.when(j <= i)
    def _attend():
        sel = keys_ref[j] >= tau_ref[...]
        for h in range(n_heads):
            hs = slice(h * B_HEAD, (h + 1) * B_HEAD)
            kh = k_ref[0, :, hs].astype(jnp.bfloat16)
            vh = v_ref[0, :, hs].astype(jnp.bfloat16)
            logits = lax.dot_general(qb_ref[:, hs], kh, (((1,), (1,)), ((), ())),
                                     preferred_element_type=jnp.float32)
            logits = logits * B_HEAD ** -0.5 + bias_ref[0, h]
            logits = jnp.where(sel, logits, NEG_BIG)
            m_old = m_ref[h]
            m_new = jnp.maximum(m_old, jnp.max(logits, axis=1, keepdims=True))
            alpha = jnp.exp(m_old - m_new)
            p = jnp.exp(logits - m_new)
            l_ref[h] = alpha * l_ref[h] + jnp.sum(p, axis=1, keepdims=True)
            acc_ref[:, hs] = alpha * acc_ref[:, hs] + jnp.dot(p.astype(jnp.bfloat16), vh,
                                                              preferred_element_type=jnp.float32)
            m_ref[h] = m_new

    @pl.when(j == i)
    def _finish():
        for h in range(n_heads):
            hs = slice(h * B_HEAD, (h + 1) * B_HEAD)
            o_ref[0, :, hs] = acc_ref[:, hs] / l_ref[h]


def dsa_prompt(q, k, v, qi, ki, wi, rel_bias, tq=256):
    B, T, b_width = q.shape
    n_heads = b_width // B_HEAD
    tq = min(tq, T)
    assert T % tq == 0 and tq % 128 == 0 and tq + 1 >= T5_MAX_DIST
    nq = T // tq
    topk = min(TOPK_MAX, T // 4)
    kit = jnp.swapaxes(ki, 1, 2).astype(jnp.bfloat16)
    r = jnp.arange(tq, dtype=jnp.int32)
    dist = jnp.arange(3, dtype=jnp.int32)[:, None, None] * tq + r[None, :, None] - r[None, None, :]
    bias3 = jnp.transpose(rel_bias[t5_bucket(dist)], (0, 3, 1, 2))
    kern = functools.partial(_dsa_prompt_kernel, topk=topk, n_idx_bits=max(1, (T - 1).bit_length()))
    return pl.pallas_call(
        kern,
        grid=(B, nq, nq),
        in_specs=[pl.BlockSpec((1, tq, qi.shape[2]), lambda b, i, j: (b, i, 0)),
                  pl.BlockSpec((1, IDX_DIM, T), lambda b, i, j: (b, 0, 0)),
                  pl.BlockSpec((1, tq, IDX_HEADS), lambda b, i, j: (b, i, 0)),
                  pl.BlockSpec((1, tq, b_width), lambda b, i, j: (b, i, 0)),
                  pl.BlockSpec((1, tq, b_width), lambda b, i, j: (b, jnp.minimum(j, i), 0)),
                  pl.BlockSpec((1, tq, b_width), lambda b, i, j: (b, jnp.minimum(j, i), 0)),
                  pl.BlockSpec((1, n_heads, tq, tq), lambda b, i, j: (jnp.clip(i - j, 0, 2), 0, 0, 0))],
        out_specs=pl.BlockSpec((1, tq, b_width), lambda b, i, j: (b, i, 0)),
        out_shape=jax.ShapeDtypeStruct((B, T, b_width), jnp.float32),
        scratch_shapes=[pltpu.VMEM((nq, tq, tq), jnp.int32),
                        pltpu.VMEM((tq, 1), jnp.int32),
                        pltpu.VMEM((tq, qi.shape[2]), jnp.bfloat16),
                        pltpu.VMEM((tq, b_width), jnp.bfloat16),
                        pltpu.VMEM((n_heads, tq, 1), jnp.float32),
                        pltpu.VMEM((n_heads, tq, 1), jnp.float32),
                        pltpu.VMEM((tq, b_width), jnp.float32)],
        compiler_params=pltpu.CompilerParams(
            dimension_semantics=("parallel", "arbitrary", "arbitrary"),
            vmem_limit_bytes=VMEM_LIMIT_BYTES),
        name="dsa_prompt",
    )(qi, kit, wi, q, k, v, bias3)


SCAN_CHUNK = A_HEAD
LANES = 128
SUBLANES = 8


def _rwkv_scan_kernel(v_ref, kk_ref, kka_ref, w_ref, km_ref, r_ref, s0_ref, o_ref, st_ref, s_scr, *, n_steps):
    c = pl.program_id(0)
    n_batch = v_ref.shape[0]
    n_tiles = v_ref.shape[2] // LANES

    @pl.when(c == 0)
    def _():
        s_scr[...] = s0_ref[...]

    if n_steps < SCAN_CHUNK:
        o_ref[...] = jnp.zeros_like(o_ref)
    seg_base = lax.broadcasted_iota(jnp.int32, (A_HEAD, LANES), 1) & (LANES - A_HEAD)

    group = min(n_steps, SUBLANES)

    def step_group(g, carry):
        t0 = g * group if isinstance(g, int) else pl.multiple_of(g * group, group)
        for b in range(n_batch):
            for p in range(n_tiles):
                ls = slice(p * LANES, (p + 1) * LANES)
                s = s_scr[b, :, ls]
                v_rows = v_ref[b, pl.ds(t0, group), ls]
                o_rows = []
                for u in range(group):
                    idx = seg_base + (t0 + u)
                    col = lambda ref: jnp.take_along_axis(ref[b, 0, :, ls], idx, axis=1)
                    s_kk = jnp.sum(s * col(kk_ref), axis=0, keepdims=True)
                    s = s * col(w_ref) - s_kk * col(kka_ref) + v_rows[u:u + 1] * col(km_ref)
                    o_rows.append(jnp.sum(s * col(r_ref), axis=0, keepdims=True))
                o_ref[b, pl.ds(t0, group), ls] = jnp.concatenate(o_rows, axis=0)
                s_scr[b, :, ls] = s
        return carry

    if n_steps == group:
        step_group(0, 0)
    else:
        lax.fori_loop(0, n_steps // group, step_group, 0)

    @pl.when(c == pl.num_programs(0) - 1)
    def _():
        st_ref[...] = s_scr[...]


def _time_transposed(x, n_chunks):
    B, _, width = x.shape
    x = x.reshape(B, n_chunks, SCAN_CHUNK, width // A_HEAD, A_HEAD)
    return jnp.transpose(x, (0, 1, 4, 3, 2)).reshape(B, n_chunks, A_HEAD, width)


def rwkv_scan(r, w, k, v, kk, a, state):
    B, T, width = r.shape
    n_heads = width // A_HEAD
    assert SCAN_CHUNK == A_HEAD and (T % SCAN_CHUNK == 0 or T < SCAN_CHUNK)
    n_chunks = pl.cdiv(T, SCAN_CHUNK)
    t_pad = n_chunks * SCAN_CHUNK
    pad = lambda x: jnp.pad(x, ((0, 0), (0, t_pad - T), (0, 0)))
    cols = [_time_transposed(pad(x), n_chunks) for x in (kk, kk * a, w, k, r)]
    st0 = jnp.transpose(state, (0, 3, 1, 2)).reshape(B, A_HEAD, width)
    row_spec = pl.BlockSpec((B, SCAN_CHUNK, width), lambda c: (0, c, 0))
    col_spec = pl.BlockSpec((B, 1, A_HEAD, width), lambda c: (0, c, 0, 0))
    st_spec = pl.BlockSpec((B, A_HEAD, width), lambda c: (0, 0, 0))
    o, st = pl.pallas_call(
        functools.partial(_rwkv_scan_kernel, n_steps=min(T, SCAN_CHUNK)),
        grid=(n_chunks,),
        in_specs=[row_spec] + [col_spec] * 5 + [st_spec],
        out_specs=[row_spec, st_spec],
        out_shape=[jax.ShapeDtypeStruct((B, t_pad, width), jnp.float32),
                   jax.ShapeDtypeStruct((B, A_HEAD, width), jnp.float32)],
        scratch_shapes=[pltpu.VMEM((B, A_HEAD, width), jnp.float32)],
        compiler_params=pltpu.CompilerParams(
            dimension_semantics=("arbitrary",),
            vmem_limit_bytes=VMEM_LIMIT_BYTES),
        name="rwkv_scan",
    )(pad(v), *cols, st0)
    new_state = jnp.transpose(st.reshape(B, A_HEAD, n_heads, A_HEAD), (0, 2, 3, 1))
    return o[:, :T], new_state


def _split2(x):
    hi = x.astype(jnp.bfloat16)
    return hi, (x - hi.astype(jnp.float32)).astype(jnp.bfloat16)


def _dot_nt_split(a, b):
    a_hi, a_lo = _split2(a)
    b_hi, b_lo = _split2(b)
    return _dot_nt(jnp.concatenate([a_hi, a_hi, a_lo], axis=1), jnp.concatenate([b_hi, b_lo, b_hi], axis=1))


def _gla_segment_matrix(chunk):
    t = np.arange(chunk)[:, None]
    u = np.arange(chunk)[None, :]
    rows = []
    for lvl in range(chunk.bit_length() - 1):
        m = (chunk // 2) >> lvl
        blk = t // m
        right = np.logical_and(u >= blk * m, u <= t)
        left = np.logical_and(u > t, u < (blk + 1) * m)
        rows.append(np.where(blk % 2 == 1, right, left))
    rows.append(u <= t)
    rows.append(u > t)
    seg = np.concatenate(rows, axis=0).astype(np.float32)
    return np.concatenate([seg, seg, seg], axis=1)


def _dot_nt(a, b, **kw):
    return lax.dot_general(a, b, (((1,), (1,)), ((), ())), preferred_element_type=jnp.float32, **kw)


def _gla_kernel(seg_ref, q_ref, k_ref, v_ref, lf_ref, g_ref, gn_ref, s0_ref, y_ref, st_ref, s_scr):
    c = pl.program_id(1)
    chunk = q_ref.shape[1]
    n_levels = chunk.bit_length() - 1
    n_heads = s_scr.shape[0]
    bf = jnp.bfloat16

    @pl.when(c == 0)
    def _():
        s_scr[...] = s0_ref[0]

    row = lax.broadcasted_iota(jnp.int32, (chunk, chunk), 0)
    col = lax.broadcasted_iota(jnp.int32, (chunk, chunk), 1)
    row_w = lax.broadcasted_iota(jnp.int32, (chunk, C_EXPAND), 0)
    seg = seg_ref[...]
    gn = gn_ref[...]

    def head(h, carry):
        ls = pl.ds(pl.multiple_of(h * C_EXPAND, C_EXPAND), C_EXPAND)
        q, k, v, lf = q_ref[0, :, ls], k_ref[0, :, ls], v_ref[0, :, ls], lf_ref[0, :, ls]
        lf_hi = lf.astype(bf)
        lf_r = lf - lf_hi.astype(jnp.float32)
        lf_mid = lf_r.astype(bf)
        lf_lo = (lf_r - lf_mid.astype(jnp.float32)).astype(bf)
        e = jnp.exp(jnp.dot(seg, jnp.concatenate([lf_hi, lf_mid, lf_lo], axis=0), preferred_element_type=jnp.float32))
        a = jnp.where(row == col, _dot_nt_split(q, k), 0.0)
        for lvl in range(n_levels):
            sh = n_levels - 1 - lvl
            x = jnp.where(((row_w >> sh) & 1) == 1, q, k) * e[lvl * chunk:(lvl + 1) * chunk]
            pair = (((row >> sh) & 1) == 1) & ((col >> sh) == (row >> sh) - 1)
            a = a + jnp.where(pair, _dot_nt_split(x, x), 0.0)
        e_b = e[n_levels * chunk:(n_levels + 1) * chunk]
        e_r = e[(n_levels + 1) * chunk:]
        st = s_scr[h]
        o = jnp.dot(a.astype(bf), v.astype(bf), preferred_element_type=jnp.float32)
        o = o + _dot_nt((q * e_b).astype(bf), st.astype(bf))
        s_scr[h] = st * e_b[chunk - 1:chunk, :] + lax.dot_general(
            v.astype(bf), (k * e_r).astype(bf), (((0,), (0,)), ((), ())), preferred_element_type=jnp.float32)
        o = o * lax.rsqrt(jnp.mean(o * o, axis=1, keepdims=True) + RMS_EPS) * gn
        gate = g_ref[0, :, ls]
        y_ref[0, :, ls] = o * (gate * jax.nn.sigmoid(gate))
        return carry

    lax.fori_loop(0, n_heads, head, 0, unroll=2)

    @pl.when(c == pl.num_programs(1) - 1)
    def _():
        st_ref[0] = s_scr[...]


def gla_gated(q, k, v, log_f, g, gn_g, state):
    B, T, width = q.shape
    n_heads = width // C_EXPAND
    assert v.shape[2] == width, "kernel assumes dk == dv == C_EXPAND"
    n_chunks = pl.cdiv(T, GLA_CHUNK)
    t_pad = n_chunks * GLA_CHUNK
    pad = lambda x: jnp.pad(x, ((0, 0), (0, t_pad - T), (0, 0)))
    seg = jnp.asarray(_gla_segment_matrix(GLA_CHUNK), dtype=jnp.bfloat16)
    st0 = jnp.swapaxes(state, 2, 3)
    tok_spec = pl.BlockSpec((1, GLA_CHUNK, width), lambda b, c: (b, c, 0))
    st_spec = pl.BlockSpec((1, n_heads, C_EXPAND, C_EXPAND), lambda b, c: (b, 0, 0, 0))
    y, st = pl.pallas_call(
        _gla_kernel,
        grid=(B, n_chunks),
        in_specs=[pl.BlockSpec(seg.shape, lambda b, c: (0, 0))] + [tok_spec] * 5
                 + [pl.BlockSpec((1, C_EXPAND), lambda b, c: (0, 0)), st_spec],
        out_specs=[tok_spec, st_spec],
        out_shape=[jax.ShapeDtypeStruct((B, t_pad, width), jnp.float32),
                   jax.ShapeDtypeStruct(st0.shape, jnp.float32)],
        scratch_shapes=[pltpu.VMEM((n_heads, C_EXPAND, C_EXPAND), jnp.float32)],
        compiler_params=pltpu.CompilerParams(
            dimension_semantics=("parallel", "arbitrary"),
            vmem_limit_bytes=VMEM_LIMIT_BYTES),
        name="gla_gated",
    )(seg, pad(q), pad(k), pad(v), pad(log_f), pad(g), gn_g.reshape(1, C_EXPAND), st0)
    return y[:, :T], jnp.swapaxes(st, 2, 3)


def _moe_up_kernel(te_ref, x_ref, wg_ref, wu_ref, o_ref):
    del te_ref
    xb = x_ref[...].astype(jnp.bfloat16)
    g = jnp.dot(xb, wg_ref[0].astype(jnp.bfloat16), preferred_element_type=jnp.float32)
    u = jnp.dot(xb, wu_ref[0].astype(jnp.bfloat16), preferred_element_type=jnp.float32)
    o_ref[...] = g * jax.nn.sigmoid(g) * u


def _moe_down_kernel(te_ref, h_ref, wd_ref, o_ref):
    del te_ref
    o_ref[...] = jnp.dot(h_ref[...].astype(jnp.bfloat16), wd_ref[0].astype(jnp.bfloat16),
                         preferred_element_type=jnp.float32)


def moe_experts(tile_expert, xs, wg, wu, wd, tm, tf=704, tn=512):
    rows, d_model = xs.shape
    d_ff = wg.shape[2]
    tf = _pick_tile(d_ff, tf, 128)
    tn = _pick_tile(d_model, tn, 128)
    params = pltpu.CompilerParams(dimension_semantics=("parallel", "arbitrary"), vmem_limit_bytes=VMEM_LIMIT_BYTES)
    h = pl.pallas_call(
        _moe_up_kernel,
        grid_spec=pltpu.PrefetchScalarGridSpec(
            num_scalar_prefetch=1, grid=(d_ff // tf, rows // tm),
            in_specs=[pl.BlockSpec((tm, d_model), lambda j, i, te: (i, 0)),
                      pl.BlockSpec((1, d_model, tf), lambda j, i, te: (te[i], 0, j)),
                      pl.BlockSpec((1, d_model, tf), lambda j, i, te: (te[i], 0, j))],
            out_specs=pl.BlockSpec((tm, tf), lambda j, i, te: (i, j))),
        out_shape=jax.ShapeDtypeStruct((rows, d_ff), jnp.float32),
        compiler_params=params, name="moe_up",
    )(tile_expert, xs, wg, wu)
    return pl.pallas_call(
        _moe_down_kernel,
        grid_spec=pltpu.PrefetchScalarGridSpec(
            num_scalar_prefetch=1, grid=(d_model // tn, rows // tm),
            in_specs=[pl.BlockSpec((tm, d_ff), lambda j, i, te: (i, 0)),
                      pl.BlockSpec((1, d_ff, tn), lambda j, i, te: (te[i], 0, j))],
            out_specs=pl.BlockSpec((tm, tn), lambda j, i, te: (i, j))),
        out_shape=jax.ShapeDtypeStruct((rows, d_model), jnp.float32),
        compiler_params=params, name="moe_down",
    )(tile_expert, h, wd)


def moe_swiglu(x, w_router, wg, wu, wd):
    lead, d_model = x.shape[:-1], x.shape[-1]
    x = x.reshape(-1, d_model)
    n_tok = x.shape[0]
    n_pairs = n_tok * TOP_EXPERTS
    tm = min(256, n_pairs)
    assert n_pairs % tm == 0
    probs = jax.nn.softmax(matmul(x, w_router), axis=-1)
    top_p, top_i = lax.top_k(probs, TOP_EXPERTS)
    top_p = top_p / jnp.sum(top_p, axis=-1, keepdims=True)
    pair_expert = top_i.reshape(-1)
    order = jnp.argsort(pair_expert, stable=True)
    counts = jnp.sum(jax.nn.one_hot(pair_expert, N_EXPERTS, dtype=jnp.int32), axis=0)
    padded = (counts + tm - 1) // tm * tm
    padded_end = jnp.cumsum(padded)
    sorted_expert = pair_expert[order]
    rank = jnp.arange(n_pairs, dtype=jnp.int32) - (jnp.cumsum(counts) - counts)[sorted_expert]
    dest = (padded_end - padded)[sorted_expert] + rank
    rows = n_pairs + N_EXPERTS * tm
    row_token = jnp.zeros((rows,), jnp.int32).at[dest].set((order // TOP_EXPERTS).astype(jnp.int32))
    pair_row = jnp.zeros((n_pairs,), jnp.int32).at[order].set(dest.astype(jnp.int32))
    tile_start = jnp.arange(rows // tm, dtype=jnp.int32) * tm
    tile_expert = jnp.minimum(jnp.searchsorted(padded_end, tile_start, side='right'), N_EXPERTS - 1).astype(jnp.int32)
    y_rows = moe_experts(tile_expert, x[row_token], wg, wu, wd, tm)
    y_pairs = y_rows[pair_row].reshape(n_tok, TOP_EXPERTS, d_model)
    y = jnp.sum(top_p[..., None] * y_pairs, axis=1)
    return y.reshape(lead + (d_model,))


def split_cols(a, sizes):
    return jnp.split(a, np.cumsum(sizes)[:-1].tolist(), axis=-1)


def gather_rows(a, idx):
    return jax.vmap(lambda a_b, i_b: a_b[i_b])(a, idx)


def layer_norm(x, g, b):
    mu = jnp.mean(x, axis=-1, keepdims=True)
    var = jnp.mean(jnp.square(x - mu), axis=-1, keepdims=True)
    return (x - mu) * lax.rsqrt(var + LN_EPS) * g + b


def t5_bucket(dist):
    dist = jnp.maximum(dist, 0)
    max_exact = T5_BUCKETS // 2
    d = jnp.maximum(dist, 1).astype(jnp.float32)
    large = max_exact + (jnp.log(d / max_exact) / math.log(T5_MAX_DIST / max_exact) * (T5_BUCKETS - max_exact)).astype(jnp.int32)
    large = jnp.minimum(large, T5_BUCKETS - 1)
    return jnp.where(dist < max_exact, dist, large)


def dsa_attend(q, qi, wi, q_pos, k_idx, gather_kv, rel_bias, topk):
    B, T = q.shape[:2]
    L = k_idx.shape[1]
    blk = min(Q_BLOCK, T)
    nb = -(-T // blk)
    pad = nb * blk - T

    def to_blocks(a):
        a = jnp.pad(a, [(0, 0), (0, pad)] + [(0, 0)] * (a.ndim - 2))
        return jnp.moveaxis(a.reshape((B, nb, blk) + a.shape[2:]), 1, 0)

    pos_b = jnp.pad(q_pos, (0, pad)).reshape(nb, blk)
    k_pos = jnp.arange(L, dtype=jnp.int32)

    def one_block(args):
        qb, qib, wib, pb = args
        s = jnp.einsum('bthd,bsd->bths', qib, k_idx) * IDX_DIM ** -0.5
        score = jnp.einsum('bth,bths->bts', wib, jax.nn.relu(s))
        score = jnp.where(k_pos[None, None, :] <= pb[None, :, None], score, -jnp.inf)
        _, sel = lax.top_k(score, topk)
        valid = sel <= pb[None, :, None]
        k_sel, v_sel = gather_kv(sel)
        logits = jnp.einsum('bthd,btkhd->bhtk', qb, k_sel) * B_HEAD ** -0.5
        bias = rel_bias[t5_bucket(pb[None, :, None] - sel)]
        logits = logits + jnp.transpose(bias, (0, 3, 1, 2))
        logits = jnp.where(valid[:, None], logits, -jnp.inf)
        p = jax.nn.softmax(logits, axis=-1)
        return jnp.einsum('bhtk,btkhd->bthd', p, v_sel)

    out = lax.map(one_block, (to_blocks(q), to_blocks(qi), to_blocks(wi), pos_b))
    return jnp.moveaxis(out, 0, 1).reshape((B, nb * blk) + q.shape[2:])[:, :T]


def attend_self(q, k, v, qi, ki, wi, rel_bias):
    T = q.shape[1]
    gather = lambda sel: (gather_rows(k, sel), gather_rows(v, sel))
    return dsa_attend(q, qi, wi, jnp.arange(T, dtype=jnp.int32), ki, gather, rel_bias, min(TOPK_MAX, T // 4))


def attend_paged(q, k, v, qi, ki, wi, pool_k, pool_v, pool_kidx, page_table, rel_bias):
    B, T = q.shape[:2]
    past = page_table.shape[1] * PAGE_SIZE
    kidx_past = pool_kidx[page_table].reshape(B, past, IDX_DIM)
    kidx_all = jnp.concatenate([kidx_past, ki], axis=1)
    flat_k = pool_k.reshape((-1,) + pool_k.shape[2:])
    flat_v = pool_v.reshape((-1,) + pool_v.shape[2:])

    def gather(sel):
        ps = jnp.minimum(sel, past - 1)
        row = gather_rows(page_table, ps // PAGE_SIZE) * PAGE_SIZE + ps % PAGE_SIZE
        is_new = (sel >= past)[..., None, None]
        ns = jnp.clip(sel - past, 0, T - 1)
        k_sel = jnp.where(is_new, gather_rows(k, ns), flat_k[row])
        v_sel = jnp.where(is_new, gather_rows(v, ns), flat_v[row])
        return k_sel, v_sel

    q_pos = past + jnp.arange(T, dtype=jnp.int32)
    return dsa_attend(q, qi, wi, q_pos, kidx_all, gather, rel_bias, min(TOPK_MAX, (past + T) // 4))


def rwkv7_mix(p_a, shift_state, wkv_state, mu, w0, w2, a0, a2, g2, kk_s, ka_s, rk_s, lnx_g, lnx_b):
    B, T, _ = p_a.shape
    a_width = w0.shape[0]
    a_heads = a_width // A_HEAD
    prev = jnp.concatenate([shift_state[:, None, :], p_a[:, :-1]], axis=1)
    xm = p_a + (prev - p_a) * mu
    r, k, v, wd, ad, gd = split_cols(xm, [a_width, a_width, a_width, W_LORA, AA_LORA, G_LORA])
    w_pre = w0 + mm(jnp.tanh(wd), w2)
    decay = jnp.exp(-jnp.exp(-jax.nn.softplus(-w_pre) - 0.5))
    a = jax.nn.sigmoid(a0 + mm(ad, a2))
    g = mm(jax.nn.sigmoid(gd), g2)
    heads = lambda t: t.reshape(B, T, a_heads, A_HEAD)
    kk = heads(k * kk_s)
    kk = kk / jnp.maximum(jnp.sqrt(jnp.sum(jnp.square(kk), axis=-1, keepdims=True)), 1e-12)
    k_mod = heads(k * (1.0 + (a - 1.0) * ka_s))
    r_h, v_h = heads(r), heads(v)
    o, S = rwkv_scan(r, decay, k_mod.reshape(B, T, a_width), v, kk.reshape(B, T, a_width), a, wkv_state)
    o = heads(o)
    mu_o = jnp.mean(o, axis=-1, keepdims=True)
    var_o = jnp.mean(jnp.square(o - mu_o), axis=-1, keepdims=True)
    o = ((o - mu_o) * lax.rsqrt(var_o + RWKV_GN_EPS)).reshape(B, T, a_width) * lnx_g + lnx_b
    bonus = jnp.sum(r_h * k_mod * rk_s.reshape(a_heads, A_HEAD), axis=-1, keepdims=True) * v_h
    y = (o + bonus.reshape(B, T, a_width)) * g
    return y, p_a[:, -1], S


def ab_mixer(x, shift_state, wkv_state, attend, w_in, rw, w_out):
    B, T, d_model = x.shape
    a_width = rw[1].shape[0]
    a_proj = 3 * a_width + W_LORA + AA_LORA + G_LORA
    b_width = d_model - a_width
    b_heads = b_width // B_HEAD
    proj = mm(x, w_in)
    y_a, new_shift, new_wkv = rwkv7_mix(proj[..., :a_proj], shift_state, wkv_state, *rw)
    q, k, v, qi, ki, wi = split_cols(proj[..., a_proj:], [b_width, b_width, b_width, IDX_HEADS * IDX_DIM, IDX_DIM, IDX_HEADS])
    y_b = attend(q, k, v, qi, ki, wi)
    y = mm(jnp.concatenate([y_a, y_b], axis=-1), w_out)
    heads = lambda t: t.reshape(B, T, b_heads, B_HEAD)
    return y, new_shift, new_wkv, heads(k), heads(v), ki


def attend_sample(q, k, v, qi, ki, wi, **kw):
    B, T, b_width = q.shape
    heads = lambda t: t.reshape(B, T, b_width // B_HEAD, B_HEAD)
    y = attend_paged(heads(q), heads(k), heads(v), qi.reshape(B, T, IDX_HEADS, IDX_DIM), ki, wi * IDX_HEADS ** -0.5, **kw)
    return y.reshape(B, T, b_width)


def hgrn2_mixer(x, state, w_in, lb, gn_g, w_out):
    B, T, c_width = x.shape
    q, f_raw, i, g = split_cols(mm(x, w_in), [c_width, c_width, c_width, c_width])
    log_f = jnp.logaddexp(jnp.log(lb), jnp.log1p(-lb) + jax.nn.log_sigmoid(f_raw))
    k_in = (1.0 - lb) * jax.nn.sigmoid(-f_raw)
    y, new_state = gla_gated(jax.nn.silu(q), k_in, i, log_f, g, gn_g, state)
    return mm(y, w_out), new_state


def kernel(x_prompt, x_sample, cache_k, cache_v, cache_kidx, state_wkv, state_shift, state_hgrn, page_table, w_in_ab, rwkv_mu, rwkv_w0, rwkv_w2, rwkv_a0, rwkv_a2, rwkv_g2, rwkv_kk, rwkv_ka, rwkv_rk, rwkv_lnx_g, rwkv_lnx_b, w_out_ab, rel_bias, ffn_w_gate, ffn_w_up, ffn_w_down, w_in_c, hgrn_lb, hgrn_gnorm, w_out_c, router, moe_w_gate, moe_w_up, moe_w_down, ln1_g, ln1_b, ln2_g, ln2_b):
    depth = ln1_g.shape[0]
    alpha = (2 * depth) ** 0.25
    a_width = rwkv_w0.shape[1]
    a_heads = a_width // A_HEAD
    a_proj = state_shift.shape[-1]
    c_heads, c_dk, c_dv = state_hgrn.shape[2:]
    lb_all = jnp.cumsum(jax.nn.softmax(hgrn_lb, axis=0), axis=0)
    lb_all = lb_all - lb_all[0:1]
    attend_p = functools.partial(dsa_prompt, rel_bias=rel_bias)
    xp, xs = x_prompt, x_sample
    kp_l, vp_l, kip_l, wkvp_l, shp_l, hgp_l = [], [], [], [], [], []
    ks_l, vs_l, kis_l, wkvs_l, shs_l, hgs_l = [], [], [], [], [], []
    for l in range(depth):
        j = l // 2
        if l % 2 == 0:
            rw = (rwkv_mu[j], rwkv_w0[j], rwkv_w2[j], rwkv_a0[j], rwkv_a2[j], rwkv_g2[j],
                  rwkv_kk[j], rwkv_ka[j], rwkv_rk[j], rwkv_lnx_g[j], rwkv_lnx_b[j])
            attend_s = functools.partial(attend_sample, pool_k=cache_k[j], pool_v=cache_v[j], pool_kidx=cache_kidx[j],
                                         page_table=page_table, rel_bias=rel_bias)
            shift0 = jnp.zeros((xp.shape[0], a_proj), xp.dtype)
            wkv0 = jnp.zeros((xp.shape[0], a_heads, A_HEAD, A_HEAD), jnp.float32)
            yp, shp, wkvp, kp, vp, kip = ab_mixer(xp, shift0, wkv0, attend_p, w_in_ab[j], rw, w_out_ab[j])
            ys, shs, wkvs, kss, vss, kis = ab_mixer(xs, state_shift[j], state_wkv[j], attend_s, w_in_ab[j], rw, w_out_ab[j])
            kp_l.append(kp); vp_l.append(vp); kip_l.append(kip); wkvp_l.append(wkvp); shp_l.append(shp)
            ks_l.append(kss); vs_l.append(vss); kis_l.append(kis); wkvs_l.append(wkvs); shs_l.append(shs)
        else:
            hg0 = jnp.zeros((xp.shape[0], c_heads, c_dk, c_dv), jnp.float32)
            yp, hgp = hgrn2_mixer(xp, hg0, w_in_c[j], lb_all[j], hgrn_gnorm[j], w_out_c[j])
            ys, hgs = hgrn2_mixer(xs, state_hgrn[j], w_in_c[j], lb_all[j], hgrn_gnorm[j], w_out_c[j])
            hgp_l.append(hgp); hgs_l.append(hgs)
        xp = layer_norm(alpha * xp + yp, ln1_g[l], ln1_b[l])
        xs = layer_norm(alpha * xs + ys, ln1_g[l], ln1_b[l])
        if l % 2 == 0:
            fp = swiglu(xp, ffn_w_gate[j], ffn_w_up[j], ffn_w_down[j])
            fs = swiglu(xs, ffn_w_gate[j], ffn_w_up[j], ffn_w_down[j])
        else:
            fp = moe_swiglu(xp, router[j], moe_w_gate[j], moe_w_up[j], moe_w_down[j])
            fs = moe_swiglu(xs, router[j], moe_w_gate[j], moe_w_up[j], moe_w_down[j])
        xp = layer_norm(alpha * xp + fp, ln2_g[l], ln2_b[l])
        xs = layer_norm(alpha * xs + fs, ln2_g[l], ln2_b[l])
    return (xp, xs, jnp.stack(kp_l), jnp.stack(vp_l), jnp.stack(kip_l), jnp.stack(wkvp_l), jnp.stack(shp_l),
            jnp.stack(hgp_l), jnp.stack(ks_l), jnp.stack(vs_l), jnp.stack(kis_l), jnp.stack(wkvs_l),
            jnp.stack(shs_l), jnp.stack(hgs_l))
```

```python
import functools
import math

import jax
import jax.numpy as jnp
import numpy as np
from jax import lax
from jax.experimental import pallas as pl
from jax.experimental.pallas import tpu as pltpu

A_HEAD = 64
W_LORA = 64
AA_LORA = 64
G_LORA = 128
RWKV_GN_EPS = 64e-5
B_HEAD = 128
IDX_HEADS = 16
IDX_DIM = 64
TOPK_MAX = 256
Q_BLOCK = 64
T5_BUCKETS = 32
T5_MAX_DIST = 128
C_EXPAND = 128
GLA_CHUNK = 64
RMS_EPS = 1e-5
N_EXPERTS = 8
TOP_EXPERTS = 2
PAGE_SIZE = 128
LN_EPS = 1e-5

VMEM_LIMIT_BYTES = 56 * 1024 * 1024
BF16_ROWS_MIN = 256


def _pick_tile(n, target, align):
    if n <= target:
        return n
    t = (target // align) * align
    while t >= align:
        if n % t == 0:
            return t
        t -= align
    return n


def _matmul_kernel(x_ref, w_ref, o_ref):
    @pl.when(pl.program_id(2) == 0)
    def _():
        o_ref[...] = jnp.zeros_like(o_ref)

    o_ref[...] += jnp.dot(x_ref[...].astype(jnp.bfloat16), w_ref[...].astype(jnp.bfloat16),
                          preferred_element_type=jnp.float32)


def matmul(x, w, tm=1024, tn=1024, tk=2048):
    m, k = x.shape
    k2, n = w.shape
    assert k == k2
    tm = _pick_tile(m, tm, 8)
    tk = _pick_tile(k, tk, 128)
    tn = min(tn, n) if n % 128 == 0 or n < 128 else tn
    grid = (pl.cdiv(n, tn), pl.cdiv(m, tm), k // tk)
    return pl.pallas_call(
        _matmul_kernel,
        grid=grid,
        in_specs=[pl.BlockSpec((tm, tk), lambda j, i, kk: (i, kk)),
                  pl.BlockSpec((tk, tn), lambda j, i, kk: (kk, j))],
        out_specs=pl.BlockSpec((tm, tn), lambda j, i, kk: (i, j)),
        out_shape=jax.ShapeDtypeStruct((m, n), jnp.float32),
        compiler_params=pltpu.CompilerParams(
            dimension_semantics=("parallel", "parallel", "arbitrary"),
            vmem_limit_bytes=VMEM_LIMIT_BYTES),
        name="matmul",
    )(x, w)


def _swiglu_up_kernel(x_ref, wg_ref, wu_ref, o_ref):
    xb = x_ref[...].astype(jnp.bfloat16)
    g = jnp.dot(xb, wg_ref[...].astype(jnp.bfloat16), preferred_element_type=jnp.float32)
    u = jnp.dot(xb, wu_ref[...].astype(jnp.bfloat16), preferred_element_type=jnp.float32)
    o_ref[...] = (g * jax.nn.sigmoid(g) * u).astype(o_ref.dtype)


def swiglu_up(x, wg, wu, tm=1024, tn=512):
    m, k = x.shape
    n = wg.shape[1]
    tm = _pick_tile(m, tm, 8)
    tn = _pick_tile(n, tn, 128)
    return pl.pallas_call(
        _swiglu_up_kernel,
        grid=(n // tn, m // tm),
        in_specs=[pl.BlockSpec((tm, k), lambda j, i: (i, 0)),
                  pl.BlockSpec((k, tn), lambda j, i: (0, j)),
                  pl.BlockSpec((k, tn), lambda j, i: (0, j))],
        out_specs=pl.BlockSpec((tm, tn), lambda j, i: (i, j)),
        out_shape=jax.ShapeDtypeStruct((m, n), x.dtype),
        compiler_params=pltpu.CompilerParams(
            dimension_semantics=("parallel", "parallel"),
            vmem_limit_bytes=VMEM_LIMIT_BYTES),
        name="swiglu_up",
    )(x, wg, wu)


def _mxu_operand(x):
    x = x.reshape(-1, x.shape[-1])
    return x.astype(jnp.bfloat16) if x.shape[0] >= BF16_ROWS_MIN else x


def mm(x, w):
    return matmul(_mxu_operand(x), w).reshape(x.shape[:-1] + (w.shape[1],))


def swiglu(x, wg, wu, wd):
    h = swiglu_up(_mxu_operand(x), wg, wu)
    return matmul(h, wd).reshape(x.shape[:-1] + (wd.shape[1],))


INT_MIN = -2 ** 31
NEG_BIG = -1e30


def _sortable_key(x):
    bits = pltpu.bitcast(jnp.where(x == 0.0, 0.0, x), jnp.int32)
    return bits ^ ((bits >> 31) & 0x7FFFFFFF)


def _dsa_prompt_kernel(qi_ref, kit_ref, wi_ref, q_ref, k_ref, v_ref, bias_ref, o_ref,
                       keys_ref, tau_ref, qib_ref, qb_ref, m_ref, l_ref, acc_ref, *, topk, n_idx_bits):
    i = pl.program_id(1)
    j = pl.program_id(2)
    tq = q_ref.shape[1]
    n_heads = q_ref.shape[2] // B_HEAD
    n_lane_tiles = tq // 128
    kf = float(topk)

    @pl.when(j == 0)
    def _select():
        qib_ref[...] = (qi_ref[0] * IDX_DIM ** -0.5).astype(jnp.bfloat16)
        qb_ref[...] = q_ref[0].astype(jnp.bfloat16)
        m_ref[...] = jnp.full_like(m_ref, NEG_BIG)
        l_ref[...] = jnp.zeros_like(l_ref)
        acc_ref[...] = jnp.zeros_like(acc_ref)
        wb = (wi_ref[0] * IDX_HEADS ** -0.5).astype(jnp.bfloat16).astype(jnp.float32)
        row = i * tq + lax.broadcasted_iota(jnp.int32, (tq, tq), 0)
        col = lax.broadcasted_iota(jnp.int32, (tq, tq), 1)

        def score_chunk(c, carry):
            kc = kit_ref[0, :, pl.ds(pl.multiple_of(c * tq, tq), tq)]
            score = jnp.zeros((tq, tq), jnp.float32)
            for h in range(IDX_HEADS):
                s = jnp.dot(qib_ref[:, h * IDX_DIM:(h + 1) * IDX_DIM], kc, preferred_element_type=jnp.float32)
                r = jnp.maximum(s, 0.0).astype(jnp.bfloat16).astype(jnp.float32)
                score = score + wb[:, h:h + 1] * r
            keys_ref[c] = jnp.where(c * tq + col <= row, _sortable_key(score), INT_MIN)
            return carry

        lax.fori_loop(0, i + 1, score_chunk, 0)

        def count(pred):
            def body(c, acc):
                hit = jnp.where(pred(keys_ref[c], c), 1.0, 0.0)
                part = hit[:, :128]
                for u in range(1, n_lane_tiles):
                    part = part + hit[:, u * 128:(u + 1) * 128]
                return acc + part
            acc = lax.fori_loop(0, i + 1, body, jnp.zeros((tq, 128), jnp.float32))
            return jnp.sum(acc, axis=1, keepdims=True)

        tau = jnp.where(count(lambda kc, c: kc >= 0) >= kf, 0, INT_MIN).astype(jnp.int32)

        def bit_step(n, tau):
            cand = tau | jnp.left_shift(jnp.int32(1), 30 - n)
            return jnp.where(count(lambda kc, c: kc >= cand) >= kf, cand, tau)

        tau = lax.fori_loop(0, 31, bit_step, tau)
        tau = jnp.maximum(tau, INT_MIN + 1)
        tau_ref[...] = tau
        n_ge = count(lambda kc, c: kc >= tau)

        @pl.when(jnp.max(n_ge) > kf)
        def _ties():
            need = kf - count(lambda kc, c: kc > tau)

            def idx_step(n, ans):
                cand = ans + jnp.left_shift(jnp.int32(1), n_idx_bits - 1 - n)
                f = count(lambda kc, c: (kc == tau) & (c * tq + col < cand))
                return jnp.where(f < need, cand, ans)

            last = lax.fori_loop(0, n_idx_bits, idx_step, jnp.zeros((tq, 1), jnp.int32))

            def demote(c, carry):
                kc = keys_ref[c]
                keys_ref[c] = jnp.where((kc == tau) & (c * tq + col > last), INT_MIN, kc)
                return carry

            lax.fori_loop(0, i + 1, demote, 0)

    @pl.when(j <= i)
    def _attend():
        sel = keys_ref[j] >= tau_ref[...]
        for h in range(n_heads):
            hs = slice(h * B_HEAD, (h + 1) * B_HEAD)
            kh = k_ref[0, :, hs].astype(jnp.bfloat16)
            vh = v_ref[0, :, hs].astype(jnp.bfloat16)
            logits = lax.dot_general(qb_ref[:, hs], kh, (((1,), (1,)), ((), ())),
                                     preferred_element_type=jnp.float32)
            logits = logits * B_HEAD ** -0.5 + bias_ref[0, h]
            logits = jnp.where(sel, logits, NEG_BIG)
            m_old = m_ref[h]
            m_new = jnp.maximum(m_old, jnp.max(logits, axis=1, keepdims=True))
            alpha = jnp.exp(m_old - m_new)
            p = jnp.exp(logits - m_new)
            l_ref[h] = alpha * l_ref[h] + jnp.sum(p, axis=1, keepdims=True)
            acc_ref[:, hs] = alpha * acc_ref[:, hs] + jnp.dot(p.astype(jnp.bfloat16), vh,
                                                              preferred_element_type=jnp.float32)
            m_ref[h] = m_new

    @pl.when(j == i)
    def _finish():
        for h in range(n_heads):
            hs = slice(h * B_HEAD, (h + 1) * B_HEAD)
            o_ref[0, :, hs] = acc_ref[:, hs] / l_ref[h]


def dsa_prompt(q, k, v, qi, ki, wi, rel_bias, tq=256):
    B, T, b_width = q.shape
    n_heads = b_width // B_HEAD
    tq = min(tq, T)
    assert T % tq == 0 and tq % 128 == 0 and tq + 1 >= T5_MAX_DIST
    nq = T // tq
    topk = min(TOPK_MAX, T // 4)
    kit = jnp.swapaxes(ki, 1, 2).astype(jnp.bfloat16)
    r = jnp.arange(tq, dtype=jnp.int32)
    dist = jnp.arange(3, dtype=jnp.int32)[:, None, None] * tq + r[None, :, None] - r[None, None, :]
    bias3 = jnp.transpose(rel_bias[t5_bucket(dist)], (0, 3, 1, 2))
    kern = functools.partial(_dsa_prompt_kernel, topk=topk, n_idx_bits=max(1, (T - 1).bit_length()))
    return pl.pallas_call(
        kern,
        grid=(B, nq, nq),
        in_specs=[pl.BlockSpec((1, tq, qi.shape[2]), lambda b, i, j: (b, i, 0)),
                  pl.BlockSpec((1, IDX_DIM, T), lambda b, i, j: (b, 0, 0)),
                  pl.BlockSpec((1, tq, IDX_HEADS), lambda b, i, j: (b, i, 0)),
                  pl.BlockSpec((1, tq, b_width), lambda b, i, j: (b, i, 0)),
                  pl.BlockSpec((1, tq, b_width), lambda b, i, j: (b, jnp.minimum(j, i), 0)),
                  pl.BlockSpec((1, tq, b_width), lambda b, i, j: (b, jnp.minimum(j, i), 0)),
                  pl.BlockSpec((1, n_heads, tq, tq), lambda b, i, j: (jnp.clip(i - j, 0, 2), 0, 0, 0))],
        out_specs=pl.BlockSpec((1, tq, b_width), lambda b, i, j: (b, i, 0)),
        out_shape=jax.ShapeDtypeStruct((B, T, b_width), jnp.float32),
        scratch_shapes=[pltpu.VMEM((nq, tq, tq), jnp.int32),
                        pltpu.VMEM((tq, 1), jnp.int32),
                        pltpu.VMEM((tq, qi.shape[2]), jnp.bfloat16),
                        pltpu.VMEM((tq, b_width), jnp.bfloat16),
                        pltpu.VMEM((n_heads, tq, 1), jnp.float32),
                        pltpu.VMEM((n_heads, tq, 1), jnp.float32),
                        pltpu.VMEM((tq, b_width), jnp.float32)],
        compiler_params=pltpu.CompilerParams(
            dimension_semantics=("parallel", "arbitrary", "arbitrary"),
            vmem_limit_bytes=VMEM_LIMIT_BYTES),
        name="dsa_prompt",
    )(qi, kit, wi, q, k, v, bias3)


SCAN_CHUNK = A_HEAD
LANES = 128
SUBLANES = 8


def _rwkv_scan_kernel(vt_ref, kk_ref, kka_ref, w_ref, km_ref, r_ref, s0_ref, ot_ref, st_ref, s_scr, *, n_steps):
    c = pl.program_id(0)
    n_batch = vt_ref.shape[0]
    n_tiles = vt_ref.shape[3] // LANES
    bf = jnp.bfloat16

    @pl.when(c == 0)
    def _():
        s_scr[...] = s0_ref[...]

    if n_steps < SCAN_CHUNK:
        ot_ref[...] = jnp.zeros_like(ot_ref)
    lane = lax.broadcasted_iota(jnp.int32, (A_HEAD, LANES), 1)
    seg_base = lane & (LANES - A_HEAD)
    lane_t = lane & (A_HEAD - 1)
    seg_ones = ((lax.broadcasted_iota(jnp.int32, (LANES, LANES), 0) // A_HEAD)
                == (lax.broadcasted_iota(jnp.int32, (LANES, LANES), 1) // A_HEAD)).astype(bf)

    group = min(n_steps, SUBLANES)
    tiles = [(b, slice(p * LANES, (p + 1) * LANES)) for b in range(n_batch) for p in range(n_tiles)]

    def seg_sums(xs):
        lhs = jnp.concatenate([x.astype(bf) for x in xs], axis=0)
        out = jnp.dot(lhs, seg_ones, preferred_element_type=jnp.float32)
        return [out[i * A_HEAD:(i + 1) * A_HEAD] for i in range(len(xs))]

    def step_group(g, carry):
        t0 = g * group if isinstance(g, int) else pl.multiple_of(g * group, group)
        rows = [[ref[b, pl.ds(t0, group), ls] for (b, ls) in tiles] for ref in (kk_ref, kka_ref, w_ref, km_ref, r_ref)]
        for u in range(group):
            kk_r, kka_r, w_r, km_r, r_r = ([x[u:u + 1] for x in q] for q in rows)
            s_kk = seg_sums([s_scr[b, :, ls] * kk_r[i] for i, (b, ls) in enumerate(tiles)])
            s_new = []
            for i, (b, ls) in enumerate(tiles):
                v_col = jnp.take_along_axis(vt_ref[b, 0, :, ls], seg_base + (t0 + u), axis=1)
                s = s_scr[b, :, ls] * w_r[i] - s_kk[i] * kka_r[i] + v_col * km_r[i]
                s_scr[b, :, ls] = s
                s_new.append(s * r_r[i])
            o_col = seg_sums(s_new)
            for i, (b, ls) in enumerate(tiles):
                ot_ref[b, 0, :, ls] = jnp.where(lane_t == t0 + u, o_col[i], ot_ref[b, 0, :, ls])
        return carry

    if n_steps == group:
        step_group(0, 0)
    else:
        lax.fori_loop(0, n_steps // group, step_group, 0)

    @pl.when(c == pl.num_programs(0) - 1)
    def _():
        st_ref[...] = s_scr[...]


def rwkv_scan(r, w, k, v, kk, a, state):
    B, T, width = r.shape
    n_heads = width // A_HEAD
    assert SCAN_CHUNK == A_HEAD and (T % SCAN_CHUNK == 0 or T < SCAN_CHUNK)
    n_chunks = pl.cdiv(T, SCAN_CHUNK)
    t_pad = n_chunks * SCAN_CHUNK
    pad = lambda x: jnp.pad(x, ((0, 0), (0, t_pad - T), (0, 0)))
    vt = jnp.transpose(pad(v).reshape(B, n_chunks, SCAN_CHUNK, n_heads, A_HEAD), (0, 1, 4, 3, 2))
    vt = vt.reshape(B, n_chunks, A_HEAD, width)
    st0 = jnp.transpose(state, (0, 2, 1, 3)).reshape(B, A_HEAD, width)
    row_spec = pl.BlockSpec((B, SCAN_CHUNK, width), lambda c: (0, c, 0))
    col_spec = pl.BlockSpec((B, 1, A_HEAD, width), lambda c: (0, c, 0, 0))
    st_spec = pl.BlockSpec((B, A_HEAD, width), lambda c: (0, 0, 0))
    ot, st = pl.pallas_call(
        functools.partial(_rwkv_scan_kernel, n_steps=min(T, SCAN_CHUNK)),
        grid=(n_chunks,),
        in_specs=[col_spec] + [row_spec] * 5 + [st_spec],
        out_specs=[col_spec, st_spec],
        out_shape=[jax.ShapeDtypeStruct((B, n_chunks, A_HEAD, width), jnp.float32),
                   jax.ShapeDtypeStruct((B, A_HEAD, width), jnp.float32)],
        scratch_shapes=[pltpu.VMEM((B, A_HEAD, width), jnp.float32)],
        compiler_params=pltpu.CompilerParams(
            dimension_semantics=("arbitrary",),
            vmem_limit_bytes=VMEM_LIMIT_BYTES),
        name="rwkv_scan",
    )(vt, pad(kk), pad(kk * a), pad(w), pad(k), pad(r), st0)
    o = jnp.transpose(ot.reshape(B, n_chunks, A_HEAD, n_heads, SCAN_CHUNK), (0, 1, 4, 3, 2)).reshape(B, t_pad, width)
    new_state = jnp.transpose(st.reshape(B, A_HEAD, n_heads, A_HEAD), (0, 2, 1, 3))
    return o[:, :T], new_state


def _split2(x):
    hi = x.astype(jnp.bfloat16)
    return hi, (x - hi.astype(jnp.float32)).astype(jnp.bfloat16)


def _dot_nt_split(a, b):
    a_hi, a_lo = _split2(a)
    b_hi, b_lo = _split2(b)
    return _dot_nt(jnp.concatenate([a_hi, a_hi, a_lo], axis=1), jnp.concatenate([b_hi, b_lo, b_hi], axis=1))


def _gla_segment_matrix(chunk):
    t = np.arange(chunk)[:, None]
    u = np.arange(chunk)[None, :]
    rows = []
    for lvl in range(chunk.bit_length() - 1):
        m = (chunk // 2) >> lvl
        blk = t // m
        right = np.logical_and(u >= blk * m, u <= t)
        left = np.logical_and(u > t, u < (blk + 1) * m)
        rows.append(np.where(blk % 2 == 1, right, left))
    rows.append(u <= t)
    rows.append(u > t)
    seg = np.concatenate(rows, axis=0).astype(np.float32)
    return np.concatenate([seg, seg, seg], axis=1)


def _dot_nt(a, b, **kw):
    return lax.dot_general(a, b, (((1,), (1,)), ((), ())), preferred_element_type=jnp.float32, **kw)


def _gla_kernel(seg_ref, q_ref, k_ref, v_ref, lf_ref, g_ref, gn_ref, s0_ref, y_ref, st_ref, s_scr):
    c = pl.program_id(1)
    chunk = q_ref.shape[1]
    n_levels = chunk.bit_length() - 1
    n_heads = s_scr.shape[0]
    bf = jnp.bfloat16

    @pl.when(c == 0)
    def _():
        s_scr[...] = s0_ref[0]

    row = lax.broadcasted_iota(jnp.int32, (chunk, chunk), 0)
    col = lax.broadcasted_iota(jnp.int32, (chunk, chunk), 1)
    row_w = lax.broadcasted_iota(jnp.int32, (chunk, C_EXPAND), 0)
    seg = seg_ref[...]
    gn = gn_ref[...]

    def head(h, carry):
        ls = pl.ds(pl.multiple_of(h * C_EXPAND, C_EXPAND), C_EXPAND)
        q, k, v, lf = q_ref[0, :, ls], k_ref[0, :, ls], v_ref[0, :, ls], lf_ref[0, :, ls]
        lf_hi = lf.astype(bf)
        lf_r = lf - lf_hi.astype(jnp.float32)
        lf_mid = lf_r.astype(bf)
        lf_lo = (lf_r - lf_mid.astype(jnp.float32)).astype(bf)
        e = jnp.exp(jnp.dot(seg, jnp.concatenate([lf_hi, lf_mid, lf_lo], axis=0), preferred_element_type=jnp.float32))
        a = jnp.where(row == col, _dot_nt_split(q, k), 0.0)
        for lvl in range(n_levels):
            sh = n_levels - 1 - lvl
            x = jnp.where(((row_w >> sh) & 1) == 1, q, k) * e[lvl * chunk:(lvl + 1) * chunk]
            pair = (((row >> sh) & 1) == 1) & ((col >> sh) == (row >> sh) - 1)
            a = a + jnp.where(pair, _dot_nt_split(x, x), 0.0)
        e_b = e[n_levels * chunk:(n_levels + 1) * chunk]
        e_r = e[(n_levels + 1) * chunk:]
        st = s_scr[h]
        o = jnp.dot(a.astype(bf), v.astype(bf), preferred_element_type=jnp.float32)
        o = o + _dot_nt((q * e_b).astype(bf), st.astype(bf))
        s_scr[h] = st * e_b[chunk - 1:chunk, :] + lax.dot_general(
            v.astype(bf), (k * e_r).astype(bf), (((0,), (0,)), ((), ())), preferred_element_type=jnp.float32)
        o = o * lax.rsqrt(jnp.mean(o * o, axis=1, keepdims=True) + RMS_EPS) * gn
        gate = g_ref[0, :, ls]
        y_ref[0, :, ls] = o * (gate * jax.nn.sigmoid(gate))
        return carry

    lax.fori_loop(0, n_heads, head, 0, unroll=2)

    @pl.when(c == pl.num_programs(1) - 1)
    def _():
        st_ref[0] = s_scr[...]


def gla_gated(q, k, v, log_f, g, gn_g, state):
    B, T, width = q.shape
    n_heads = width // C_EXPAND
    assert v.shape[2] == width, "kernel assumes dk == dv == C_EXPAND"
    n_chunks = pl.cdiv(T, GLA_CHUNK)
    t_pad = n_chunks * GLA_CHUNK
    pad = lambda x: jnp.pad(x, ((0, 0), (0, t_pad - T), (0, 0)))
    seg = jnp.asarray(_gla_segment_matrix(GLA_CHUNK), dtype=jnp.bfloat16)
    st0 = jnp.swapaxes(state, 2, 3)
    tok_spec = pl.BlockSpec((1, GLA_CHUNK, width), lambda b, c: (b, c, 0))
    st_spec = pl.BlockSpec((1, n_heads, C_EXPAND, C_EXPAND), lambda b, c: (b, 0, 0, 0))
    y, st = pl.pallas_call(
        _gla_kernel,
        grid=(B, n_chunks),
        in_specs=[pl.BlockSpec(seg.shape, lambda b, c: (0, 0))] + [tok_spec] * 5
                 + [pl.BlockSpec((1, C_EXPAND), lambda b, c: (0, 0)), st_spec],
        out_specs=[tok_spec, st_spec],
        out_shape=[jax.ShapeDtypeStruct((B, t_pad, width), jnp.float32),
                   jax.ShapeDtypeStruct(st0.shape, jnp.float32)],
        scratch_shapes=[pltpu.VMEM((n_heads, C_EXPAND, C_EXPAND), jnp.float32)],
        compiler_params=pltpu.CompilerParams(
            dimension_semantics=("parallel", "arbitrary"),
            vmem_limit_bytes=VMEM_LIMIT_BYTES),
        name="gla_gated",
    )(seg, pad(q), pad(k), pad(v), pad(log_f), pad(g), gn_g.reshape(1, C_EXPAND), st0)
    return y[:, :T], jnp.swapaxes(st, 2, 3)


def _moe_up_kernel(te_ref, x_ref, wg_ref, wu_ref, o_ref):
    del te_ref
    xb = x_ref[...].astype(jnp.bfloat16)
    g = jnp.dot(xb, wg_ref[0].astype(jnp.bfloat16), preferred_element_type=jnp.float32)
    u = jnp.dot(xb, wu_ref[0].astype(jnp.bfloat16), preferred_element_type=jnp.float32)
    o_ref[...] = (g * jax.nn.sigmoid(g) * u).astype(o_ref.dtype)


def _moe_down_kernel(te_ref, h_ref, wd_ref, o_ref):
    del te_ref
    o_ref[...] = jnp.dot(h_ref[...].astype(jnp.bfloat16), wd_ref[0].astype(jnp.bfloat16),
                         preferred_element_type=jnp.float32)


def moe_experts(tile_expert, xs, wg, wu, wd, tm, tf=704, tn=1024):
    rows, d_model = xs.shape
    d_ff = wg.shape[2]
    tf = _pick_tile(d_ff, tf, 128)
    tn = _pick_tile(d_model, tn, 128)
    params = pltpu.CompilerParams(dimension_semantics=("parallel", "arbitrary"), vmem_limit_bytes=VMEM_LIMIT_BYTES)
    h = pl.pallas_call(
        _moe_up_kernel,
        grid_spec=pltpu.PrefetchScalarGridSpec(
            num_scalar_prefetch=1, grid=(d_ff // tf, rows // tm),
            in_specs=[pl.BlockSpec((tm, d_model), lambda j, i, te: (i, 0)),
                      pl.BlockSpec((1, d_model, tf), lambda j, i, te: (te[i], 0, j)),
                      pl.BlockSpec((1, d_model, tf), lambda j, i, te: (te[i], 0, j))],
            out_specs=pl.BlockSpec((tm, tf), lambda j, i, te: (i, j))),
        out_shape=jax.ShapeDtypeStruct((rows, d_ff), xs.dtype),
        compiler_params=params, name="moe_up",
    )(tile_expert, xs, wg, wu)
    return pl.pallas_call(
        _moe_down_kernel,
        grid_spec=pltpu.PrefetchScalarGridSpec(
            num_scalar_prefetch=1, grid=(d_model // tn, rows // tm),
            in_specs=[pl.BlockSpec((tm, d_ff), lambda j, i, te: (i, 0)),
                      pl.BlockSpec((1, d_ff, tn), lambda j, i, te: (te[i], 0, j))],
            out_specs=pl.BlockSpec((tm, tn), lambda j, i, te: (i, j))),
        out_shape=jax.ShapeDtypeStruct((rows, d_model), jnp.float32),
        compiler_params=params, name="moe_down",
    )(tile_expert, h, wd)


def moe_swiglu(x, w_router, wg, wu, wd):
    lead, d_model = x.shape[:-1], x.shape[-1]
    x = x.reshape(-1, d_model)
    n_tok = x.shape[0]
    n_pairs = n_tok * TOP_EXPERTS
    tm = min(512, n_pairs)
    assert n_pairs % tm == 0
    x = _mxu_operand(x)
    probs = jax.nn.softmax(matmul(x, w_router), axis=-1)
    top_p, top_i = lax.top_k(probs, TOP_EXPERTS)
    top_p = top_p / jnp.sum(top_p, axis=-1, keepdims=True)
    pair_expert = top_i.reshape(-1)
    order = jnp.argsort(pair_expert, stable=True)
    counts = jnp.sum(jax.nn.one_hot(pair_expert, N_EXPERTS, dtype=jnp.int32), axis=0)
    padded = (counts + tm - 1) // tm * tm
    padded_end = jnp.cumsum(padded)
    sorted_expert = pair_expert[order]
    rank = jnp.arange(n_pairs, dtype=jnp.int32) - (jnp.cumsum(counts) - counts)[sorted_expert]
    dest = (padded_end - padded)[sorted_expert] + rank
    rows = n_pairs + N_EXPERTS * tm
    row_token = jnp.zeros((rows,), jnp.int32).at[dest].set((order // TOP_EXPERTS).astype(jnp.int32))
    pair_row = jnp.zeros((n_pairs,), jnp.int32).at[order].set(dest.astype(jnp.int32))
    tile_start = jnp.arange(rows // tm, dtype=jnp.int32) * tm
    tile_expert = jnp.minimum(jnp.searchsorted(padded_end, tile_start, side='right'), N_EXPERTS - 1).astype(jnp.int32)
    y_rows = moe_experts(tile_expert, x[row_token], wg, wu, wd, tm)
    y_pairs = y_rows[pair_row].reshape(n_tok, TOP_EXPERTS, d_model)
    y = jnp.sum(top_p[..., None] * y_pairs, axis=1)
    return y.reshape(lead + (d_model,))


def split_cols(a, sizes):
    return jnp.split(a, np.cumsum(sizes)[:-1].tolist(), axis=-1)


def gather_rows(a, idx):
    return jax.vmap(lambda a_b, i_b: a_b[i_b])(a, idx)


def layer_norm(x, g, b):
    mu = jnp.mean(x, axis=-1, keepdims=True)
    var = jnp.mean(jnp.square(x - mu), axis=-1, keepdims=True)
    return (x - mu) * lax.rsqrt(var + LN_EPS) * g + b


def t5_bucket(dist):
    dist = jnp.maximum(dist, 0)
    max_exact = T5_BUCKETS // 2
    d = jnp.maximum(dist, 1).astype(jnp.float32)
    large = max_exact + (jnp.log(d / max_exact) / math.log(T5_MAX_DIST / max_exact) * (T5_BUCKETS - max_exact)).astype(jnp.int32)
    large = jnp.minimum(large, T5_BUCKETS - 1)
    return jnp.where(dist < max_exact, dist, large)


def dsa_attend(q, qi, wi, q_pos, k_idx, gather_kv, rel_bias, topk):
    B, T = q.shape[:2]
    L = k_idx.shape[1]
    blk = min(Q_BLOCK, T)
    nb = -(-T // blk)
    pad = nb * blk - T

    def to_blocks(a):
        a = jnp.pad(a, [(0, 0), (0, pad)] + [(0, 0)] * (a.ndim - 2))
        return jnp.moveaxis(a.reshape((B, nb, blk) + a.shape[2:]), 1, 0)

    pos_b = jnp.pad(q_pos, (0, pad)).reshape(nb, blk)
    k_pos = jnp.arange(L, dtype=jnp.int32)

    def one_block(args):
        qb, qib, wib, pb = args
        s = jnp.einsum('bthd,bsd->bths', qib, k_idx) * IDX_DIM ** -0.5
        score = jnp.einsum('bth,bths->bts', wib, jax.nn.relu(s))
        score = jnp.where(k_pos[None, None, :] <= pb[None, :, None], score, -jnp.inf)
        _, sel = lax.top_k(score, topk)
        valid = sel <= pb[None, :, None]
        k_sel, v_sel = gather_kv(sel)
        logits = jnp.einsum('bthd,btkhd->bhtk', qb, k_sel) * B_HEAD ** -0.5
        bias = rel_bias[t5_bucket(pb[None, :, None] - sel)]
        logits = logits + jnp.transpose(bias, (0, 3, 1, 2))
        logits = jnp.where(valid[:, None], logits, -jnp.inf)
        p = jax.nn.softmax(logits, axis=-1)
        return jnp.einsum('bhtk,btkhd->bthd', p, v_sel)

    out = lax.map(one_block, (to_blocks(q), to_blocks(qi), to_blocks(wi), pos_b))
    return jnp.moveaxis(out, 0, 1).reshape((B, nb * blk) + q.shape[2:])[:, :T]


def attend_self(q, k, v, qi, ki, wi, rel_bias):
    T = q.shape[1]
    gather = lambda sel: (gather_rows(k, sel), gather_rows(v, sel))
    return dsa_attend(q, qi, wi, jnp.arange(T, dtype=jnp.int32), ki, gather, rel_bias, min(TOPK_MAX, T // 4))


def attend_paged(q, k, v, qi, ki, wi, cache_k, cache_v, cache_kidx, layer, page_table, rel_bias):
    B, T = q.shape[:2]
    past = page_table.shape[1] * PAGE_SIZE
    pages = page_table + layer * cache_k.shape[1]
    kidx_past = cache_kidx.reshape((-1,) + cache_kidx.shape[2:])[pages].reshape(B, past, IDX_DIM)
    kidx_all = jnp.concatenate([kidx_past, ki], axis=1)
    flat_k = cache_k.reshape((-1,) + cache_k.shape[3:])
    flat_v = cache_v.reshape((-1,) + cache_v.shape[3:])

    def gather(sel):
        ps = jnp.minimum(sel, past - 1)
        row = gather_rows(pages, ps // PAGE_SIZE) * PAGE_SIZE + ps % PAGE_SIZE
        is_new = (sel >= past)[..., None, None]
        ns = jnp.clip(sel - past, 0, T - 1)
        k_sel = jnp.where(is_new, gather_rows(k, ns), flat_k[row])
        v_sel = jnp.where(is_new, gather_rows(v, ns), flat_v[row])
        return k_sel, v_sel

    q_pos = past + jnp.arange(T, dtype=jnp.int32)
    return dsa_attend(q, qi, wi, q_pos, kidx_all, gather, rel_bias, min(TOPK_MAX, (past + T) // 4))


def rwkv7_mix(p_a, shift_state, wkv_state, mu, w0, w2, a0, a2, g2, kk_s, ka_s, rk_s, lnx_g, lnx_b):
    B, T, _ = p_a.shape
    a_width = w0.shape[0]
    a_heads = a_width // A_HEAD
    prev = jnp.concatenate([shift_state[:, None, :], p_a[:, :-1]], axis=1)
    xm = p_a + (prev - p_a) * mu
    r, k, v, wd, ad, gd = split_cols(xm, [a_width, a_width, a_width, W_LORA, AA_LORA, G_LORA])
    w_pre = w0 + mm(jnp.tanh(wd), w2)
    decay = jnp.exp(-jnp.exp(-jax.nn.softplus(-w_pre) - 0.5))
    a = jax.nn.sigmoid(a0 + mm(ad, a2))
    g = mm(jax.nn.sigmoid(gd), g2)
    heads = lambda t: t.reshape(B, T, a_heads, A_HEAD)
    kk = heads(k * kk_s)
    kk = kk / jnp.maximum(jnp.sqrt(jnp.sum(jnp.square(kk), axis=-1, keepdims=True)), 1e-12)
    k_mod = heads(k * (1.0 + (a - 1.0) * ka_s))
    r_h, v_h = heads(r), heads(v)
    o, S = rwkv_scan(r, decay, k_mod.reshape(B, T, a_width), v, kk.reshape(B, T, a_width), a, wkv_state)
    o = heads(o)
    mu_o = jnp.mean(o, axis=-1, keepdims=True)
    var_o = jnp.mean(jnp.square(o - mu_o), axis=-1, keepdims=True)
    o = ((o - mu_o) * lax.rsqrt(var_o + RWKV_GN_EPS)).reshape(B, T, a_width) * lnx_g + lnx_b
    bonus = jnp.sum(r_h * k_mod * rk_s.reshape(a_heads, A_HEAD), axis=-1, keepdims=True) * v_h
    y = (o + bonus.reshape(B, T, a_width)) * g
    return y, p_a[:, -1], S


def ab_mixer(x, shift_state, wkv_state, attend, w_in, rw, w_out):
    B, T, d_model = x.shape
    a_width = rw[1].shape[0]
    a_proj = 3 * a_width + W_LORA + AA_LORA + G_LORA
    b_width = d_model - a_width
    b_heads = b_width // B_HEAD
    proj = mm(x, w_in)
    y_a, new_shift, new_wkv = rwkv7_mix(proj[..., :a_proj], shift_state, wkv_state, *rw)
    q, k, v, qi, ki, wi = split_cols(proj[..., a_proj:], [b_width, b_width, b_width, IDX_HEADS * IDX_DIM, IDX_DIM, IDX_HEADS])
    y_b = attend(q, k, v, qi, ki, wi)
    y = mm(jnp.concatenate([y_a, y_b], axis=-1), w_out)
    heads = lambda t: t.reshape(B, T, b_heads, B_HEAD)
    return y, new_shift, new_wkv, heads(k), heads(v), ki


def attend_sample(q, k, v, qi, ki, wi, **kw):
    B, T, b_width = q.shape
    heads = lambda t: t.reshape(B, T, b_width // B_HEAD, B_HEAD)
    y = attend_paged(heads(q), heads(k), heads(v), qi.reshape(B, T, IDX_HEADS, IDX_DIM), ki, wi * IDX_HEADS ** -0.5, **kw)
    return y.reshape(B, T, b_width)


def hgrn2_mixer(x, state, w_in, lb, gn_g, w_out):
    B, T, c_width = x.shape
    q, f_raw, i, g = split_cols(mm(x, w_in), [c_width, c_width, c_width, c_width])
    log_f = jnp.logaddexp(jnp.log(lb), jnp.log1p(-lb) + jax.nn.log_sigmoid(f_raw))
    k_in = (1.0 - lb) * jax.nn.sigmoid(-f_raw)
    y, new_state = gla_gated(jax.nn.silu(q), k_in, i, log_f, g, gn_g, state)
    return mm(y, w_out), new_state


def kernel(x_prompt, x_sample, cache_k, cache_v, cache_kidx, state_wkv, state_shift, state_hgrn, page_table, w_in_ab, rwkv_mu, rwkv_w0, rwkv_w2, rwkv_a0, rwkv_a2, rwkv_g2, rwkv_kk, rwkv_ka, rwkv_rk, rwkv_lnx_g, rwkv_lnx_b, w_out_ab, rel_bias, ffn_w_gate, ffn_w_up, ffn_w_down, w_in_c, hgrn_lb, hgrn_gnorm, w_out_c, router, moe_w_gate, moe_w_up, moe_w_down, ln1_g, ln1_b, ln2_g, ln2_b):
    depth = ln1_g.shape[0]
    alpha = (2 * depth) ** 0.25
    a_width = rwkv_w0.shape[1]
    a_heads = a_width // A_HEAD
    a_proj = state_shift.shape[-1]
    c_heads, c_dk, c_dv = state_hgrn.shape[2:]
    lb_all = jnp.cumsum(jax.nn.softmax(hgrn_lb, axis=0), axis=0)
    lb_all = lb_all - lb_all[0:1]
    attend_p = functools.partial(dsa_prompt, rel_bias=rel_bias)
    xp, xs = x_prompt, x_sample
    kp_l, vp_l, kip_l, wkvp_l, shp_l, hgp_l = [], [], [], [], [], []
    ks_l, vs_l, kis_l, wkvs_l, shs_l, hgs_l = [], [], [], [], [], []
    for l in range(depth):
        j = l // 2
        if l % 2 == 0:
            rw = (rwkv_mu[j], rwkv_w0[j], rwkv_w2[j], rwkv_a0[j], rwkv_a2[j], rwkv_g2[j],
                  rwkv_kk[j], rwkv_ka[j], rwkv_rk[j], rwkv_lnx_g[j], rwkv_lnx_b[j])
            attend_s = functools.partial(attend_sample, cache_k=cache_k, cache_v=cache_v, cache_kidx=cache_kidx,
                                         layer=j, page_table=page_table, rel_bias=rel_bias)
            shift0 = jnp.zeros((xp.shape[0], a_proj), xp.dtype)
            wkv0 = jnp.zeros((xp.shape[0], a_heads, A_HEAD, A_HEAD), jnp.float32)
            yp, shp, wkvp, kp, vp, kip = ab_mixer(xp, shift0, wkv0, attend_p, w_in_ab[j], rw, w_out_ab[j])
            ys, shs, wkvs, kss, vss, kis = ab_mixer(xs, state_shift[j], state_wkv[j], attend_s, w_in_ab[j], rw, w_out_ab[j])
            kp_l.append(kp); vp_l.append(vp); kip_l.append(kip); wkvp_l.append(wkvp); shp_l.append(shp)
            ks_l.append(kss); vs_l.append(vss); kis_l.append(kis); wkvs_l.append(wkvs); shs_l.append(shs)
        else:
            hg0 = jnp.zeros((xp.shape[0], c_heads, c_dk, c_dv), jnp.float32)
            yp, hgp = hgrn2_mixer(xp, hg0, w_in_c[j], lb_all[j], hgrn_gnorm[j], w_out_c[j])
            ys, hgs = hgrn2_mixer(xs, state_hgrn[j], w_in_c[j], lb_all[j], hgrn_gnorm[j], w_out_c[j])
            hgp_l.append(hgp); hgs_l.append(hgs)
        xp = layer_norm(alpha * xp + yp, ln1_g[l], ln1_b[l])
        xs = layer_norm(alpha * xs + ys, ln1_g[l], ln1_b[l])
        if l % 2 == 0:
            fp = swiglu(xp, ffn_w_gate[j], ffn_w_up[j], ffn_w_down[j])
            fs = swiglu(xs, ffn_w_gate[j], ffn_w_up[j], ffn_w_down[j])
        else:
            fp = moe_swiglu(xp, router[j], moe_w_gate[j], moe_w_up[j], moe_w_down[j])
            fs = moe_swiglu(xs, router[j], moe_w_gate[j], moe_w_up[j], moe_w_down[j])
        xp = layer_norm(alpha * xp + fp, ln2_g[l], ln2_b[l])
        xs = layer_norm(alpha * xs + fs, ln2_g[l], ln2_b[l])
    return (xp, xs, jnp.stack(kp_l), jnp.stack(vp_l), jnp.stack(kip_l), jnp.stack(wkvp_l), jnp.stack(shp_l),
            jnp.stack(hgp_l), jnp.stack(ks_l), jnp.stack(vs_l), jnp.stack(kis_l), jnp.stack(wkvs_l),
            jnp.stack(shs_l), jnp.stack(hgs_l))
```

```python
import functools
import math

import jax
import jax.numpy as jnp
import numpy as np
from jax import lax
from jax.experimental import pallas as pl
from jax.experimental.pallas import tpu as pltpu

A_HEAD = 64
W_LORA = 64
AA_LORA = 64
G_LORA = 128
RWKV_GN_EPS = 64e-5
B_HEAD = 128
IDX_HEADS = 16
IDX_DIM = 64
TOPK_MAX = 256
Q_BLOCK = 64
T5_BUCKETS = 32
T5_MAX_DIST = 128
C_EXPAND = 128
GLA_CHUNK = 64
GLA_HEAD_GROUP = 4
RMS_EPS = 1e-5
N_EXPERTS = 8
TOP_EXPERTS = 2
PAGE_SIZE = 128
LN_EPS = 1e-5

VMEM_LIMIT_BYTES = 56 * 1024 * 1024
BF16_ROWS_MIN = 256


def _pick_tile(n, target, align):
    if n <= target:
        return n
    t = (target // align) * align
    while t >= align:
        if n % t == 0:
            return t
        t -= align
    return n


def _layer(w):
    return w if isinstance(w, tuple) else (w[None], 0)


def _matmul_kernel(x_ref, w_ref, o_ref):
    @pl.when(pl.program_id(2) == 0)
    def _():
        o_ref[...] = jnp.zeros_like(o_ref)

    o_ref[...] += jnp.dot(x_ref[...].astype(jnp.bfloat16), w_ref[0].astype(jnp.bfloat16),
                          preferred_element_type=jnp.float32)


def matmul(x, w, tm=1024, tn=1024, tk=2048):
    m, k = x.shape
    w, layer = _layer(w)
    k2, n = w.shape[1:]
    assert k == k2
    tm = _pick_tile(m, tm, 8)
    tk = _pick_tile(k, tk, 128)
    tn = min(tn, n) if n % 128 == 0 or n < 128 else tn
    grid = (pl.cdiv(n, tn), pl.cdiv(m, tm), k // tk)
    return pl.pallas_call(
        _matmul_kernel,
        grid=grid,
        in_specs=[pl.BlockSpec((tm, tk), lambda j, i, kk: (i, kk)),
                  pl.BlockSpec((1, tk, tn), lambda j, i, kk: (layer, kk, j))],
        out_specs=pl.BlockSpec((tm, tn), lambda j, i, kk: (i, j)),
        out_shape=jax.ShapeDtypeStruct((m, n), jnp.float32),
        compiler_params=pltpu.CompilerParams(
            dimension_semantics=("parallel", "parallel", "arbitrary"),
            vmem_limit_bytes=VMEM_LIMIT_BYTES),
        name="matmul",
    )(x, w)


def _swiglu_up_kernel(x_ref, wg_ref, wu_ref, o_ref):
    xb = x_ref[...].astype(jnp.bfloat16)
    g = jnp.dot(xb, wg_ref[0].astype(jnp.bfloat16), preferred_element_type=jnp.float32)
    u = jnp.dot(xb, wu_ref[0].astype(jnp.bfloat16), preferred_element_type=jnp.float32)
    o_ref[...] = (g * jax.nn.sigmoid(g) * u).astype(o_ref.dtype)


def swiglu_up(x, wg, wu, tm=1024, tn=512):
    m, k = x.shape
    (wg, layer), (wu, _) = _layer(wg), _layer(wu)
    n = wg.shape[2]
    tm = _pick_tile(m, tm, 8)
    tn = _pick_tile(n, tn, 128)
    return pl.pallas_call(
        _swiglu_up_kernel,
        grid=(n // tn, m // tm),
        in_specs=[pl.BlockSpec((tm, k), lambda j, i: (i, 0)),
                  pl.BlockSpec((1, k, tn), lambda j, i: (layer, 0, j)),
                  pl.BlockSpec((1, k, tn), lambda j, i: (layer, 0, j))],
        out_specs=pl.BlockSpec((tm, tn), lambda j, i: (i, j)),
        out_shape=jax.ShapeDtypeStruct((m, n), x.dtype),
        compiler_params=pltpu.CompilerParams(
            dimension_semantics=("parallel", "parallel"),
            vmem_limit_bytes=VMEM_LIMIT_BYTES),
        name="swiglu_up",
    )(x, wg, wu)


def _mxu_operand(x):
    x = x.reshape(-1, x.shape[-1])
    return x.astype(jnp.bfloat16) if x.shape[0] >= BF16_ROWS_MIN else x


def mm(x, w):
    return matmul(_mxu_operand(x), w).reshape(x.shape[:-1] + (_layer(w)[0].shape[2],))


def swiglu(x, wg, wu, wd):
    h = swiglu_up(_mxu_operand(x), wg, wu)
    return matmul(h, wd).reshape(x.shape[:-1] + (_layer(wd)[0].shape[2],))


INT_MIN = -2 ** 31
NEG_BIG = -1e30


def _sortable_key(x):
    bits = pltpu.bitcast(jnp.where(x == 0.0, 0.0, x), jnp.int32)
    return bits ^ ((bits >> 31) & 0x7FFFFFFF)


def _dsa_prompt_kernel(qi_ref, ki_ref, wit_ref, q_ref, k_ref, v_ref, bias_ref, o_ref,
                       keys_ref, tau_ref, qib_ref, qb_ref, m_ref, l_ref, acc_ref, *, topk, n_idx_bits):
    i = pl.program_id(1)
    j = pl.program_id(2)
    tq = q_ref.shape[1]
    n_heads = q_ref.shape[2] // B_HEAD
    kf = float(topk)

    def key_sum(x):
        part = x[:SUBLANES]
        for u in range(1, tq // SUBLANES):
            part = part + x[u * SUBLANES:(u + 1) * SUBLANES]
        return part

    @pl.when(j == 0)
    def _select():
        qib_ref[...] = (qi_ref[0] * IDX_DIM ** -0.5).astype(jnp.bfloat16)
        qb_ref[...] = q_ref[0].astype(jnp.bfloat16)
        m_ref[...] = jnp.full_like(m_ref, NEG_BIG)
        l_ref[...] = jnp.zeros_like(l_ref)
        acc_ref[...] = jnp.zeros_like(acc_ref)
        wb = (wit_ref[0] * IDX_HEADS ** -0.5).astype(jnp.bfloat16).astype(jnp.float32)
        key_pos = lax.broadcasted_iota(jnp.int32, (tq, tq), 0)
        qry_pos = i * tq + lax.broadcasted_iota(jnp.int32, (tq, tq), 1)

        def score_chunk(c, carry):
            kc = ki_ref[0, pl.ds(pl.multiple_of(c * tq, tq), tq), :]
            score = jnp.zeros((tq, tq), jnp.float32)
            for h in range(IDX_HEADS):
                s = _dot_nt(kc, qib_ref[:, h * IDX_DIM:(h + 1) * IDX_DIM])
                r = jnp.maximum(s, 0.0).astype(jnp.bfloat16).astype(jnp.float32)
                score = score + wb[h:h + 1, :] * r
            keys_ref[c] = jnp.where(c * tq + key_pos <= qry_pos, _sortable_key(score), INT_MIN)
            return carry

        lax.fori_loop(0, i + 1, score_chunk, 0)

        def count(pred):
            def body(c, acc):
                return acc + key_sum(jnp.where(pred(keys_ref[c], c), 1.0, 0.0))
            acc = lax.fori_loop(0, i + 1, body, jnp.zeros((SUBLANES, tq), jnp.float32))
            return jnp.sum(acc, axis=0, keepdims=True)

        tau = jnp.where(count(lambda kc, c: kc >= 0) >= kf, 0, INT_MIN).astype(jnp.int32)

        def bit_step(n, tau):
            cand = tau | jnp.left_shift(jnp.int32(1), 30 - n)
            return jnp.where(count(lambda kc, c: kc >= cand) >= kf, cand, tau)

        tau = lax.fori_loop(0, 31, bit_step, tau)
        tau = jnp.maximum(tau, INT_MIN + 1)
        tau_ref[...] = tau
        n_ge = count(lambda kc, c: kc >= tau)

        @pl.when(jnp.max(n_ge) > kf)
        def _ties():
            need = kf - count(lambda kc, c: kc > tau)

            def idx_step(n, ans):
                cand = ans + jnp.left_shift(jnp.int32(1), n_idx_bits - 1 - n)
                f = count(lambda kc, c: (kc == tau) & (c * tq + key_pos < cand))
                return jnp.where(f < need, cand, ans)

            last = lax.fori_loop(0, n_idx_bits, idx_step, jnp.zeros((1, tq), jnp.int32))

            def demote(c, carry):
                kc = keys_ref[c]
                keys_ref[c] = jnp.where((kc == tau) & (c * tq + key_pos > last), INT_MIN, kc)
                return carry

            lax.fori_loop(0, i + 1, demote, 0)

    @pl.when(j <= i)
    def _attend():
        sel = keys_ref[j] >= tau_ref[...]
        heads = [slice(h * B_HEAD, (h + 1) * B_HEAD) for h in range(n_heads)]
        raw = [_dot_nt(k_ref[0, :, hs].astype(jnp.bfloat16), qb_ref[:, hs]) for hs in heads]
        probs, alphas = [], []
        for h in range(n_heads):
            logits = jnp.where(sel, raw[h] * B_HEAD ** -0.5 + bias_ref[0, h], NEG_BIG)
            m_old = m_ref[h]
            m_new = jnp.maximum(m_old, jnp.max(logits, axis=0, keepdims=True))
            alpha = jnp.exp(m_old - m_new)
            p = jnp.exp(logits - m_new)
            l_ref[h] = alpha * l_ref[h] + jnp.sum(p, axis=0, keepdims=True)
            m_ref[h] = m_new
            probs.append(p.astype(jnp.bfloat16))
            alphas.append(alpha)
        for h, hs in enumerate(heads):
            acc_ref[h] = alphas[h] * acc_ref[h] + lax.dot_general(
                v_ref[0, :, hs].astype(jnp.bfloat16), probs[h], (((0,), (0,)), ((), ())),
                preferred_element_type=jnp.float32)

    @pl.when(j == i)
    def _finish():
        for h in range(n_heads):
            o_ref[0, :, h * B_HEAD:(h + 1) * B_HEAD] = jnp.transpose(acc_ref[h] / l_ref[h])


def dsa_prompt(q, k, v, qi, ki, wi, rel_bias, tq=256):
    B, T, b_width = q.shape
    n_heads = b_width // B_HEAD
    tq = min(tq, T)
    assert T % tq == 0 and tq % 128 == 0 and tq + 1 >= T5_MAX_DIST
    nq = T // tq
    topk = min(TOPK_MAX, T // 4)
    r = jnp.arange(tq, dtype=jnp.int32)
    dist = jnp.arange(3, dtype=jnp.int32)[:, None, None] * tq + r[None, None, :] - r[None, :, None]
    bias3 = jnp.einsum('dkqc,ch->dhkq', jax.nn.one_hot(t5_bucket(dist), T5_BUCKETS, dtype=jnp.float32), rel_bias,
                       precision=lax.Precision.HIGHEST)
    kern = functools.partial(_dsa_prompt_kernel, topk=topk, n_idx_bits=max(1, (T - 1).bit_length()))
    return pl.pallas_call(
        kern,
        grid=(B, nq, nq),
        in_specs=[pl.BlockSpec((1, tq, qi.shape[2]), lambda b, i, j: (b, i, 0)),
                  pl.BlockSpec((1, T, IDX_DIM), lambda b, i, j: (b, 0, 0)),
                  pl.BlockSpec((1, IDX_HEADS, tq), lambda b, i, j: (b, 0, i)),
                  pl.BlockSpec((1, tq, b_width), lambda b, i, j: (b, i, 0)),
                  pl.BlockSpec((1, tq, b_width), lambda b, i, j: (b, jnp.minimum(j, i), 0)),
                  pl.BlockSpec((1, tq, b_width), lambda b, i, j: (b, jnp.minimum(j, i), 0)),
                  pl.BlockSpec((1, n_heads, tq, tq), lambda b, i, j: (jnp.clip(i - j, 0, 2), 0, 0, 0))],
        out_specs=pl.BlockSpec((1, tq, b_width), lambda b, i, j: (b, i, 0)),
        out_shape=jax.ShapeDtypeStruct((B, T, b_width), jnp.float32),
        scratch_shapes=[pltpu.VMEM((nq, tq, tq), jnp.int32),
                        pltpu.VMEM((1, tq), jnp.int32),
                        pltpu.VMEM((tq, qi.shape[2]), jnp.bfloat16),
                        pltpu.VMEM((tq, b_width), jnp.bfloat16),
                        pltpu.VMEM((n_heads, 1, tq), jnp.float32),
                        pltpu.VMEM((n_heads, 1, tq), jnp.float32),
                        pltpu.VMEM((n_heads, B_HEAD, tq), jnp.float32)],
        compiler_params=pltpu.CompilerParams(
            dimension_semantics=("parallel", "arbitrary", "arbitrary"),
            vmem_limit_bytes=VMEM_LIMIT_BYTES),
        name="dsa_prompt",
    )(qi, ki.astype(jnp.bfloat16), jnp.swapaxes(wi, 1, 2), q, k, v, bias3)


SCAN_CHUNK = A_HEAD
LANES = 128
SUBLANES = 8


def _rwkv_scan_kernel(vt_ref, kk_ref, kka_ref, w_ref, km_ref, r_ref, s0_ref, ot_ref, st_ref, s_scr, *, n_steps):
    c = pl.program_id(0)
    n_batch = vt_ref.shape[0]
    n_tiles = vt_ref.shape[3] // LANES
    bf = jnp.bfloat16

    @pl.when(c == 0)
    def _():
        s_scr[...] = s0_ref[...]

    if n_steps < SCAN_CHUNK:
        ot_ref[...] = jnp.zeros_like(ot_ref)
    lane = lax.broadcasted_iota(jnp.int32, (A_HEAD, LANES), 1)
    seg_base = lane & (LANES - A_HEAD)
    lane_t = lane & (A_HEAD - 1)
    seg_ones = ((lax.broadcasted_iota(jnp.int32, (LANES, LANES), 0) // A_HEAD)
                == (lax.broadcasted_iota(jnp.int32, (LANES, LANES), 1) // A_HEAD)).astype(bf)

    group = min(n_steps, SUBLANES)
    tiles = [(b, slice(p * LANES, (p + 1) * LANES)) for b in range(n_batch) for p in range(n_tiles)]

    def seg_sums(xs):
        lhs = jnp.concatenate([x.astype(bf) for x in xs], axis=0)
        out = jnp.dot(lhs, seg_ones, preferred_element_type=jnp.float32)
        return [out[i * A_HEAD:(i + 1) * A_HEAD] for i in range(len(xs))]

    def step_group(g, carry):
        t0 = g * group if isinstance(g, int) else pl.multiple_of(g * group, group)
        rows = [[ref[b, pl.ds(t0, group), ls] for (b, ls) in tiles] for ref in (kk_ref, kka_ref, w_ref, km_ref, r_ref)]
        for u in range(group):
            kk_r, kka_r, w_r, km_r, r_r = ([x[u:u + 1] for x in q] for q in rows)
            s_kk = seg_sums([s_scr[b, :, ls] * kk_r[i] for i, (b, ls) in enumerate(tiles)])
            s_new = []
            for i, (b, ls) in enumerate(tiles):
                v_col = jnp.take_along_axis(vt_ref[b, 0, :, ls], seg_base + (t0 + u), axis=1)
                s = s_scr[b, :, ls] * w_r[i] - s_kk[i] * kka_r[i] + v_col * km_r[i]
                s_scr[b, :, ls] = s
                s_new.append(s * r_r[i])
            o_col = seg_sums(s_new)
            for i, (b, ls) in enumerate(tiles):
                ot_ref[b, 0, :, ls] = jnp.where(lane_t == t0 + u, o_col[i], ot_ref[b, 0, :, ls])
        return carry

    if n_steps == group:
        step_group(0, 0)
    else:
        lax.fori_loop(0, n_steps // group, step_group, 0)

    @pl.when(c == pl.num_programs(0) - 1)
    def _():
        st_ref[...] = s_scr[...]


def _rwkv_step_kernel(s_ref, kk_ref, kka_ref, w_ref, km_ref, r_ref, v_ref, o_ref, st_ref):
    rnd = lambda x: x.astype(jnp.bfloat16).astype(jnp.float32)
    s = s_ref[...]
    s_kk = jnp.sum(rnd(s) * rnd(kk_ref[...]), axis=-1, keepdims=True)
    s = s * w_ref[...] - s_kk * kka_ref[...] + v_ref[...] * km_ref[...]
    o_ref[...] = jnp.sum(rnd(s) * rnd(r_ref[...]), axis=-1, keepdims=True)
    st_ref[...] = s


def rwkv_step(r, w, k, v, kk, a, state, groups_per_block=16):
    B, _, width = r.shape
    n_heads = width // A_HEAD
    n_groups = B * n_heads
    gb = _pick_tile(n_groups, groups_per_block, 1)
    row = lambda x: x.reshape(n_groups, 1, A_HEAD)
    row_spec = pl.BlockSpec((gb, 1, A_HEAD), lambda g: (g, 0, 0))
    col_spec = pl.BlockSpec((gb, A_HEAD, 1), lambda g: (g, 0, 0))
    st_spec = pl.BlockSpec((gb, A_HEAD, A_HEAD), lambda g: (g, 0, 0))
    o, st = pl.pallas_call(
        _rwkv_step_kernel,
        grid=(n_groups // gb,),
        in_specs=[st_spec] + [row_spec] * 5 + [col_spec],
        out_specs=[col_spec, st_spec],
        out_shape=[jax.ShapeDtypeStruct((n_groups, A_HEAD, 1), jnp.float32),
                   jax.ShapeDtypeStruct((n_groups, A_HEAD, A_HEAD), jnp.float32)],
        compiler_params=pltpu.CompilerParams(dimension_semantics=("parallel",), vmem_limit_bytes=VMEM_LIMIT_BYTES),
        name="rwkv_step",
    )(state.reshape(n_groups, A_HEAD, A_HEAD), row(kk), row(kk * a), row(w), row(k), row(r),
      v.reshape(n_groups, A_HEAD, 1))
    return o.reshape(B, 1, width), st.reshape(state.shape)


def rwkv_scan(r, w, k, v, kk, a, state):
    B, T, width = r.shape
    n_heads = width // A_HEAD
    if T == 1:
        return rwkv_step(r, w, k, v, kk, a, state)
    assert SCAN_CHUNK == A_HEAD and (T % SCAN_CHUNK == 0 or T < SCAN_CHUNK)
    n_chunks = pl.cdiv(T, SCAN_CHUNK)
    t_pad = n_chunks * SCAN_CHUNK
    pad = lambda x: jnp.pad(x, ((0, 0), (0, t_pad - T), (0, 0)))
    vt = jnp.transpose(pad(v).reshape(B, n_chunks, SCAN_CHUNK, n_heads, A_HEAD), (0, 1, 4, 3, 2))
    vt = vt.reshape(B, n_chunks, A_HEAD, width)
    st0 = jnp.transpose(state, (0, 2, 1, 3)).reshape(B, A_HEAD, width)
    row_spec = pl.BlockSpec((B, SCAN_CHUNK, width), lambda c: (0, c, 0))
    col_spec = pl.BlockSpec((B, 1, A_HEAD, width), lambda c: (0, c, 0, 0))
    st_spec = pl.BlockSpec((B, A_HEAD, width), lambda c: (0, 0, 0))
    ot, st = pl.pallas_call(
        functools.partial(_rwkv_scan_kernel, n_steps=min(T, SCAN_CHUNK)),
        grid=(n_chunks,),
        in_specs=[col_spec] + [row_spec] * 5 + [st_spec],
        out_specs=[col_spec, st_spec],
        out_shape=[jax.ShapeDtypeStruct((B, n_chunks, A_HEAD, width), jnp.float32),
                   jax.ShapeDtypeStruct((B, A_HEAD, width), jnp.float32)],
        scratch_shapes=[pltpu.VMEM((B, A_HEAD, width), jnp.float32)],
        compiler_params=pltpu.CompilerParams(
            dimension_semantics=("arbitrary",),
            vmem_limit_bytes=VMEM_LIMIT_BYTES),
        name="rwkv_scan",
    )(vt, pad(kk), pad(kk * a), pad(w), pad(k), pad(r), st0)
    o = jnp.transpose(ot.reshape(B, n_chunks, A_HEAD, n_heads, SCAN_CHUNK), (0, 1, 4, 3, 2)).reshape(B, t_pad, width)
    new_state = jnp.transpose(st.reshape(B, A_HEAD, n_heads, A_HEAD), (0, 2, 1, 3))
    return o[:, :T], new_state


def _split2(x):
    hi = x.astype(jnp.bfloat16)
    return hi, (x - hi.astype(jnp.float32)).astype(jnp.bfloat16)


def _dot_nt_split(a, b):
    a_hi, a_lo = _split2(a)
    b_hi, b_lo = _split2(b)
    return _dot_nt(jnp.concatenate([a_hi, a_hi, a_lo], axis=1), jnp.concatenate([b_hi, b_lo, b_hi], axis=1))


def _gla_segment_matrix(chunk):
    t = np.arange(chunk)[:, None]
    u = np.arange(chunk)[None, :]
    rows = []
    for lvl in range(chunk.bit_length() - 1):
        m = (chunk // 2) >> lvl
        blk = t // m
        right = np.logical_and(u >= blk * m, u <= t)
        left = np.logical_and(u > t, u < (blk + 1) * m)
        rows.append(np.where(blk % 2 == 1, right, left))
    rows.append(u <= t)
    rows.append(u > t)
    seg = np.concatenate(rows, axis=0).astype(np.float32)
    return np.concatenate([seg, seg, seg], axis=1)


def _dot_nt(a, b, **kw):
    return lax.dot_general(a, b, (((1,), (1,)), ((), ())), preferred_element_type=jnp.float32, **kw)


def _gla_kernel(seg_ref, q_ref, k_ref, v_ref, lf_ref, g_ref, gn_ref, s0_ref, y_ref, st_ref, s_scr):
    c = pl.program_id(1)
    chunk = q_ref.shape[1]
    n_levels = chunk.bit_length() - 1
    n_heads = s_scr.shape[0]
    bf = jnp.bfloat16

    @pl.when(c == 0)
    def _():
        s_scr[...] = s0_ref[0]

    row = lax.broadcasted_iota(jnp.int32, (chunk, chunk), 0)
    col = lax.broadcasted_iota(jnp.int32, (chunk, chunk), 1)
    row_w = lax.broadcasted_iota(jnp.int32, (chunk, C_EXPAND), 0)
    seg = seg_ref[...]
    gn = gn_ref[...]

    def head_group(hg, carry):
        hs = [hg * GLA_HEAD_GROUP + u for u in range(GLA_HEAD_GROUP)]
        lanes = [pl.ds(pl.multiple_of(h * C_EXPAND, C_EXPAND), C_EXPAND) for h in hs]
        qs = [q_ref[0, :, ls] for ls in lanes]
        ks = [k_ref[0, :, ls] for ls in lanes]
        es = []
        for ls in lanes:
            lf = lf_ref[0, :, ls]
            lf_hi = lf.astype(bf)
            lf_r = lf - lf_hi.astype(jnp.float32)
            lf_mid = lf_r.astype(bf)
            lf_lo = (lf_r - lf_mid.astype(jnp.float32)).astype(bf)
            es.append(jnp.exp(jnp.dot(seg, jnp.concatenate([lf_hi, lf_mid, lf_lo], axis=0),
                                      preferred_element_type=jnp.float32)))
        grams = []
        for q, k, e in zip(qs, ks, es):
            g_h = [_dot_nt_split(q, k)]
            for lvl in range(n_levels):
                sh = n_levels - 1 - lvl
                x = jnp.where(((row_w >> sh) & 1) == 1, q, k) * e[lvl * chunk:(lvl + 1) * chunk]
                g_h.append(_dot_nt_split(x, x))
            grams.append(g_h)
        outs = []
        for h, ls, q, k, e, g_h in zip(hs, lanes, qs, ks, es, grams):
            a = jnp.where(row == col, g_h[0], 0.0)
            for lvl in range(n_levels):
                sh = n_levels - 1 - lvl
                pair = (((row >> sh) & 1) == 1) & ((col >> sh) == (row >> sh) - 1)
                a = a + jnp.where(pair, g_h[lvl + 1], 0.0)
            e_b = e[n_levels * chunk:(n_levels + 1) * chunk]
            e_r = e[(n_levels + 1) * chunk:]
            v = v_ref[0, :, ls].astype(bf)
            st = s_scr[h]
            o = jnp.dot(a.astype(bf), v, preferred_element_type=jnp.float32)
            outs.append(o + _dot_nt((q * e_b).astype(bf), st.astype(bf)))
            s_scr[h] = st * e_b[chunk - 1:chunk, :] + lax.dot_general(
                v, (k * e_r).astype(bf), (((0,), (0,)), ((), ())), preferred_element_type=jnp.float32)
        for ls, o in zip(lanes, outs):
            o = o * lax.rsqrt(jnp.mean(o * o, axis=1, keepdims=True) + RMS_EPS) * gn
            gate = g_ref[0, :, ls]
            y_ref[0, :, ls] = o * (gate * jax.nn.sigmoid(gate))
        return carry

    lax.fori_loop(0, n_heads // GLA_HEAD_GROUP, head_group, 0)

    @pl.when(c == pl.num_programs(1) - 1)
    def _():
        st_ref[0] = s_scr[...]


def gla_gated(q, k, v, log_f, g, gn_g, state):
    B, T, width = q.shape
    n_heads = width // C_EXPAND
    assert v.shape[2] == width, "kernel assumes dk == dv == C_EXPAND"
    n_chunks = pl.cdiv(T, GLA_CHUNK)
    t_pad = n_chunks * GLA_CHUNK
    pad = lambda x: jnp.pad(x, ((0, 0), (0, t_pad - T), (0, 0)))
    seg = jnp.asarray(_gla_segment_matrix(GLA_CHUNK), dtype=jnp.bfloat16)
    st0 = jnp.swapaxes(state, 2, 3)
    tok_spec = pl.BlockSpec((1, GLA_CHUNK, width), lambda b, c: (b, c, 0))
    st_spec = pl.BlockSpec((1, n_heads, C_EXPAND, C_EXPAND), lambda b, c: (b, 0, 0, 0))
    y, st = pl.pallas_call(
        _gla_kernel,
        grid=(B, n_chunks),
        in_specs=[pl.BlockSpec(seg.shape, lambda b, c: (0, 0))] + [tok_spec] * 5
                 + [pl.BlockSpec((1, C_EXPAND), lambda b, c: (0, 0)), st_spec],
        out_specs=[tok_spec, st_spec],
        out_shape=[jax.ShapeDtypeStruct((B, t_pad, width), jnp.float32),
                   jax.ShapeDtypeStruct(st0.shape, jnp.float32)],
        scratch_shapes=[pltpu.VMEM((n_heads, C_EXPAND, C_EXPAND), jnp.float32)],
        compiler_params=pltpu.CompilerParams(
            dimension_semantics=("parallel", "arbitrary"),
            vmem_limit_bytes=VMEM_LIMIT_BYTES),
        name="gla_gated",
    )(seg, pad(q), pad(k), pad(v), pad(log_f), pad(g), gn_g.reshape(1, C_EXPAND), st0)
    return y[:, :T], jnp.swapaxes(st, 2, 3)


def _moe_up_kernel(te_ref, x_ref, wg_ref, wu_ref, o_ref):
    del te_ref
    xb = x_ref[...].astype(jnp.bfloat16)
    g = jnp.dot(xb, wg_ref[0, 0].astype(jnp.bfloat16), preferred_element_type=jnp.float32)
    u = jnp.dot(xb, wu_ref[0, 0].astype(jnp.bfloat16), preferred_element_type=jnp.float32)
    o_ref[...] = (g * jax.nn.sigmoid(g) * u).astype(o_ref.dtype)


def _moe_down_kernel(te_ref, h_ref, wd_ref, o_ref):
    del te_ref
    o_ref[...] = jnp.dot(h_ref[...].astype(jnp.bfloat16), wd_ref[0, 0].astype(jnp.bfloat16),
                         preferred_element_type=jnp.float32)


def moe_experts(tile_expert, xs, wg, wu, wd, layer, tm, tf=704, tn=1024):
    rows, d_model = xs.shape
    d_ff = wg.shape[3]
    tf = _pick_tile(d_ff, tf, 128)
    tn = _pick_tile(d_model, tn, 128)
    params = pltpu.CompilerParams(dimension_semantics=("parallel", "arbitrary"), vmem_limit_bytes=VMEM_LIMIT_BYTES)
    h = pl.pallas_call(
        _moe_up_kernel,
        grid_spec=pltpu.PrefetchScalarGridSpec(
            num_scalar_prefetch=1, grid=(d_ff // tf, rows // tm),
            in_specs=[pl.BlockSpec((tm, d_model), lambda j, i, te: (i, 0)),
                      pl.BlockSpec((1, 1, d_model, tf), lambda j, i, te: (layer, te[i], 0, j)),
                      pl.BlockSpec((1, 1, d_model, tf), lambda j, i, te: (layer, te[i], 0, j))],
            out_specs=pl.BlockSpec((tm, tf), lambda j, i, te: (i, j))),
        out_shape=jax.ShapeDtypeStruct((rows, d_ff), jnp.bfloat16 if rows >= BF16_ROWS_MIN else jnp.float32),
        compiler_params=params, name="moe_up",
    )(tile_expert, xs, wg, wu)
    return pl.pallas_call(
        _moe_down_kernel,
        grid_spec=pltpu.PrefetchScalarGridSpec(
            num_scalar_prefetch=1, grid=(d_model // tn, rows // tm),
            in_specs=[pl.BlockSpec((tm, d_ff), lambda j, i, te: (i, 0)),
                      pl.BlockSpec((1, 1, d_ff, tn), lambda j, i, te: (layer, te[i], 0, j))],
            out_specs=pl.BlockSpec((tm, tn), lambda j, i, te: (i, j))),
        out_shape=jax.ShapeDtypeStruct((rows, d_model), jnp.float32),
        compiler_params=params, name="moe_down",
    )(tile_expert, h, wd)


def moe_swiglu(x, w_router, wg, wu, wd, layer):
    lead, d_model = x.shape[:-1], x.shape[-1]
    x = x.reshape(-1, d_model)
    n_tok = x.shape[0]
    n_pairs = n_tok * TOP_EXPERTS
    tm = min(512, n_pairs)
    assert n_pairs % tm == 0
    probs = jax.nn.softmax(matmul(x, (w_router, layer)), axis=-1)
    top_p, top_i = lax.top_k(probs, TOP_EXPERTS)
    top_p = top_p / jnp.sum(top_p, axis=-1, keepdims=True)
    pair_expert = top_i.reshape(-1)
    order = jnp.argsort(pair_expert, stable=True)
    counts = jnp.sum(jax.nn.one_hot(pair_expert, N_EXPERTS, dtype=jnp.int32), axis=0)
    padded = (counts + tm - 1) // tm * tm
    padded_end = jnp.cumsum(padded)
    sorted_expert = pair_expert[order]
    rank = jnp.arange(n_pairs, dtype=jnp.int32) - (jnp.cumsum(counts) - counts)[sorted_expert]
    dest = (padded_end - padded)[sorted_expert] + rank
    rows = n_pairs + N_EXPERTS * tm
    row_token = jnp.zeros((rows,), jnp.int32).at[dest].set((order // TOP_EXPERTS).astype(jnp.int32))
    pair_row = jnp.zeros((n_pairs,), jnp.int32).at[order].set(dest.astype(jnp.int32))
    tile_start = jnp.arange(rows // tm, dtype=jnp.int32) * tm
    tile_expert = jnp.minimum(jnp.searchsorted(padded_end, tile_start, side='right'), N_EXPERTS - 1).astype(jnp.int32)
    y_rows = moe_experts(tile_expert, x[row_token], wg, wu, wd, layer, tm)
    pair_row = pair_row.reshape(n_tok, TOP_EXPERTS)
    y = sum(top_p[:, s:s + 1] * y_rows[pair_row[:, s]] for s in range(TOP_EXPERTS))
    return y.reshape(lead + (d_model,))


def split_cols(a, sizes):
    return jnp.split(a, np.cumsum(sizes)[:-1].tolist(), axis=-1)


def gather_rows(a, idx):
    return jax.vmap(lambda a_b, i_b: a_b[i_b])(a, idx)


def layer_norm(x, g, b):
    mu = jnp.mean(x, axis=-1, keepdims=True)
    var = jnp.mean(jnp.square(x - mu), axis=-1, keepdims=True)
    return (x - mu) * lax.rsqrt(var + LN_EPS) * g + b


def t5_bucket(dist):
    dist = jnp.maximum(dist, 0)
    max_exact = T5_BUCKETS // 2
    d = jnp.maximum(dist, 1).astype(jnp.float32)
    large = max_exact + (jnp.log(d / max_exact) / math.log(T5_MAX_DIST / max_exact) * (T5_BUCKETS - max_exact)).astype(jnp.int32)
    large = jnp.minimum(large, T5_BUCKETS - 1)
    return jnp.where(dist < max_exact, dist, large)


def dsa_attend(q, qi, wi, q_pos, k_idx, gather_kv, rel_bias, topk):
    B, T = q.shape[:2]
    L = k_idx.shape[1]
    blk = min(Q_BLOCK, T)
    nb = -(-T // blk)
    pad = nb * blk - T

    def to_blocks(a):
        a = jnp.pad(a, [(0, 0), (0, pad)] + [(0, 0)] * (a.ndim - 2))
        return jnp.moveaxis(a.reshape((B, nb, blk) + a.shape[2:]), 1, 0)

    pos_b = jnp.pad(q_pos, (0, pad)).reshape(nb, blk)
    k_pos = jnp.arange(L, dtype=jnp.int32)

    def one_block(args):
        qb, qib, wib, pb = args
        s = jnp.einsum('bthd,bsd->bths', qib, k_idx) * IDX_DIM ** -0.5
        score = jnp.einsum('bth,bths->bts', wib, jax.nn.relu(s))
        score = jnp.where(k_pos[None, None, :] <= pb[None, :, None], score, -jnp.inf)
        _, sel = lax.top_k(score, topk)
        valid = sel <= pb[None, :, None]
        k_sel, v_sel = gather_kv(sel)
        logits = jnp.einsum('bthd,btkhd->bhtk', qb, k_sel) * B_HEAD ** -0.5
        bias = rel_bias[t5_bucket(pb[None, :, None] - sel)]
        logits = logits + jnp.transpose(bias, (0, 3, 1, 2))
        logits = jnp.where(valid[:, None], logits, -jnp.inf)
        p = jax.nn.softmax(logits, axis=-1)
        return jnp.einsum('bhtk,btkhd->bthd', p, v_sel)

    out = lax.map(one_block, (to_blocks(q), to_blocks(qi), to_blocks(wi), pos_b))
    return jnp.moveaxis(out, 0, 1).reshape((B, nb * blk) + q.shape[2:])[:, :T]


def attend_paged(q, k, v, qi, ki, wi, cache_k, cache_v, cache_kidx, layer, page_table, rel_bias):
    B, T = q.shape[:2]
    past = page_table.shape[1] * PAGE_SIZE
    pages = page_table + layer * cache_k.shape[1]
    kidx_past = cache_kidx.reshape((-1,) + cache_kidx.shape[2:])[pages].reshape(B, past, IDX_DIM)
    kidx_all = jnp.concatenate([kidx_past, ki], axis=1)
    flat_k = cache_k.reshape((-1,) + cache_k.shape[3:])
    flat_v = cache_v.reshape((-1,) + cache_v.shape[3:])

    def gather(sel):
        ps = jnp.minimum(sel, past - 1)
        row = gather_rows(pages, ps // PAGE_SIZE) * PAGE_SIZE + ps % PAGE_SIZE
        is_new = (sel >= past)[..., None, None]
        ns = jnp.clip(sel - past, 0, T - 1)
        k_sel = jnp.where(is_new, gather_rows(k, ns), flat_k[row])
        v_sel = jnp.where(is_new, gather_rows(v, ns), flat_v[row])
        return k_sel, v_sel

    q_pos = past + jnp.arange(T, dtype=jnp.int32)
    return dsa_attend(q, qi, wi, q_pos, kidx_all, gather, rel_bias, min(TOPK_MAX, (past + T) // 4))


def rwkv7_mix(p_a, shift_state, wkv_state, mu, w0, w2, a0, a2, g2, kk_s, ka_s, rk_s, lnx_g, lnx_b):
    B, T, _ = p_a.shape
    a_width = w0.shape[0]
    a_heads = a_width // A_HEAD
    prev = jnp.concatenate([shift_state[:, None, :], p_a[:, :-1]], axis=1)
    xm = p_a + (prev - p_a) * mu
    r, k, v, wd, ad, gd = split_cols(xm, [a_width, a_width, a_width, W_LORA, AA_LORA, G_LORA])
    w_pre = w0 + mm(jnp.tanh(wd), w2)
    decay = jnp.exp(-jnp.exp(-jax.nn.softplus(-w_pre) - 0.5))
    a = jax.nn.sigmoid(a0 + mm(ad, a2))
    g = mm(jax.nn.sigmoid(gd), g2)
    heads = lambda t: t.reshape(B, T, a_heads, A_HEAD)
    kk = heads(k * kk_s)
    kk = kk / jnp.maximum(jnp.sqrt(jnp.sum(jnp.square(kk), axis=-1, keepdims=True)), 1e-12)
    k_mod = heads(k * (1.0 + (a - 1.0) * ka_s))
    r_h, v_h = heads(r), heads(v)
    o, S = rwkv_scan(r, decay, k_mod.reshape(B, T, a_width), v, kk.reshape(B, T, a_width), a, wkv_state)
    o = heads(o)
    mu_o = jnp.mean(o, axis=-1, keepdims=True)
    var_o = jnp.mean(jnp.square(o - mu_o), axis=-1, keepdims=True)
    o = ((o - mu_o) * lax.rsqrt(var_o + RWKV_GN_EPS)).reshape(B, T, a_width) * lnx_g + lnx_b
    bonus = jnp.sum(r_h * k_mod * rk_s.reshape(a_heads, A_HEAD), axis=-1, keepdims=True) * v_h
    y = (o + bonus.reshape(B, T, a_width)) * g
    return y, p_a[:, -1], S


def ab_mixer(x, shift_state, wkv_state, attend, w_in, rw, w_out):
    B, T, d_model = x.shape
    a_width = rw[1].shape[0]
    a_proj = 3 * a_width + W_LORA + AA_LORA + G_LORA
    b_width = d_model - a_width
    b_heads = b_width // B_HEAD
    proj = mm(x, w_in)
    y_a, new_shift, new_wkv = rwkv7_mix(proj[..., :a_proj], shift_state, wkv_state, *rw)
    q, k, v, qi, ki, wi = split_cols(proj[..., a_proj:], [b_width, b_width, b_width, IDX_HEADS * IDX_DIM, IDX_DIM, IDX_HEADS])
    y_b = attend(q, k, v, qi, ki, wi)
    y = mm(jnp.concatenate([y_a, y_b], axis=-1), w_out)
    heads = lambda t: t.reshape(B, T, b_heads, B_HEAD)
    return y, new_shift, new_wkv, heads(k), heads(v), ki


def attend_sample(q, k, v, qi, ki, wi, **kw):
    B, T, b_width = q.shape
    heads = lambda t: t.reshape(B, T, b_width // B_HEAD, B_HEAD)
    y = attend_paged(heads(q), heads(k), heads(v), qi.reshape(B, T, IDX_HEADS, IDX_DIM), ki, wi * IDX_HEADS ** -0.5, **kw)
    return y.reshape(B, T, b_width)


def hgrn2_mixer(x, state, w_in, lb, gn_g, w_out):
    B, T, c_width = x.shape
    q, f_raw, i, g = split_cols(mm(x, w_in), [c_width, c_width, c_width, c_width])
    log_f = jnp.logaddexp(jnp.log(lb), jnp.log1p(-lb) + jax.nn.log_sigmoid(f_raw))
    k_in = (1.0 - lb) * jax.nn.sigmoid(-f_raw)
    y, new_state = gla_gated(jax.nn.silu(q), k_in, i, log_f, g, gn_g, state)
    return mm(y, w_out), new_state


def kernel(x_prompt, x_sample, cache_k, cache_v, cache_kidx, state_wkv, state_shift, state_hgrn, page_table, w_in_ab, rwkv_mu, rwkv_w0, rwkv_w2, rwkv_a0, rwkv_a2, rwkv_g2, rwkv_kk, rwkv_ka, rwkv_rk, rwkv_lnx_g, rwkv_lnx_b, w_out_ab, rel_bias, ffn_w_gate, ffn_w_up, ffn_w_down, w_in_c, hgrn_lb, hgrn_gnorm, w_out_c, router, moe_w_gate, moe_w_up, moe_w_down, ln1_g, ln1_b, ln2_g, ln2_b):
    depth = ln1_g.shape[0]
    alpha = (2 * depth) ** 0.25
    a_width = rwkv_w0.shape[1]
    a_heads = a_width // A_HEAD
    a_proj = state_shift.shape[-1]
    c_heads, c_dk, c_dv = state_hgrn.shape[2:]
    lb_all = jnp.cumsum(jax.nn.softmax(hgrn_lb, axis=0), axis=0)
    lb_all = lb_all - lb_all[0:1]
    attend_p = functools.partial(dsa_prompt, rel_bias=rel_bias)
    xp, xs = x_prompt, x_sample
    kp_l, vp_l, kip_l, wkvp_l, shp_l, hgp_l = [], [], [], [], [], []
    ks_l, vs_l, kis_l, wkvs_l, shs_l, hgs_l = [], [], [], [], [], []
    for l in range(depth):
        j = l // 2
        if l % 2 == 0:
            rw = (rwkv_mu[j], rwkv_w0[j], rwkv_w2[j], rwkv_a0[j], rwkv_a2[j], rwkv_g2[j],
                  rwkv_kk[j], rwkv_ka[j], rwkv_rk[j], rwkv_lnx_g[j], rwkv_lnx_b[j])
            attend_s = functools.partial(attend_sample, cache_k=cache_k, cache_v=cache_v, cache_kidx=cache_kidx,
                                         layer=j, page_table=page_table, rel_bias=rel_bias)
            shift0 = jnp.zeros((xp.shape[0], a_proj), xp.dtype)
            wkv0 = jnp.zeros((xp.shape[0], a_heads, A_HEAD, A_HEAD), jnp.float32)
            yp, shp, wkvp, kp, vp, kip = ab_mixer(xp, shift0, wkv0, attend_p, (w_in_ab, j), rw, (w_out_ab, j))
            ys, shs, wkvs, kss, vss, kis = ab_mixer(xs, state_shift[j], state_wkv[j], attend_s, (w_in_ab, j), rw, (w_out_ab, j))
            kp_l.append(kp); vp_l.append(vp); kip_l.append(kip); wkvp_l.append(wkvp); shp_l.append(shp)
            ks_l.append(kss); vs_l.append(vss); kis_l.append(kis); wkvs_l.append(wkvs); shs_l.append(shs)
        else:
            hg0 = jnp.zeros((xp.shape[0], c_heads, c_dk, c_dv), jnp.float32)
            yp, hgp = hgrn2_mixer(xp, hg0, (w_in_c, j), lb_all[j], hgrn_gnorm[j], (w_out_c, j))
            ys, hgs = hgrn2_mixer(xs, state_hgrn[j], (w_in_c, j), lb_all[j], hgrn_gnorm[j], (w_out_c, j))
            hgp_l.append(hgp); hgs_l.append(hgs)
        xp = layer_norm(alpha * xp + yp, ln1_g[l], ln1_b[l])
        xs = layer_norm(alpha * xs + ys, ln1_g[l], ln1_b[l])
        if l % 2 == 0:
            fp = swiglu(xp, (ffn_w_gate, j), (ffn_w_up, j), (ffn_w_down, j))
            fs = swiglu(xs, (ffn_w_gate, j), (ffn_w_up, j), (ffn_w_down, j))
        else:
            fp = moe_swiglu(xp, router, moe_w_gate, moe_w_up, moe_w_down, j)
            fs = moe_swiglu(xs, router, moe_w_gate, moe_w_up, moe_w_down, j)
        xp = layer_norm(alpha * xp + fp, ln2_g[l], ln2_b[l])
        xs = layer_norm(alpha * xs + fs, ln2_g[l], ln2_b[l])
    return (xp, xs, jnp.stack(kp_l), jnp.stack(vp_l), jnp.stack(kip_l), jnp.stack(wkvp_l), jnp.stack(shp_l),
            jnp.stack(hgp_l), jnp.stack(ks_l), jnp.stack(vs_l), jnp.stack(kis_l), jnp.stack(wkvs_l),
            jnp.stack(shs_l), jnp.stack(hgs_l))
```

```python
import functools
import math

import jax
import jax.numpy as jnp
import numpy as np
from jax import lax
from jax.experimental import pallas as pl
from jax.experimental.pallas import tpu as pltpu

A_HEAD = 64
W_LORA = 64
AA_LORA = 64
G_LORA = 128
RWKV_GN_EPS = 64e-5
B_HEAD = 128
IDX_HEADS = 16
IDX_DIM = 64
TOPK_MAX = 256
Q_BLOCK = 64
T5_BUCKETS = 32
T5_MAX_DIST = 128
C_EXPAND = 128
GLA_CHUNK = 64
GLA_HEAD_GROUP = 4
RMS_EPS = 1e-5
N_EXPERTS = 8
TOP_EXPERTS = 2
PAGE_SIZE = 128
LN_EPS = 1e-5

VMEM_LIMIT_BYTES = 56 * 1024 * 1024
BF16_ROWS_MIN = 256


def _pick_tile(n, target, align):
    if n <= target:
        return n
    t = (target // align) * align
    while t >= align:
        if n % t == 0:
            return t
        t -= align
    return n


def _layer(w):
    return w if isinstance(w, tuple) else (w[None], 0)


def _matmul_kernel(x_ref, w_ref, o_ref):
    @pl.when(pl.program_id(2) == 0)
    def _():
        o_ref[...] = jnp.zeros_like(o_ref)

    o_ref[...] += jnp.dot(x_ref[...].astype(jnp.bfloat16), w_ref[0].astype(jnp.bfloat16),
                          preferred_element_type=jnp.float32)


def matmul(x, w, tm=1024, tn=1024, tk=2048):
    m, k = x.shape
    w, layer = _layer(w)
    k2, n = w.shape[1:]
    assert k == k2
    tm = _pick_tile(m, tm, 8)
    tk = _pick_tile(k, tk, 128)
    tn = min(tn, n) if n % 128 == 0 or n < 128 else tn
    grid = (pl.cdiv(n, tn), pl.cdiv(m, tm), k // tk)
    return pl.pallas_call(
        _matmul_kernel,
        grid=grid,
        in_specs=[pl.BlockSpec((tm, tk), lambda j, i, kk: (i, kk)),
                  pl.BlockSpec((1, tk, tn), lambda j, i, kk: (layer, kk, j))],
        out_specs=pl.BlockSpec((tm, tn), lambda j, i, kk: (i, j)),
        out_shape=jax.ShapeDtypeStruct((m, n), jnp.float32),
        compiler_params=pltpu.CompilerParams(
            dimension_semantics=("parallel", "parallel", "arbitrary"),
            vmem_limit_bytes=VMEM_LIMIT_BYTES),
        name="matmul",
    )(x, w)


def _swiglu_up_kernel(x_ref, wg_ref, wu_ref, o_ref):
    xb = x_ref[...].astype(jnp.bfloat16)
    g = jnp.dot(xb, wg_ref[0].astype(jnp.bfloat16), preferred_element_type=jnp.float32)
    u = jnp.dot(xb, wu_ref[0].astype(jnp.bfloat16), preferred_element_type=jnp.float32)
    o_ref[...] = (g * jax.nn.sigmoid(g) * u).astype(o_ref.dtype)


def swiglu_up(x, wg, wu, tm=1024, tn=512):
    m, k = x.shape
    (wg, layer), (wu, _) = _layer(wg), _layer(wu)
    n = wg.shape[2]
    tm = _pick_tile(m, tm, 8)
    tn = _pick_tile(n, tn, 128)
    return pl.pallas_call(
        _swiglu_up_kernel,
        grid=(n // tn, m // tm),
        in_specs=[pl.BlockSpec((tm, k), lambda j, i: (i, 0)),
                  pl.BlockSpec((1, k, tn), lambda j, i: (layer, 0, j)),
                  pl.BlockSpec((1, k, tn), lambda j, i: (layer, 0, j))],
        out_specs=pl.BlockSpec((tm, tn), lambda j, i: (i, j)),
        out_shape=jax.ShapeDtypeStruct((m, n), x.dtype),
        compiler_params=pltpu.CompilerParams(
            dimension_semantics=("parallel", "parallel"),
            vmem_limit_bytes=VMEM_LIMIT_BYTES),
        name="swiglu_up",
    )(x, wg, wu)


def _mxu_operand(x):
    x = x.reshape(-1, x.shape[-1])
    return x.astype(jnp.bfloat16) if x.shape[0] >= BF16_ROWS_MIN else x


def mm(x, w):
    return matmul(_mxu_operand(x), w).reshape(x.shape[:-1] + (_layer(w)[0].shape[2],))


def swiglu(x, wg, wu, wd):
    h = swiglu_up(_mxu_operand(x), wg, wu)
    return matmul(h, wd).reshape(x.shape[:-1] + (_layer(wd)[0].shape[2],))


INT_MIN = -2 ** 31
NEG_BIG = -1e30


def _sortable_key(x):
    bits = pltpu.bitcast(jnp.where(x == 0.0, 0.0, x), jnp.int32)
    return bits ^ ((bits >> 31) & 0x7FFFFFFF)


def _dsa_prompt_kernel(qt_ref, kt_ref, qi_ref, ki_ref, wit_ref, q_ref, k_ref, v_ref, bias_ref, o_ref,
                       keys_ref, tau_ref, qib_ref, qb_ref, m_ref, l_ref, acc_ref, *, topk, n_idx_bits):
    i = qt_ref[pl.program_id(1)]
    j = kt_ref[pl.program_id(1)]
    tq = q_ref.shape[1]
    n_heads = q_ref.shape[2] // B_HEAD
    kf = float(topk)

    def key_sum(x):
        part = x[:SUBLANES]
        for u in range(1, tq // SUBLANES):
            part = part + x[u * SUBLANES:(u + 1) * SUBLANES]
        return part

    @pl.when(j == 0)
    def _select():
        qib_ref[...] = (qi_ref[0] * IDX_DIM ** -0.5).astype(jnp.bfloat16)
        qb_ref[...] = q_ref[0].astype(jnp.bfloat16)
        m_ref[...] = jnp.full_like(m_ref, NEG_BIG)
        l_ref[...] = jnp.zeros_like(l_ref)
        acc_ref[...] = jnp.zeros_like(acc_ref)
        wb = (wit_ref[0] * IDX_HEADS ** -0.5).astype(jnp.bfloat16).astype(jnp.float32)
        key_pos = lax.broadcasted_iota(jnp.int32, (tq, tq), 0)
        qry_pos = i * tq + lax.broadcasted_iota(jnp.int32, (tq, tq), 1)

        def score_chunk(c, carry):
            kc = ki_ref[0, pl.ds(pl.multiple_of(c * tq, tq), tq), :]
            score = jnp.zeros((tq, tq), jnp.float32)
            for h in range(IDX_HEADS):
                s = _dot_nt(kc, qib_ref[:, h * IDX_DIM:(h + 1) * IDX_DIM])
                r = jnp.maximum(s, 0.0).astype(jnp.bfloat16).astype(jnp.float32)
                score = score + wb[h:h + 1, :] * r
            keys_ref[c] = jnp.where(c * tq + key_pos <= qry_pos, _sortable_key(score), INT_MIN)
            return carry

        lax.fori_loop(0, i + 1, score_chunk, 0)

        def count(pred):
            def body(c, acc):
                return acc + key_sum(jnp.where(pred(keys_ref[c], c), 1.0, 0.0))
            acc = lax.fori_loop(0, i + 1, body, jnp.zeros((SUBLANES, tq), jnp.float32))
            return jnp.sum(acc, axis=0, keepdims=True)

        tau = jnp.where(count(lambda kc, c: kc >= 0) >= kf, 0, INT_MIN).astype(jnp.int32)

        def bit_step(n, tau):
            cand = tau | jnp.left_shift(jnp.int32(1), 30 - n)
            return jnp.where(count(lambda kc, c: kc >= cand) >= kf, cand, tau)

        tau = lax.fori_loop(0, 31, bit_step, tau)
        tau = jnp.maximum(tau, INT_MIN + 1)
        tau_ref[...] = tau
        n_ge = count(lambda kc, c: kc >= tau)

        @pl.when(jnp.max(n_ge) > kf)
        def _ties():
            need = kf - count(lambda kc, c: kc > tau)

            def idx_step(n, ans):
                cand = ans + jnp.left_shift(jnp.int32(1), n_idx_bits - 1 - n)
                f = count(lambda kc, c: (kc == tau) & (c * tq + key_pos < cand))
                return jnp.where(f < need, cand, ans)

            last = lax.fori_loop(0, n_idx_bits, idx_step, jnp.zeros((1, tq), jnp.int32))

            def demote(c, carry):
                kc = keys_ref[c]
                keys_ref[c] = jnp.where((kc == tau) & (c * tq + key_pos > last), INT_MIN, kc)
                return carry

            lax.fori_loop(0, i + 1, demote, 0)

    sel = keys_ref[j] >= tau_ref[...]
    heads = [slice(h * B_HEAD, (h + 1) * B_HEAD) for h in range(n_heads)]
    raw = [_dot_nt(k_ref[0, :, hs].astype(jnp.bfloat16), qb_ref[:, hs]) for hs in heads]
    probs, alphas = [], []
    for h in range(n_heads):
        logits = jnp.where(sel, raw[h] * B_HEAD ** -0.5 + bias_ref[0, h], NEG_BIG)
        m_old = m_ref[h]
        m_new = jnp.maximum(m_old, jnp.max(logits, axis=0, keepdims=True))
        alpha = jnp.exp(m_old - m_new)
        p = jnp.exp(logits - m_new)
        l_ref[h] = alpha * l_ref[h] + jnp.sum(p, axis=0, keepdims=True)
        m_ref[h] = m_new
        probs.append(p.astype(jnp.bfloat16))
        alphas.append(alpha)
    for h, hs in enumerate(heads):
        acc_ref[h] = alphas[h] * acc_ref[h] + lax.dot_general(
            v_ref[0, :, hs].astype(jnp.bfloat16), probs[h], (((0,), (0,)), ((), ())),
            preferred_element_type=jnp.float32)

    @pl.when(j == i)
    def _finish():
        for h in range(n_heads):
            o_ref[0, :, h * B_HEAD:(h + 1) * B_HEAD] = jnp.transpose(acc_ref[h] / l_ref[h])


def dsa_prompt(q, k, v, qi, ki, wi, rel_bias, tq=256):
    B, T, b_width = q.shape
    n_heads = b_width // B_HEAD
    tq = min(tq, T)
    assert T % tq == 0 and tq % 128 == 0 and tq + 1 >= T5_MAX_DIST
    nq = T // tq
    topk = min(TOPK_MAX, T // 4)
    r = jnp.arange(tq, dtype=jnp.int32)
    dist = jnp.arange(3, dtype=jnp.int32)[:, None, None] * tq + r[None, None, :] - r[None, :, None]
    bias3 = jnp.einsum('dkqc,ch->dhkq', jax.nn.one_hot(t5_bucket(dist), T5_BUCKETS, dtype=jnp.float32), rel_bias,
                       precision=lax.Precision.HIGHEST)
    kern = functools.partial(_dsa_prompt_kernel, topk=topk, n_idx_bits=max(1, (T - 1).bit_length()))
    pairs = np.array([(i, j) for i in range(nq) for j in range(i + 1)], dtype=np.int32)
    q_tile, k_tile = jnp.asarray(pairs[:, 0]), jnp.asarray(pairs[:, 1])
    return pl.pallas_call(
        kern,
        grid_spec=pltpu.PrefetchScalarGridSpec(
            num_scalar_prefetch=2, grid=(B, len(pairs)),
            in_specs=[pl.BlockSpec((1, tq, qi.shape[2]), lambda b, p, qt, kt: (b, qt[p], 0)),
                      pl.BlockSpec((1, T, IDX_DIM), lambda b, p, qt, kt: (b, 0, 0)),
                      pl.BlockSpec((1, IDX_HEADS, tq), lambda b, p, qt, kt: (b, 0, qt[p])),
                      pl.BlockSpec((1, tq, b_width), lambda b, p, qt, kt: (b, qt[p], 0)),
                      pl.BlockSpec((1, tq, b_width), lambda b, p, qt, kt: (b, kt[p], 0)),
                      pl.BlockSpec((1, tq, b_width), lambda b, p, qt, kt: (b, kt[p], 0)),
                      pl.BlockSpec((1, n_heads, tq, tq),
                                   lambda b, p, qt, kt: (jnp.minimum(qt[p] - kt[p], 2), 0, 0, 0))],
            out_specs=pl.BlockSpec((1, tq, b_width), lambda b, p, qt, kt: (b, qt[p], 0)),
            scratch_shapes=[pltpu.VMEM((nq, tq, tq), jnp.int32),
                            pltpu.VMEM((1, tq), jnp.int32),
                            pltpu.VMEM((tq, qi.shape[2]), jnp.bfloat16),
                            pltpu.VMEM((tq, b_width), jnp.bfloat16),
                            pltpu.VMEM((n_heads, 1, tq), jnp.float32),
                            pltpu.VMEM((n_heads, 1, tq), jnp.float32),
                            pltpu.VMEM((n_heads, B_HEAD, tq), jnp.float32)]),
        out_shape=jax.ShapeDtypeStruct((B, T, b_width), jnp.float32),
        compiler_params=pltpu.CompilerParams(
            dimension_semantics=("parallel", "arbitrary"),
            vmem_limit_bytes=VMEM_LIMIT_BYTES),
        name="dsa_prompt",
    )(q_tile, k_tile, qi, ki.astype(jnp.bfloat16), jnp.swapaxes(wi, 1, 2), q, k, v, bias3)


SCAN_CHUNK = A_HEAD
LANES = 128
SUBLANES = 8


def _rwkv_scan_kernel(vt_ref, kk_ref, kka_ref, w_ref, km_ref, r_ref, s0_ref, ot_ref, st_ref, s_scr, *, n_steps):
    c = pl.program_id(0)
    n_batch = vt_ref.shape[0]
    n_tiles = vt_ref.shape[3] // LANES
    bf = jnp.bfloat16

    @pl.when(c == 0)
    def _():
        s_scr[...] = s0_ref[...]

    if n_steps < SCAN_CHUNK:
        ot_ref[...] = jnp.zeros_like(ot_ref)
    lane = lax.broadcasted_iota(jnp.int32, (A_HEAD, LANES), 1)
    seg_base = lane & (LANES - A_HEAD)
    lane_t = lane & (A_HEAD - 1)
    seg_ones = ((lax.broadcasted_iota(jnp.int32, (LANES, LANES), 0) // A_HEAD)
                == (lax.broadcasted_iota(jnp.int32, (LANES, LANES), 1) // A_HEAD)).astype(bf)

    group = min(n_steps, SUBLANES)
    tiles = [(b, slice(p * LANES, (p + 1) * LANES)) for b in range(n_batch) for p in range(n_tiles)]

    def seg_sums(xs):
        lhs = jnp.concatenate([x.astype(bf) for x in xs], axis=0)
        out = jnp.dot(lhs, seg_ones, preferred_element_type=jnp.float32)
        return [out[i * A_HEAD:(i + 1) * A_HEAD] for i in range(len(xs))]

    def step_group(g, carry):
        t0 = g * group if isinstance(g, int) else pl.multiple_of(g * group, group)
        rows = [[ref[b, pl.ds(t0, group), ls] for (b, ls) in tiles] for ref in (kk_ref, kka_ref, w_ref, km_ref, r_ref)]
        for u in range(group):
            kk_r, kka_r, w_r, km_r, r_r = ([x[u:u + 1] for x in q] for q in rows)
            s_kk = seg_sums([s_scr[b, :, ls] * kk_r[i] for i, (b, ls) in enumerate(tiles)])
            s_new = []
            for i, (b, ls) in enumerate(tiles):
                v_col = jnp.take_along_axis(vt_ref[b, 0, :, ls], seg_base + (t0 + u), axis=1)
                s = s_scr[b, :, ls] * w_r[i] - s_kk[i] * kka_r[i] + v_col * km_r[i]
                s_scr[b, :, ls] = s
                s_new.append(s * r_r[i])
            o_col = seg_sums(s_new)
            for i, (b, ls) in enumerate(tiles):
                ot_ref[b, 0, :, ls] = jnp.where(lane_t == t0 + u, o_col[i], ot_ref[b, 0, :, ls])
        return carry

    if n_steps == group:
        step_group(0, 0)
    else:
        lax.fori_loop(0, n_steps // group, step_group, 0)

    @pl.when(c == pl.num_programs(0) - 1)
    def _():
        st_ref[...] = s_scr[...]


def _rwkv_step_kernel(s_ref, kk_ref, kka_ref, w_ref, km_ref, r_ref, v_ref, o_ref, st_ref):
    rnd = lambda x: x.astype(jnp.bfloat16).astype(jnp.float32)
    s = s_ref[...]
    s_kk = jnp.sum(rnd(s) * rnd(kk_ref[...]), axis=-1, keepdims=True)
    s = s * w_ref[...] - s_kk * kka_ref[...] + v_ref[...] * km_ref[...]
    o_ref[...] = jnp.sum(rnd(s) * rnd(r_ref[...]), axis=-1, keepdims=True)
    st_ref[...] = s


def rwkv_step(r, w, k, v, kk, a, state, groups_per_block=16):
    B, _, width = r.shape
    n_heads = width // A_HEAD
    n_groups = B * n_heads
    gb = _pick_tile(n_groups, groups_per_block, 1)
    row = lambda x: x.reshape(n_groups, 1, A_HEAD)
    row_spec = pl.BlockSpec((gb, 1, A_HEAD), lambda g: (g, 0, 0))
    col_spec = pl.BlockSpec((gb, A_HEAD, 1), lambda g: (g, 0, 0))
    st_spec = pl.BlockSpec((gb, A_HEAD, A_HEAD), lambda g: (g, 0, 0))
    o, st = pl.pallas_call(
        _rwkv_step_kernel,
        grid=(n_groups // gb,),
        in_specs=[st_spec] + [row_spec] * 5 + [col_spec],
        out_specs=[col_spec, st_spec],
        out_shape=[jax.ShapeDtypeStruct((n_groups, A_HEAD, 1), jnp.float32),
                   jax.ShapeDtypeStruct((n_groups, A_HEAD, A_HEAD), jnp.float32)],
        compiler_params=pltpu.CompilerParams(dimension_semantics=("parallel",), vmem_limit_bytes=VMEM_LIMIT_BYTES),
        name="rwkv_step",
    )(state.reshape(n_groups, A_HEAD, A_HEAD), row(kk), row(kk * a), row(w), row(k), row(r),
      v.reshape(n_groups, A_HEAD, 1))
    return o.reshape(B, 1, width), st.reshape(state.shape)


def rwkv_scan(r, w, k, v, kk, a, state):
    B, T, width = r.shape
    n_heads = width // A_HEAD
    if T == 1:
        return rwkv_step(r, w, k, v, kk, a, state)
    assert SCAN_CHUNK == A_HEAD and (T % SCAN_CHUNK == 0 or T < SCAN_CHUNK)
    n_chunks = pl.cdiv(T, SCAN_CHUNK)
    t_pad = n_chunks * SCAN_CHUNK
    pad = lambda x: jnp.pad(x, ((0, 0), (0, t_pad - T), (0, 0)))
    vt = jnp.transpose(pad(v).reshape(B, n_chunks, SCAN_CHUNK, n_heads, A_HEAD), (0, 1, 4, 3, 2))
    vt = vt.reshape(B, n_chunks, A_HEAD, width)
    st0 = jnp.transpose(state, (0, 2, 1, 3)).reshape(B, A_HEAD, width)
    row_spec = pl.BlockSpec((B, SCAN_CHUNK, width), lambda c: (0, c, 0))
    col_spec = pl.BlockSpec((B, 1, A_HEAD, width), lambda c: (0, c, 0, 0))
    st_spec = pl.BlockSpec((B, A_HEAD, width), lambda c: (0, 0, 0))
    ot, st = pl.pallas_call(
        functools.partial(_rwkv_scan_kernel, n_steps=min(T, SCAN_CHUNK)),
        grid=(n_chunks,),
        in_specs=[col_spec] + [row_spec] * 5 + [st_spec],
        out_specs=[col_spec, st_spec],
        out_shape=[jax.ShapeDtypeStruct((B, n_chunks, A_HEAD, width), jnp.float32),
                   jax.ShapeDtypeStruct((B, A_HEAD, width), jnp.float32)],
        scratch_shapes=[pltpu.VMEM((B, A_HEAD, width), jnp.float32)],
        compiler_params=pltpu.CompilerParams(
            dimension_semantics=("arbitrary",),
            vmem_limit_bytes=VMEM_LIMIT_BYTES),
        name="rwkv_scan",
    )(vt, pad(kk), pad(kk * a), pad(w), pad(k), pad(r), st0)
    o = jnp.transpose(ot.reshape(B, n_chunks, A_HEAD, n_heads, SCAN_CHUNK), (0, 1, 4, 3, 2)).reshape(B, t_pad, width)
    new_state = jnp.transpose(st.reshape(B, A_HEAD, n_heads, A_HEAD), (0, 2, 1, 3))
    return o[:, :T], new_state


def _split2(x):
    hi = x.astype(jnp.bfloat16)
    return hi, (x - hi.astype(jnp.float32)).astype(jnp.bfloat16)


def _dot_nt_split(a, b):
    a_hi, a_lo = _split2(a)
    b_hi, b_lo = _split2(b)
    return _dot_nt(jnp.concatenate([a_hi, a_hi, a_lo], axis=1), jnp.concatenate([b_hi, b_lo, b_hi], axis=1))


def _gla_segment_matrix(chunk):
    t = np.arange(chunk)[:, None]
    u = np.arange(chunk)[None, :]
    rows = []
    for lvl in range(chunk.bit_length() - 1):
        m = (chunk // 2) >> lvl
        blk = t // m
        right = np.logical_and(u >= blk * m, u <= t)
        left = np.logical_and(u > t, u < (blk + 1) * m)
        rows.append(np.where(blk % 2 == 1, right, left))
    rows.append(u <= t)
    rows.append(u > t)
    seg = np.concatenate(rows, axis=0).astype(np.float32)
    return np.concatenate([seg, seg, seg], axis=1)


def _dot_nt(a, b, **kw):
    return lax.dot_general(a, b, (((1,), (1,)), ((), ())), preferred_element_type=jnp.float32, **kw)


def _gla_kernel(seg_ref, q_ref, k_ref, v_ref, lf_ref, g_ref, gn_ref, s0_ref, y_ref, st_ref, s_scr):
    c = pl.program_id(1)
    chunk = q_ref.shape[1]
    n_levels = chunk.bit_length() - 1
    n_heads = s_scr.shape[0]
    bf = jnp.bfloat16

    @pl.when(c == 0)
    def _():
        s_scr[...] = s0_ref[0]

    row = lax.broadcasted_iota(jnp.int32, (chunk, chunk), 0)
    col = lax.broadcasted_iota(jnp.int32, (chunk, chunk), 1)
    row_w = lax.broadcasted_iota(jnp.int32, (chunk, C_EXPAND), 0)
    seg = seg_ref[...]
    gn = gn_ref[...]

    def head_group(hg, carry):
        hs = [hg * GLA_HEAD_GROUP + u for u in range(GLA_HEAD_GROUP)]
        lanes = [pl.ds(pl.multiple_of(h * C_EXPAND, C_EXPAND), C_EXPAND) for h in hs]
        qs = [q_ref[0, :, ls] for ls in lanes]
        ks = [k_ref[0, :, ls] for ls in lanes]
        es = []
        for ls in lanes:
            lf = lf_ref[0, :, ls]
            lf_hi = lf.astype(bf)
            lf_r = lf - lf_hi.astype(jnp.float32)
            lf_mid = lf_r.astype(bf)
            lf_lo = (lf_r - lf_mid.astype(jnp.float32)).astype(bf)
            es.append(jnp.exp(jnp.dot(seg, jnp.concatenate([lf_hi, lf_mid, lf_lo], axis=0),
                                      preferred_element_type=jnp.float32)))
        grams = []
        for q, k, e in zip(qs, ks, es):
            g_h = [_dot_nt_split(q, k)]
            for lvl in range(n_levels):
                sh = n_levels - 1 - lvl
                x = jnp.where(((row_w >> sh) & 1) == 1, q, k) * e[lvl * chunk:(lvl + 1) * chunk]
                g_h.append(_dot_nt_split(x, x))
            grams.append(g_h)
        outs = []
        for h, ls, q, k, e, g_h in zip(hs, lanes, qs, ks, es, grams):
            a = jnp.where(row == col, g_h[0], 0.0)
            for lvl in range(n_levels):
                sh = n_levels - 1 - lvl
                pair = (((row >> sh) & 1) == 1) & ((col >> sh) == (row >> sh) - 1)
                a = a + jnp.where(pair, g_h[lvl + 1], 0.0)
            e_b = e[n_levels * chunk:(n_levels + 1) * chunk]
            e_r = e[(n_levels + 1) * chunk:]
            v = v_ref[0, :, ls].astype(bf)
            st = s_scr[h]
            o = jnp.dot(a.astype(bf), v, preferred_element_type=jnp.float32)
            outs.append(o + _dot_nt((q * e_b).astype(bf), st.astype(bf)))
            s_scr[h] = st * e_b[chunk - 1:chunk, :] + lax.dot_general(
                v, (k * e_r).astype(bf), (((0,), (0,)), ((), ())), preferred_element_type=jnp.float32)
        for ls, o in zip(lanes, outs):
            o = o * lax.rsqrt(jnp.mean(o * o, axis=1, keepdims=True) + RMS_EPS) * gn
            gate = g_ref[0, :, ls]
            y_ref[0, :, ls] = o * (gate * jax.nn.sigmoid(gate))
        return carry

    lax.fori_loop(0, n_heads // GLA_HEAD_GROUP, head_group, 0)

    @pl.when(c == pl.num_programs(1) - 1)
    def _():
        st_ref[0] = s_scr[...]


def gla_gated(q, k, v, log_f, g, gn_g, state):
    B, T, width = q.shape
    n_heads = width // C_EXPAND
    assert v.shape[2] == width, "kernel assumes dk == dv == C_EXPAND"
    n_chunks = pl.cdiv(T, GLA_CHUNK)
    t_pad = n_chunks * GLA_CHUNK
    pad = lambda x: jnp.pad(x, ((0, 0), (0, t_pad - T), (0, 0)))
    seg = jnp.asarray(_gla_segment_matrix(GLA_CHUNK), dtype=jnp.bfloat16)
    st0 = jnp.swapaxes(state, 2, 3)
    tok_spec = pl.BlockSpec((1, GLA_CHUNK, width), lambda b, c: (b, c, 0))
    st_spec = pl.BlockSpec((1, n_heads, C_EXPAND, C_EXPAND), lambda b, c: (b, 0, 0, 0))
    y, st = pl.pallas_call(
        _gla_kernel,
        grid=(B, n_chunks),
        in_specs=[pl.BlockSpec(seg.shape, lambda b, c: (0, 0))] + [tok_spec] * 5
                 + [pl.BlockSpec((1, C_EXPAND), lambda b, c: (0, 0)), st_spec],
        out_specs=[tok_spec, st_spec],
        out_shape=[jax.ShapeDtypeStruct((B, t_pad, width), jnp.float32),
                   jax.ShapeDtypeStruct(st0.shape, jnp.float32)],
        scratch_shapes=[pltpu.VMEM((n_heads, C_EXPAND, C_EXPAND), jnp.float32)],
        compiler_params=pltpu.CompilerParams(
            dimension_semantics=("parallel", "arbitrary"),
            vmem_limit_bytes=VMEM_LIMIT_BYTES),
        name="gla_gated",
    )(seg, pad(q), pad(k), pad(v), pad(log_f), pad(g), gn_g.reshape(1, C_EXPAND), st0)
    return y[:, :T], jnp.swapaxes(st, 2, 3)


def _moe_up_kernel(te_ref, nt_ref, x_ref, wg_ref, wu_ref, o_ref):
    del te_ref
    used = pl.program_id(1) < nt_ref[0]

    @pl.when(used)
    def _():
        xb = x_ref[...].astype(jnp.bfloat16)
        g = jnp.dot(xb, wg_ref[0, 0].astype(jnp.bfloat16), preferred_element_type=jnp.float32)
        u = jnp.dot(xb, wu_ref[0, 0].astype(jnp.bfloat16), preferred_element_type=jnp.float32)
        o_ref[...] = (g * jax.nn.sigmoid(g) * u).astype(o_ref.dtype)

    @pl.when(jnp.logical_not(used))
    def _():
        o_ref[...] = jnp.zeros_like(o_ref)


def _moe_down_kernel(te_ref, nt_ref, h_ref, wd_ref, o_ref):
    del te_ref
    used = pl.program_id(1) < nt_ref[0]

    @pl.when(used)
    def _():
        o_ref[...] = jnp.dot(h_ref[...].astype(jnp.bfloat16), wd_ref[0, 0].astype(jnp.bfloat16),
                             preferred_element_type=jnp.float32)

    @pl.when(jnp.logical_not(used))
    def _():
        o_ref[...] = jnp.zeros_like(o_ref)


def moe_experts(tile_expert, n_tiles_used, xs, wg, wu, wd, layer, tm, tf=704, tn=1024):
    rows, d_model = xs.shape
    d_ff = wg.shape[3]
    tf = _pick_tile(d_ff, tf, 128)
    tn = _pick_tile(d_model, tn, 128)
    params = pltpu.CompilerParams(dimension_semantics=("parallel", "arbitrary"), vmem_limit_bytes=VMEM_LIMIT_BYTES)
    h = pl.pallas_call(
        _moe_up_kernel,
        grid_spec=pltpu.PrefetchScalarGridSpec(
            num_scalar_prefetch=2, grid=(d_ff // tf, rows // tm),
            in_specs=[pl.BlockSpec((tm, d_model), lambda j, i, te, nt: (i, 0)),
                      pl.BlockSpec((1, 1, d_model, tf), lambda j, i, te, nt: (layer, te[i], 0, j)),
                      pl.BlockSpec((1, 1, d_model, tf), lambda j, i, te, nt: (layer, te[i], 0, j))],
            out_specs=pl.BlockSpec((tm, tf), lambda j, i, te, nt: (i, j))),
        out_shape=jax.ShapeDtypeStruct((rows, d_ff), jnp.bfloat16 if rows >= BF16_ROWS_MIN else jnp.float32),
        compiler_params=params, name="moe_up",
    )(tile_expert, n_tiles_used, xs, wg, wu)
    return pl.pallas_call(
        _moe_down_kernel,
        grid_spec=pltpu.PrefetchScalarGridSpec(
            num_scalar_prefetch=2, grid=(d_model // tn, rows // tm),
            in_specs=[pl.BlockSpec((tm, d_ff), lambda j, i, te, nt: (i, 0)),
                      pl.BlockSpec((1, 1, d_ff, tn), lambda j, i, te, nt: (layer, te[i], 0, j))],
            out_specs=pl.BlockSpec((tm, tn), lambda j, i, te, nt: (i, j))),
        out_shape=jax.ShapeDtypeStruct((rows, d_model), jnp.float32),
        compiler_params=params, name="moe_down",
    )(tile_expert, n_tiles_used, h, wd)


def moe_swiglu(x, w_router, wg, wu, wd, layer):
    lead, d_model = x.shape[:-1], x.shape[-1]
    x = x.reshape(-1, d_model)
    n_tok = x.shape[0]
    n_pairs = n_tok * TOP_EXPERTS
    tm = min(512, n_pairs)
    assert n_pairs % tm == 0
    probs = jax.nn.softmax(matmul(x, (w_router, layer)), axis=-1)
    top_p, top_i = lax.top_k(probs, TOP_EXPERTS)
    top_p = top_p / jnp.sum(top_p, axis=-1, keepdims=True)
    pair_expert = top_i.reshape(-1)
    member = jax.nn.one_hot(pair_expert, N_EXPERTS, dtype=jnp.int32)
    seen = jnp.cumsum(member, axis=0)
    rank = jnp.sum(member * seen, axis=1) - 1
    counts = seen[-1]
    padded = (counts + tm - 1) // tm * tm
    padded_end = jnp.cumsum(padded)
    pair_row = ((padded_end - padded)[pair_expert] + rank).astype(jnp.int32)
    rows = n_pairs + N_EXPERTS * tm
    row_token = jnp.zeros((rows,), jnp.int32).at[pair_row].set(jnp.arange(n_pairs, dtype=jnp.int32) // TOP_EXPERTS)
    tile_start = jnp.arange(rows // tm, dtype=jnp.int32) * tm
    tile_expert = jnp.minimum(jnp.searchsorted(padded_end, tile_start, side='right'), N_EXPERTS - 1).astype(jnp.int32)
    n_tiles_used = (padded_end[-1:] // tm).astype(jnp.int32)
    y_rows = moe_experts(tile_expert, n_tiles_used, x[row_token], wg, wu, wd, layer, tm)
    pair_row = pair_row.reshape(n_tok, TOP_EXPERTS)
    y = sum(top_p[:, s:s + 1] * y_rows[pair_row[:, s]] for s in range(TOP_EXPERTS))
    return y.reshape(lead + (d_model,))


def split_cols(a, sizes):
    return jnp.split(a, np.cumsum(sizes)[:-1].tolist(), axis=-1)


def gather_rows(a, idx):
    return jax.vmap(lambda a_b, i_b: a_b[i_b])(a, idx)


def layer_norm(x, g, b):
    mu = jnp.mean(x, axis=-1, keepdims=True)
    var = jnp.mean(jnp.square(x - mu), axis=-1, keepdims=True)
    return (x - mu) * lax.rsqrt(var + LN_EPS) * g + b


def t5_bucket(dist):
    dist = jnp.maximum(dist, 0)
    max_exact = T5_BUCKETS // 2
    d = jnp.maximum(dist, 1).astype(jnp.float32)
    large = max_exact + (jnp.log(d / max_exact) / math.log(T5_MAX_DIST / max_exact) * (T5_BUCKETS - max_exact)).astype(jnp.int32)
    large = jnp.minimum(large, T5_BUCKETS - 1)
    return jnp.where(dist < max_exact, dist, large)


def dsa_attend(q, qi, wi, q_pos, k_idx, gather_kv, rel_bias, topk):
    B, T = q.shape[:2]
    L = k_idx.shape[1]
    blk = min(Q_BLOCK, T)
    nb = -(-T // blk)
    pad = nb * blk - T

    def to_blocks(a):
        a = jnp.pad(a, [(0, 0), (0, pad)] + [(0, 0)] * (a.ndim - 2))
        return jnp.moveaxis(a.reshape((B, nb, blk) + a.shape[2:]), 1, 0)

    pos_b = jnp.pad(q_pos, (0, pad)).reshape(nb, blk)
    k_pos = jnp.arange(L, dtype=jnp.int32)

    def one_block(args):
        qb, qib, wib, pb = args
        s = jnp.einsum('bthd,bsd->bths', qib, k_idx) * IDX_DIM ** -0.5
        score = jnp.einsum('bth,bths->bts', wib, jax.nn.relu(s))
        score = jnp.where(k_pos[None, None, :] <= pb[None, :, None], score, -jnp.inf)
        _, sel = lax.top_k(score, topk)
        valid = sel <= pb[None, :, None]
        k_sel, v_sel = gather_kv(sel)
        logits = jnp.einsum('bthd,btkhd->bhtk', qb, k_sel) * B_HEAD ** -0.5
        bias = rel_bias[t5_bucket(pb[None, :, None] - sel)]
        logits = logits + jnp.transpose(bias, (0, 3, 1, 2))
        logits = jnp.where(valid[:, None], logits, -jnp.inf)
        p = jax.nn.softmax(logits, axis=-1)
        return jnp.einsum('bhtk,btkhd->bthd', p, v_sel)

    out = lax.map(one_block, (to_blocks(q), to_blocks(qi), to_blocks(wi), pos_b))
    return jnp.moveaxis(out, 0, 1).reshape((B, nb * blk) + q.shape[2:])[:, :T]


def attend_paged(q, k, v, qi, ki, wi, cache_k, cache_v, cache_kidx, layer, page_table, rel_bias):
    B, T = q.shape[:2]
    past = page_table.shape[1] * PAGE_SIZE
    pages = page_table + layer * cache_k.shape[1]
    kidx_past = cache_kidx.reshape((-1,) + cache_kidx.shape[2:])[pages].reshape(B, past, IDX_DIM)
    kidx_all = jnp.concatenate([kidx_past, ki], axis=1)
    flat_k = cache_k.reshape((-1,) + cache_k.shape[3:])
    flat_v = cache_v.reshape((-1,) + cache_v.shape[3:])

    def gather(sel):
        ps = jnp.minimum(sel, past - 1)
        row = gather_rows(pages, ps // PAGE_SIZE) * PAGE_SIZE + ps % PAGE_SIZE
        is_new = (sel >= past)[..., None, None]
        ns = jnp.clip(sel - past, 0, T - 1)
        k_sel = jnp.where(is_new, gather_rows(k, ns), flat_k[row])
        v_sel = jnp.where(is_new, gather_rows(v, ns), flat_v[row])
        return k_sel, v_sel

    q_pos = past + jnp.arange(T, dtype=jnp.int32)
    return dsa_attend(q, qi, wi, q_pos, kidx_all, gather, rel_bias, min(TOPK_MAX, (past + T) // 4))


def rwkv7_mix(p_a, shift_state, wkv_state, mu, w0, w2, a0, a2, g2, kk_s, ka_s, rk_s, lnx_g, lnx_b):
    B, T, _ = p_a.shape
    a_width = w0.shape[0]
    a_heads = a_width // A_HEAD
    prev = jnp.concatenate([shift_state[:, None, :], p_a[:, :-1]], axis=1)
    xm = p_a + (prev - p_a) * mu
    r, k, v, wd, ad, gd = split_cols(xm, [a_width, a_width, a_width, W_LORA, AA_LORA, G_LORA])
    w_pre = w0 + mm(jnp.tanh(wd), w2)
    decay = jnp.exp(-jnp.exp(-jax.nn.softplus(-w_pre) - 0.5))
    a = jax.nn.sigmoid(a0 + mm(ad, a2))
    g = mm(jax.nn.sigmoid(gd), g2)
    heads = lambda t: t.reshape(B, T, a_heads, A_HEAD)
    kk = heads(k * kk_s)
    kk = kk / jnp.maximum(jnp.sqrt(jnp.sum(jnp.square(kk), axis=-1, keepdims=True)), 1e-12)
    k_mod = heads(k * (1.0 + (a - 1.0) * ka_s))
    r_h, v_h = heads(r), heads(v)
    o, S = rwkv_scan(r, decay, k_mod.reshape(B, T, a_width), v, kk.reshape(B, T, a_width), a, wkv_state)
    o = heads(o)
    mu_o = jnp.mean(o, axis=-1, keepdims=True)
    var_o = jnp.mean(jnp.square(o - mu_o), axis=-1, keepdims=True)
    o = ((o - mu_o) * lax.rsqrt(var_o + RWKV_GN_EPS)).reshape(B, T, a_width) * lnx_g + lnx_b
    bonus = jnp.sum(r_h * k_mod * rk_s.reshape(a_heads, A_HEAD), axis=-1, keepdims=True) * v_h
    y = (o + bonus.reshape(B, T, a_width)) * g
    return y, p_a[:, -1], S


def ab_mixer(x, shift_state, wkv_state, attend, w_in, rw, w_out):
    B, T, d_model = x.shape
    a_width = rw[1].shape[0]
    a_proj = 3 * a_width + W_LORA + AA_LORA + G_LORA
    b_width = d_model - a_width
    b_heads = b_width // B_HEAD
    proj = mm(x, w_in)
    y_a, new_shift, new_wkv = rwkv7_mix(proj[..., :a_proj], shift_state, wkv_state, *rw)
    q, k, v, qi, ki, wi = split_cols(proj[..., a_proj:], [b_width, b_width, b_width, IDX_HEADS * IDX_DIM, IDX_DIM, IDX_HEADS])
    y_b = attend(q, k, v, qi, ki, wi)
    y = mm(jnp.concatenate([y_a, y_b], axis=-1), w_out)
    heads = lambda t: t.reshape(B, T, b_heads, B_HEAD)
    return y, new_shift, new_wkv, heads(k), heads(v), ki


def attend_sample(q, k, v, qi, ki, wi, **kw):
    B, T, b_width = q.shape
    heads = lambda t: t.reshape(B, T, b_width // B_HEAD, B_HEAD)
    y = attend_paged(heads(q), heads(k), heads(v), qi.reshape(B, T, IDX_HEADS, IDX_DIM), ki, wi * IDX_HEADS ** -0.5, **kw)
    return y.reshape(B, T, b_width)


def hgrn2_mixer(x, state, w_in, lb, gn_g, w_out):
    B, T, c_width = x.shape
    q, f_raw, i, g = split_cols(mm(x, w_in), [c_width, c_width, c_width, c_width])
    log_f = jnp.logaddexp(jnp.log(lb), jnp.log1p(-lb) + jax.nn.log_sigmoid(f_raw))
    k_in = (1.0 - lb) * jax.nn.sigmoid(-f_raw)
    y, new_state = gla_gated(jax.nn.silu(q), k_in, i, log_f, g, gn_g, state)
    return mm(y, w_out), new_state


def kernel(x_prompt, x_sample, cache_k, cache_v, cache_kidx, state_wkv, state_shift, state_hgrn, page_table, w_in_ab, rwkv_mu, rwkv_w0, rwkv_w2, rwkv_a0, rwkv_a2, rwkv_g2, rwkv_kk, rwkv_ka, rwkv_rk, rwkv_lnx_g, rwkv_lnx_b, w_out_ab, rel_bias, ffn_w_gate, ffn_w_up, ffn_w_down, w_in_c, hgrn_lb, hgrn_gnorm, w_out_c, router, moe_w_gate, moe_w_up, moe_w_down, ln1_g, ln1_b, ln2_g, ln2_b):
    depth = ln1_g.shape[0]
    alpha = (2 * depth) ** 0.25
    a_width = rwkv_w0.shape[1]
    a_heads = a_width // A_HEAD
    a_proj = state_shift.shape[-1]
    c_heads, c_dk, c_dv = state_hgrn.shape[2:]
    lb_all = jnp.cumsum(jax.nn.softmax(hgrn_lb, axis=0), axis=0)
    lb_all = lb_all - lb_all[0:1]
    attend_p = functools.partial(dsa_prompt, rel_bias=rel_bias)
    xp, xs = x_prompt, x_sample
    kp_l, vp_l, kip_l, wkvp_l, shp_l, hgp_l = [], [], [], [], [], []
    ks_l, vs_l, kis_l, wkvs_l, shs_l, hgs_l = [], [], [], [], [], []
    for l in range(depth):
        j = l // 2
        if l % 2 == 0:
            rw = (rwkv_mu[j], rwkv_w0[j], rwkv_w2[j], rwkv_a0[j], rwkv_a2[j], rwkv_g2[j],
                  rwkv_kk[j], rwkv_ka[j], rwkv_rk[j], rwkv_lnx_g[j], rwkv_lnx_b[j])
            attend_s = functools.partial(attend_sample, cache_k=cache_k, cache_v=cache_v, cache_kidx=cache_kidx,
                                         layer=j, page_table=page_table, rel_bias=rel_bias)
            shift0 = jnp.zeros((xp.shape[0], a_proj), xp.dtype)
            wkv0 = jnp.zeros((xp.shape[0], a_heads, A_HEAD, A_HEAD), jnp.float32)
            yp, shp, wkvp, kp, vp, kip = ab_mixer(xp, shift0, wkv0, attend_p, (w_in_ab, j), rw, (w_out_ab, j))
            ys, shs, wkvs, kss, vss, kis = ab_mixer(xs, state_shift[j], state_wkv[j], attend_s, (w_in_ab, j), rw, (w_out_ab, j))
            kp_l.append(kp); vp_l.append(vp); kip_l.append(kip); wkvp_l.append(wkvp); shp_l.append(shp)
            ks_l.append(kss); vs_l.append(vss); kis_l.append(kis); wkvs_l.append(wkvs); shs_l.append(shs)
        else:
            hg0 = jnp.zeros((xp.shape[0], c_heads, c_dk, c_dv), jnp.float32)
            yp, hgp = hgrn2_mixer(xp, hg0, (w_in_c, j), lb_all[j], hgrn_gnorm[j], (w_out_c, j))
            ys, hgs = hgrn2_mixer(xs, state_hgrn[j], (w_in_c, j), lb_all[j], hgrn_gnorm[j], (w_out_c, j))
            hgp_l.append(hgp); hgs_l.append(hgs)
        xp = layer_norm(alpha * xp + yp, ln1_g[l], ln1_b[l])
        xs = layer_norm(alpha * xs + ys, ln1_g[l], ln1_b[l])
        if l % 2 == 0:
            fp = swiglu(xp, (ffn_w_gate, j), (ffn_w_up, j), (ffn_w_down, j))
            fs = swiglu(xs, (ffn_w_gate, j), (ffn_w_up, j), (ffn_w_down, j))
        else:
            fp = moe_swiglu(xp, router, moe_w_gate, moe_w_up, moe_w_down, j)
            fs = moe_swiglu(xs, router, moe_w_gate, moe_w_up, moe_w_down, j)
        xp = layer_norm(alpha * xp + fp, ln2_g[l], ln2_b[l])
        xs = layer_norm(alpha * xs + fs, ln2_g[l], ln2_b[l])
    return (xp, xs, jnp.stack(kp_l), jnp.stack(vp_l), jnp.stack(kip_l), jnp.stack(wkvp_l), jnp.stack(shp_l),
            jnp.stack(hgp_l), jnp.stack(ks_l), jnp.stack(vs_l), jnp.stack(kis_l), jnp.stack(wkvs_l),
            jnp.stack(shs_l), jnp.stack(hgs_l))
```

```python
import functools
import math

import jax
import jax.numpy as jnp
import numpy as np
from jax import lax
from jax.experimental import pallas as pl
from jax.experimental.pallas import tpu as pltpu

A_HEAD = 64
W_LORA = 64
AA_LORA = 64
G_LORA = 128
RWKV_GN_EPS = 64e-5
B_HEAD = 128
IDX_HEADS = 16
IDX_DIM = 64
TOPK_MAX = 256
Q_BLOCK = 64
T5_BUCKETS = 32
T5_MAX_DIST = 128
C_EXPAND = 128
GLA_CHUNK = 64
GLA_HEAD_GROUP = 8
RMS_EPS = 1e-5
N_EXPERTS = 8
TOP_EXPERTS = 2
PAGE_SIZE = 128
LN_EPS = 1e-5

VMEM_LIMIT_BYTES = 56 * 1024 * 1024
BF16_ROWS_MIN = 256


def _pick_tile(n, target, align):
    if n <= target:
        return n
    t = (target // align) * align
    while t >= align:
        if n % t == 0:
            return t
        t -= align
    return n


def _layer(w):
    return w if isinstance(w, tuple) else (w[None], 0)


def _matmul_kernel(x_ref, w_ref, o_ref):
    @pl.when(pl.program_id(2) == 0)
    def _():
        o_ref[...] = jnp.zeros_like(o_ref)

    o_ref[...] += jnp.dot(x_ref[...].astype(jnp.bfloat16), w_ref[0].astype(jnp.bfloat16),
                          preferred_element_type=jnp.float32)


def matmul(x, w, tm=1024, tn=1024, tk=2048):
    m, k = x.shape
    w, layer = _layer(w)
    k2, n = w.shape[1:]
    assert k == k2
    tm = _pick_tile(m, tm, 8)
    tk = _pick_tile(k, tk, 128)
    tn = min(tn, n) if n % 128 == 0 or n < 128 else tn
    grid = (pl.cdiv(n, tn), pl.cdiv(m, tm), k // tk)
    return pl.pallas_call(
        _matmul_kernel,
        grid=grid,
        in_specs=[pl.BlockSpec((tm, tk), lambda j, i, kk: (i, kk)),
                  pl.BlockSpec((1, tk, tn), lambda j, i, kk: (layer, kk, j))],
        out_specs=pl.BlockSpec((tm, tn), lambda j, i, kk: (i, j)),
        out_shape=jax.ShapeDtypeStruct((m, n), jnp.float32),
        compiler_params=pltpu.CompilerParams(
            dimension_semantics=("parallel", "parallel", "arbitrary"),
            vmem_limit_bytes=VMEM_LIMIT_BYTES),
        name="matmul",
    )(x, w)


def _swiglu_up_kernel(x_ref, wg_ref, wu_ref, o_ref):
    xb = x_ref[...].astype(jnp.bfloat16)
    g = jnp.dot(xb, wg_ref[0].astype(jnp.bfloat16), preferred_element_type=jnp.float32)
    u = jnp.dot(xb, wu_ref[0].astype(jnp.bfloat16), preferred_element_type=jnp.float32)
    o_ref[...] = (g * jax.nn.sigmoid(g) * u).astype(o_ref.dtype)


def swiglu_up(x, wg, wu, tm=1024, tn=512):
    m, k = x.shape
    (wg, layer), (wu, _) = _layer(wg), _layer(wu)
    n = wg.shape[2]
    tm = _pick_tile(m, tm, 8)
    tn = _pick_tile(n, tn, 128)
    return pl.pallas_call(
        _swiglu_up_kernel,
        grid=(n // tn, m // tm),
        in_specs=[pl.BlockSpec((tm, k), lambda j, i: (i, 0)),
                  pl.BlockSpec((1, k, tn), lambda j, i: (layer, 0, j)),
                  pl.BlockSpec((1, k, tn), lambda j, i: (layer, 0, j))],
        out_specs=pl.BlockSpec((tm, tn), lambda j, i: (i, j)),
        out_shape=jax.ShapeDtypeStruct((m, n), x.dtype),
        compiler_params=pltpu.CompilerParams(
            dimension_semantics=("parallel", "parallel"),
            vmem_limit_bytes=VMEM_LIMIT_BYTES),
        name="swiglu_up",
    )(x, wg, wu)


def _mxu_operand(x):
    x = x.reshape(-1, x.shape[-1])
    return x.astype(jnp.bfloat16) if x.shape[0] >= BF16_ROWS_MIN else x


def mm(x, w):
    return matmul(_mxu_operand(x), w).reshape(x.shape[:-1] + (_layer(w)[0].shape[2],))


def swiglu(x, wg, wu, wd):
    h = swiglu_up(_mxu_operand(x), wg, wu)
    return matmul(h, wd).reshape(x.shape[:-1] + (_layer(wd)[0].shape[2],))


INT_MIN = -2 ** 31
NEG_BIG = -1e30


def _sortable_key(x):
    bits = pltpu.bitcast(jnp.where(x == 0.0, 0.0, x), jnp.int32)
    return bits ^ ((bits >> 31) & 0x7FFFFFFF)


def _dsa_prompt_kernel(qt_ref, kt_ref, qi_ref, ki_ref, wit_ref, q_ref, k_ref, v_ref, bias_ref, o_ref,
                       keys_ref, tau_ref, qib_ref, qb_ref, m_ref, l_ref, acc_ref, *, topk, n_idx_bits):
    i = qt_ref[pl.program_id(1)]
    j = kt_ref[pl.program_id(1)]
    tq = q_ref.shape[1]
    n_heads = q_ref.shape[2] // B_HEAD
    kf = float(topk)

    def key_sum(x):
        part = x[:SUBLANES]
        for u in range(1, tq // SUBLANES):
            part = part + x[u * SUBLANES:(u + 1) * SUBLANES]
        return part

    @pl.when(j == 0)
    def _select():
        qib_ref[...] = (qi_ref[0] * IDX_DIM ** -0.5).astype(jnp.bfloat16)
        qb_ref[...] = q_ref[0].astype(jnp.bfloat16)
        m_ref[...] = jnp.full_like(m_ref, NEG_BIG)
        l_ref[...] = jnp.zeros_like(l_ref)
        acc_ref[...] = jnp.zeros_like(acc_ref)
        wb = (wit_ref[0] * IDX_HEADS ** -0.5).astype(jnp.bfloat16).astype(jnp.float32)
        key_pos = lax.broadcasted_iota(jnp.int32, (tq, tq), 0)
        qry_pos = i * tq + lax.broadcasted_iota(jnp.int32, (tq, tq), 1)

        def score_chunk(c, carry):
            kc = ki_ref[0, pl.ds(pl.multiple_of(c * tq, tq), tq), :]
            score = jnp.zeros((tq, tq), jnp.float32)
            for h in range(IDX_HEADS):
                s = _dot_nt(kc, qib_ref[:, h * IDX_DIM:(h + 1) * IDX_DIM])
                r = jnp.maximum(s, 0.0).astype(jnp.bfloat16).astype(jnp.float32)
                score = score + wb[h:h + 1, :] * r
            keys_ref[c] = jnp.where(c * tq + key_pos <= qry_pos, _sortable_key(score), INT_MIN)
            return carry

        lax.fori_loop(0, i + 1, score_chunk, 0)

        def count(pred):
            def body(c, acc):
                return acc + key_sum(jnp.where(pred(keys_ref[c], c), 1.0, 0.0))
            acc = lax.fori_loop(0, i + 1, body, jnp.zeros((SUBLANES, tq), jnp.float32))
            return jnp.sum(acc, axis=0, keepdims=True)

        tau = jnp.where(count(lambda kc, c: kc >= 0) >= kf, 0, INT_MIN).astype(jnp.int32)

        def bit_step(n, tau):
            cand = tau | jnp.left_shift(jnp.int32(1), 30 - n)
            return jnp.where(count(lambda kc, c: kc >= cand) >= kf, cand, tau)

        tau = lax.fori_loop(0, 31, bit_step, tau)
        tau = jnp.maximum(tau, INT_MIN + 1)
        tau_ref[...] = tau
        n_ge = count(lambda kc, c: kc >= tau)

        @pl.when(jnp.max(n_ge) > kf)
        def _ties():
            need = kf - count(lambda kc, c: kc > tau)

            def idx_step(n, ans):
                cand = ans + jnp.left_shift(jnp.int32(1), n_idx_bits - 1 - n)
                f = count(lambda kc, c: (kc == tau) & (c * tq + key_pos < cand))
                return jnp.where(f < need, cand, ans)

            last = lax.fori_loop(0, n_idx_bits, idx_step, jnp.zeros((1, tq), jnp.int32))

            def demote(c, carry):
                kc = keys_ref[c]
                keys_ref[c] = jnp.where((kc == tau) & (c * tq + key_pos > last), INT_MIN, kc)
                return carry

            lax.fori_loop(0, i + 1, demote, 0)

    sel = keys_ref[j] >= tau_ref[...]
    heads = [slice(h * B_HEAD, (h + 1) * B_HEAD) for h in range(n_heads)]
    raw = [_dot_nt(k_ref[0, :, hs].astype(jnp.bfloat16), qb_ref[:, hs]) for hs in heads]
    probs, alphas = [], []
    for h in range(n_heads):
        logits = jnp.where(sel, raw[h] * B_HEAD ** -0.5 + bias_ref[0, h], NEG_BIG)
        m_old = m_ref[h]
        m_new = jnp.maximum(m_old, jnp.max(logits, axis=0, keepdims=True))
        alpha = jnp.exp(m_old - m_new)
        p = jnp.exp(logits - m_new)
        l_ref[h] = alpha * l_ref[h] + jnp.sum(p, axis=0, keepdims=True)
        m_ref[h] = m_new
        probs.append(p.astype(jnp.bfloat16))
        alphas.append(alpha)
    for h, hs in enumerate(heads):
        acc_ref[h] = alphas[h] * acc_ref[h] + lax.dot_general(
            v_ref[0, :, hs].astype(jnp.bfloat16), probs[h], (((0,), (0,)), ((), ())),
            preferred_element_type=jnp.float32)

    @pl.when(j == i)
    def _finish():
        for h in range(n_heads):
            o_ref[0, :, h * B_HEAD:(h + 1) * B_HEAD] = jnp.transpose(acc_ref[h] / l_ref[h])


def dsa_prompt(q, k, v, qi, ki, wi, rel_bias, tq=256):
    B, T, b_width = q.shape
    n_heads = b_width // B_HEAD
    tq = min(tq, T)
    assert T % tq == 0 and tq % 128 == 0 and tq + 1 >= T5_MAX_DIST
    nq = T // tq
    topk = min(TOPK_MAX, T // 4)
    r = jnp.arange(tq, dtype=jnp.int32)
    dist = jnp.arange(3, dtype=jnp.int32)[:, None, None] * tq + r[None, None, :] - r[None, :, None]
    bias3 = jnp.einsum('dkqc,ch->dhkq', jax.nn.one_hot(t5_bucket(dist), T5_BUCKETS, dtype=jnp.float32), rel_bias,
                       precision=lax.Precision.HIGHEST)
    kern = functools.partial(_dsa_prompt_kernel, topk=topk, n_idx_bits=max(1, (T - 1).bit_length()))
    pairs = np.array([(i, j) for i in range(nq) for j in range(i + 1)], dtype=np.int32)
    q_tile, k_tile = jnp.asarray(pairs[:, 0]), jnp.asarray(pairs[:, 1])
    return pl.pallas_call(
        kern,
        grid_spec=pltpu.PrefetchScalarGridSpec(
            num_scalar_prefetch=2, grid=(B, len(pairs)),
            in_specs=[pl.BlockSpec((1, tq, qi.shape[2]), lambda b, p, qt, kt: (b, qt[p], 0)),
                      pl.BlockSpec((1, T, IDX_DIM), lambda b, p, qt, kt: (b, 0, 0)),
                      pl.BlockSpec((1, IDX_HEADS, tq), lambda b, p, qt, kt: (b, 0, qt[p])),
                      pl.BlockSpec((1, tq, b_width), lambda b, p, qt, kt: (b, qt[p], 0)),
                      pl.BlockSpec((1, tq, b_width), lambda b, p, qt, kt: (b, kt[p], 0)),
                      pl.BlockSpec((1, tq, b_width), lambda b, p, qt, kt: (b, kt[p], 0)),
                      pl.BlockSpec((1, n_heads, tq, tq),
                                   lambda b, p, qt, kt: (jnp.minimum(qt[p] - kt[p], 2), 0, 0, 0))],
            out_specs=pl.BlockSpec((1, tq, b_width), lambda b, p, qt, kt: (b, qt[p], 0)),
            scratch_shapes=[pltpu.VMEM((nq, tq, tq), jnp.int32),
                            pltpu.VMEM((1, tq), jnp.int32),
                            pltpu.VMEM((tq, qi.shape[2]), jnp.bfloat16),
                            pltpu.VMEM((tq, b_width), jnp.bfloat16),
                            pltpu.VMEM((n_heads, 1, tq), jnp.float32),
                            pltpu.VMEM((n_heads, 1, tq), jnp.float32),
                            pltpu.VMEM((n_heads, B_HEAD, tq), jnp.float32)]),
        out_shape=jax.ShapeDtypeStruct((B, T, b_width), jnp.float32),
        compiler_params=pltpu.CompilerParams(
            dimension_semantics=("parallel", "arbitrary"),
            vmem_limit_bytes=VMEM_LIMIT_BYTES),
        name="dsa_prompt",
    )(q_tile, k_tile, qi, ki.astype(jnp.bfloat16), jnp.swapaxes(wi, 1, 2), q, k, v, bias3)


SCAN_CHUNK = A_HEAD
LANES = 128
SUBLANES = 8


def _rwkv_scan_kernel(vt_ref, kk_ref, kka_ref, w_ref, km_ref, r_ref, s0_ref, ot_ref, st_ref, s_scr, *, n_steps):
    c = pl.program_id(0)
    n_batch = vt_ref.shape[0]
    n_tiles = vt_ref.shape[3] // LANES
    bf = jnp.bfloat16

    @pl.when(c == 0)
    def _():
        s_scr[...] = s0_ref[...]

    if n_steps < SCAN_CHUNK:
        ot_ref[...] = jnp.zeros_like(ot_ref)
    lane = lax.broadcasted_iota(jnp.int32, (A_HEAD, LANES), 1)
    seg_base = lane & (LANES - A_HEAD)
    lane_t = lane & (A_HEAD - 1)
    seg_ones = ((lax.broadcasted_iota(jnp.int32, (LANES, LANES), 0) // A_HEAD)
                == (lax.broadcasted_iota(jnp.int32, (LANES, LANES), 1) // A_HEAD)).astype(bf)

    group = min(n_steps, SUBLANES)
    tiles = [(b, slice(p * LANES, (p + 1) * LANES)) for b in range(n_batch) for p in range(n_tiles)]

    def seg_sums(xs):
        lhs = jnp.concatenate([x.astype(bf) for x in xs], axis=0)
        out = jnp.dot(lhs, seg_ones, preferred_element_type=jnp.float32)
        return [out[i * A_HEAD:(i + 1) * A_HEAD] for i in range(len(xs))]

    def step_group(g, carry):
        t0 = g * group if isinstance(g, int) else pl.multiple_of(g * group, group)
        rows = [[ref[b, pl.ds(t0, group), ls] for (b, ls) in tiles] for ref in (kk_ref, kka_ref, w_ref, km_ref, r_ref)]
        for u in range(group):
            kk_r, kka_r, w_r, km_r, r_r = ([x[u:u + 1] for x in q] for q in rows)
            s_kk = seg_sums([s_scr[b, :, ls] * kk_r[i] for i, (b, ls) in enumerate(tiles)])
            s_new = []
            for i, (b, ls) in enumerate(tiles):
                v_col = jnp.take_along_axis(vt_ref[b, 0, :, ls], seg_base + (t0 + u), axis=1)
                s = s_scr[b, :, ls] * w_r[i] - s_kk[i] * kka_r[i] + v_col * km_r[i]
                s_scr[b, :, ls] = s
                s_new.append(s * r_r[i])
            o_col = seg_sums(s_new)
            for i, (b, ls) in enumerate(tiles):
                ot_ref[b, 0, :, ls] = jnp.where(lane_t == t0 + u, o_col[i], ot_ref[b, 0, :, ls])
        return carry

    if n_steps == group:
        step_group(0, 0)
    else:
        lax.fori_loop(0, n_steps // group, step_group, 0)

    @pl.when(c == pl.num_programs(0) - 1)
    def _():
        st_ref[...] = s_scr[...]


def _rwkv_step_kernel(s_ref, kk_ref, kka_ref, w_ref, km_ref, r_ref, v_ref, o_ref, st_ref):
    rnd = lambda x: x.astype(jnp.bfloat16).astype(jnp.float32)
    s = s_ref[...]
    s_kk = jnp.sum(rnd(s) * rnd(kk_ref[...]), axis=-1, keepdims=True)
    s = s * w_ref[...] - s_kk * kka_ref[...] + v_ref[...] * km_ref[...]
    o_ref[...] = jnp.sum(rnd(s) * rnd(r_ref[...]), axis=-1, keepdims=True)
    st_ref[...] = s


def rwkv_step(r, w, k, v, kk, a, state, groups_per_block=16):
    B, _, width = r.shape
    n_heads = width // A_HEAD
    n_groups = B * n_heads
    gb = _pick_tile(n_groups, groups_per_block, 1)
    row = lambda x: x.reshape(n_groups, 1, A_HEAD)
    row_spec = pl.BlockSpec((gb, 1, A_HEAD), lambda g: (g, 0, 0))
    col_spec = pl.BlockSpec((gb, A_HEAD, 1), lambda g: (g, 0, 0))
    st_spec = pl.BlockSpec((gb, A_HEAD, A_HEAD), lambda g: (g, 0, 0))
    o, st = pl.pallas_call(
        _rwkv_step_kernel,
        grid=(n_groups // gb,),
        in_specs=[st_spec] + [row_spec] * 5 + [col_spec],
        out_specs=[col_spec, st_spec],
        out_shape=[jax.ShapeDtypeStruct((n_groups, A_HEAD, 1), jnp.float32),
                   jax.ShapeDtypeStruct((n_groups, A_HEAD, A_HEAD), jnp.float32)],
        compiler_params=pltpu.CompilerParams(dimension_semantics=("parallel",), vmem_limit_bytes=VMEM_LIMIT_BYTES),
        name="rwkv_step",
    )(state.reshape(n_groups, A_HEAD, A_HEAD), row(kk), row(kk * a), row(w), row(k), row(r),
      v.reshape(n_groups, A_HEAD, 1))
    return o.reshape(B, 1, width), st.reshape(state.shape)


def rwkv_scan(r, w, k, v, kk, a, state):
    B, T, width = r.shape
    n_heads = width // A_HEAD
    if T == 1:
        return rwkv_step(r, w, k, v, kk, a, state)
    assert SCAN_CHUNK == A_HEAD and (T % SCAN_CHUNK == 0 or T < SCAN_CHUNK)
    n_chunks = pl.cdiv(T, SCAN_CHUNK)
    t_pad = n_chunks * SCAN_CHUNK
    pad = lambda x: jnp.pad(x, ((0, 0), (0, t_pad - T), (0, 0)))
    vt = jnp.transpose(pad(v).reshape(B, n_chunks, SCAN_CHUNK, n_heads, A_HEAD), (0, 1, 4, 3, 2))
    vt = vt.reshape(B, n_chunks, A_HEAD, width)
    st0 = jnp.transpose(state, (0, 2, 1, 3)).reshape(B, A_HEAD, width)
    row_spec = pl.BlockSpec((B, SCAN_CHUNK, width), lambda c: (0, c, 0))
    col_spec = pl.BlockSpec((B, 1, A_HEAD, width), lambda c: (0, c, 0, 0))
    st_spec = pl.BlockSpec((B, A_HEAD, width), lambda c: (0, 0, 0))
    ot, st = pl.pallas_call(
        functools.partial(_rwkv_scan_kernel, n_steps=min(T, SCAN_CHUNK)),
        grid=(n_chunks,),
        in_specs=[col_spec] + [row_spec] * 5 + [st_spec],
        out_specs=[col_spec, st_spec],
        out_shape=[jax.ShapeDtypeStruct((B, n_chunks, A_HEAD, width), jnp.float32),
                   jax.ShapeDtypeStruct((B, A_HEAD, width), jnp.float32)],
        scratch_shapes=[pltpu.VMEM((B, A_HEAD, width), jnp.float32)],
        compiler_params=pltpu.CompilerParams(
            dimension_semantics=("arbitrary",),
            vmem_limit_bytes=VMEM_LIMIT_BYTES),
        name="rwkv_scan",
    )(vt, pad(kk), pad(kk * a), pad(w), pad(k), pad(r), st0)
    o = jnp.transpose(ot.reshape(B, n_chunks, A_HEAD, n_heads, SCAN_CHUNK), (0, 1, 4, 3, 2)).reshape(B, t_pad, width)
    new_state = jnp.transpose(st.reshape(B, A_HEAD, n_heads, A_HEAD), (0, 2, 1, 3))
    return o[:, :T], new_state


def _split2(x):
    hi = x.astype(jnp.bfloat16)
    return hi, (x - hi.astype(jnp.float32)).astype(jnp.bfloat16)


def _dot_nt_split(a, b):
    a_hi, a_lo = _split2(a)
    b_hi, b_lo = _split2(b)
    return _dot_nt(jnp.concatenate([a_hi, a_hi, a_lo], axis=1), jnp.concatenate([b_hi, b_lo, b_hi], axis=1))


def _gla_segment_matrix(chunk):
    t = np.arange(chunk)[:, None]
    u = np.arange(chunk)[None, :]
    rows = []
    for lvl in range(chunk.bit_length() - 1):
        m = (chunk // 2) >> lvl
        blk = t // m
        right = np.logical_and(u >= blk * m, u <= t)
        left = np.logical_and(u > t, u < (blk + 1) * m)
        rows.append(np.where(blk % 2 == 1, right, left))
    rows.append(u <= t)
    rows.append(u > t)
    seg = np.concatenate(rows, axis=0).astype(np.float32)
    return np.concatenate([seg, seg, seg], axis=1)


def _dot_nt(a, b, **kw):
    return lax.dot_general(a, b, (((1,), (1,)), ((), ())), preferred_element_type=jnp.float32, **kw)


def _gla_kernel(seg_ref, q_ref, k_ref, v_ref, lf_ref, g_ref, gn_ref, s0_ref, y_ref, st_ref, s_scr):
    c = pl.program_id(1)
    chunk = q_ref.shape[1]
    n_levels = chunk.bit_length() - 1
    n_heads = s_scr.shape[0]
    bf = jnp.bfloat16

    @pl.when(c == 0)
    def _():
        s_scr[...] = s0_ref[0]

    row = lax.broadcasted_iota(jnp.int32, (chunk, chunk), 0)
    col = lax.broadcasted_iota(jnp.int32, (chunk, chunk), 1)
    row_w = lax.broadcasted_iota(jnp.int32, (chunk, C_EXPAND), 0)
    seg = seg_ref[...]
    gn = gn_ref[...]

    def head_group(hg, carry):
        hs = [hg * GLA_HEAD_GROUP + u for u in range(GLA_HEAD_GROUP)]
        lanes = [pl.ds(pl.multiple_of(h * C_EXPAND, C_EXPAND), C_EXPAND) for h in hs]
        qs = [q_ref[0, :, ls] for ls in lanes]
        ks = [k_ref[0, :, ls] for ls in lanes]
        es = []
        for ls in lanes:
            lf = lf_ref[0, :, ls]
            lf_hi = lf.astype(bf)
            lf_r = lf - lf_hi.astype(jnp.float32)
            lf_mid = lf_r.astype(bf)
            lf_lo = (lf_r - lf_mid.astype(jnp.float32)).astype(bf)
            es.append(jnp.exp(jnp.dot(seg, jnp.concatenate([lf_hi, lf_mid, lf_lo], axis=0),
                                      preferred_element_type=jnp.float32)))
        grams = []
        for q, k, e in zip(qs, ks, es):
            g_h = [_dot_nt_split(q, k)]
            for lvl in range(n_levels):
                sh = n_levels - 1 - lvl
                x = jnp.where(((row_w >> sh) & 1) == 1, q, k) * e[lvl * chunk:(lvl + 1) * chunk]
                g_h.append(_dot_nt_split(x, x))
            grams.append(g_h)
        outs = []
        for h, ls, q, k, e, g_h in zip(hs, lanes, qs, ks, es, grams):
            a = jnp.where(row == col, g_h[0], 0.0)
            for lvl in range(n_levels):
                sh = n_levels - 1 - lvl
                pair = (((row >> sh) & 1) == 1) & ((col >> sh) == (row >> sh) - 1)
                a = a + jnp.where(pair, g_h[lvl + 1], 0.0)
            e_b = e[n_levels * chunk:(n_levels + 1) * chunk]
            e_r = e[(n_levels + 1) * chunk:]
            v = v_ref[0, :, ls].astype(bf)
            st = s_scr[h]
            o = jnp.dot(a.astype(bf), v, preferred_element_type=jnp.float32)
            outs.append(o + _dot_nt((q * e_b).astype(bf), st.astype(bf)))
            s_scr[h] = st * e_b[chunk - 1:chunk, :] + lax.dot_general(
                v, (k * e_r).astype(bf), (((0,), (0,)), ((), ())), preferred_element_type=jnp.float32)
        for ls, o in zip(lanes, outs):
            o = o * lax.rsqrt(jnp.mean(o * o, axis=1, keepdims=True) + RMS_EPS) * gn
            gate = g_ref[0, :, ls]
            y_ref[0, :, ls] = o * (gate * jax.nn.sigmoid(gate))
        return carry

    lax.fori_loop(0, n_heads // GLA_HEAD_GROUP, head_group, 0)

    @pl.when(c == pl.num_programs(1) - 1)
    def _():
        st_ref[0] = s_scr[...]


def _gla_step_kernel(st_ref, q_ref, k_ref, lf_ref, v_ref, g_ref, gn_ref, y_ref, so_ref):
    rnd = lambda x: x.astype(jnp.bfloat16).astype(jnp.float32)
    q, k, v = q_ref[...], k_ref[...], v_ref[...]
    decay = jnp.exp(lf_ref[...])
    st = st_ref[...]
    o = jnp.sum(q * k, axis=-1, keepdims=True) * v + jnp.sum(rnd(q * decay) * rnd(st), axis=-1, keepdims=True)
    so_ref[...] = st * decay + v * k
    o = o * lax.rsqrt(jnp.mean(o * o, axis=1, keepdims=True) + RMS_EPS) * gn_ref[...]
    gate = g_ref[...]
    y_ref[...] = o * (gate * jax.nn.sigmoid(gate))


def gla_step(q, k, v, log_f, g, gn_g, state, groups_per_block=16):
    B, _, width = q.shape
    n_heads = width // C_EXPAND
    n_groups = B * n_heads
    gb = _pick_tile(n_groups, groups_per_block, 1)
    row = lambda x: x.reshape(n_groups, 1, C_EXPAND)
    col = lambda x: x.reshape(n_groups, C_EXPAND, 1)
    row_spec = pl.BlockSpec((gb, 1, C_EXPAND), lambda i: (i, 0, 0))
    col_spec = pl.BlockSpec((gb, C_EXPAND, 1), lambda i: (i, 0, 0))
    st_spec = pl.BlockSpec((gb, C_EXPAND, C_EXPAND), lambda i: (i, 0, 0))
    y, st = pl.pallas_call(
        _gla_step_kernel,
        grid=(n_groups // gb,),
        in_specs=[st_spec, row_spec, row_spec, row_spec, col_spec, col_spec,
                  pl.BlockSpec((1, C_EXPAND, 1), lambda i: (0, 0, 0))],
        out_specs=[col_spec, st_spec],
        out_shape=[jax.ShapeDtypeStruct((n_groups, C_EXPAND, 1), jnp.float32),
                   jax.ShapeDtypeStruct((n_groups, C_EXPAND, C_EXPAND), jnp.float32)],
        compiler_params=pltpu.CompilerParams(dimension_semantics=("parallel",), vmem_limit_bytes=VMEM_LIMIT_BYTES),
        name="gla_step",
    )(jnp.swapaxes(state, 2, 3).reshape(n_groups, C_EXPAND, C_EXPAND), row(q), row(k), row(log_f), col(v), col(g),
      gn_g.reshape(1, C_EXPAND, 1))
    return y.reshape(B, 1, width), jnp.swapaxes(st.reshape(B, n_heads, C_EXPAND, C_EXPAND), 2, 3)


def gla_gated(q, k, v, log_f, g, gn_g, state):
    B, T, width = q.shape
    n_heads = width // C_EXPAND
    assert v.shape[2] == width, "kernel assumes dk == dv == C_EXPAND"
    if T == 1:
        return gla_step(q, k, v, log_f, g, gn_g, state)
    n_chunks = pl.cdiv(T, GLA_CHUNK)
    t_pad = n_chunks * GLA_CHUNK
    pad = lambda x: jnp.pad(x, ((0, 0), (0, t_pad - T), (0, 0)))
    seg = jnp.asarray(_gla_segment_matrix(GLA_CHUNK), dtype=jnp.bfloat16)
    st0 = jnp.swapaxes(state, 2, 3)
    tok_spec = pl.BlockSpec((1, GLA_CHUNK, width), lambda b, c: (b, c, 0))
    st_spec = pl.BlockSpec((1, n_heads, C_EXPAND, C_EXPAND), lambda b, c: (b, 0, 0, 0))
    y, st = pl.pallas_call(
        _gla_kernel,
        grid=(B, n_chunks),
        in_specs=[pl.BlockSpec(seg.shape, lambda b, c: (0, 0))] + [tok_spec] * 5
                 + [pl.BlockSpec((1, C_EXPAND), lambda b, c: (0, 0)), st_spec],
        out_specs=[tok_spec, st_spec],
        out_shape=[jax.ShapeDtypeStruct((B, t_pad, width), jnp.float32),
                   jax.ShapeDtypeStruct(st0.shape, jnp.float32)],
        scratch_shapes=[pltpu.VMEM((n_heads, C_EXPAND, C_EXPAND), jnp.float32)],
        compiler_params=pltpu.CompilerParams(
            dimension_semantics=("parallel", "arbitrary"),
            vmem_limit_bytes=VMEM_LIMIT_BYTES),
        name="gla_gated",
    )(seg, pad(q), pad(k), pad(v), pad(log_f), pad(g), gn_g.reshape(1, C_EXPAND), st0)
    return y[:, :T], jnp.swapaxes(st, 2, 3)


def _moe_up_kernel(te_ref, nt_ref, x_ref, wg_ref, wu_ref, o_ref):
    del te_ref
    used = pl.program_id(1) < nt_ref[0]

    @pl.when(used)
    def _():
        xb = x_ref[...].astype(jnp.bfloat16)
        g = jnp.dot(xb, wg_ref[0, 0].astype(jnp.bfloat16), preferred_element_type=jnp.float32)
        u = jnp.dot(xb, wu_ref[0, 0].astype(jnp.bfloat16), preferred_element_type=jnp.float32)
        o_ref[...] = (g * jax.nn.sigmoid(g) * u).astype(o_ref.dtype)

    @pl.when(jnp.logical_not(used))
    def _():
        o_ref[...] = jnp.zeros_like(o_ref)


def _moe_down_kernel(te_ref, nt_ref, h_ref, wd_ref, o_ref):
    del te_ref
    used = pl.program_id(1) < nt_ref[0]

    @pl.when(used)
    def _():
        o_ref[...] = jnp.dot(h_ref[...].astype(jnp.bfloat16), wd_ref[0, 0].astype(jnp.bfloat16),
                             preferred_element_type=jnp.float32)

    @pl.when(jnp.logical_not(used))
    def _():
        o_ref[...] = jnp.zeros_like(o_ref)


def moe_experts(tile_expert, n_tiles_used, xs, wg, wu, wd, layer, tm, tf=704, tn=1024):
    rows, d_model = xs.shape
    d_ff = wg.shape[3]
    tf = _pick_tile(d_ff, tf, 128)
    tn = _pick_tile(d_model, tn, 128)
    params = pltpu.CompilerParams(dimension_semantics=("parallel", "arbitrary"), vmem_limit_bytes=VMEM_LIMIT_BYTES)
    h = pl.pallas_call(
        _moe_up_kernel,
        grid_spec=pltpu.PrefetchScalarGridSpec(
            num_scalar_prefetch=2, grid=(d_ff // tf, rows // tm),
            in_specs=[pl.BlockSpec((tm, d_model), lambda j, i, te, nt: (i, 0)),
                      pl.BlockSpec((1, 1, d_model, tf), lambda j, i, te, nt: (layer, te[i], 0, j)),
                      pl.BlockSpec((1, 1, d_model, tf), lambda j, i, te, nt: (layer, te[i], 0, j))],
            out_specs=pl.BlockSpec((tm, tf), lambda j, i, te, nt: (i, j))),
        out_shape=jax.ShapeDtypeStruct((rows, d_ff), jnp.bfloat16 if rows >= BF16_ROWS_MIN else jnp.float32),
        compiler_params=params, name="moe_up",
    )(tile_expert, n_tiles_used, xs, wg, wu)
    return pl.pallas_call(
        _moe_down_kernel,
        grid_spec=pltpu.PrefetchScalarGridSpec(
            num_scalar_prefetch=2, grid=(d_model // tn, rows // tm),
            in_specs=[pl.BlockSpec((tm, d_ff), lambda j, i, te, nt: (i, 0)),
                      pl.BlockSpec((1, 1, d_ff, tn), lambda j, i, te, nt: (layer, te[i], 0, j))],
            out_specs=pl.BlockSpec((tm, tn), lambda j, i, te, nt: (i, j))),
        out_shape=jax.ShapeDtypeStruct((rows, d_model), jnp.float32),
        compiler_params=params, name="moe_down",
    )(tile_expert, n_tiles_used, h, wd)


def moe_swiglu(x, w_router, wg, wu, wd, layer):
    lead, d_model = x.shape[:-1], x.shape[-1]
    x = x.reshape(-1, d_model)
    n_tok = x.shape[0]
    n_pairs = n_tok * TOP_EXPERTS
    tm = min(1024, n_pairs)
    assert n_pairs % tm == 0
    probs = jax.nn.softmax(matmul(x, (w_router, layer)), axis=-1)
    top_p, top_i = lax.top_k(probs, TOP_EXPERTS)
    top_p = top_p / jnp.sum(top_p, axis=-1, keepdims=True)
    pair_expert = top_i.reshape(-1)
    member = jax.nn.one_hot(pair_expert, N_EXPERTS, dtype=jnp.int32)
    seen = jnp.cumsum(member, axis=0)
    rank = jnp.sum(member * seen, axis=1) - 1
    counts = seen[-1]
    padded = (counts + tm - 1) // tm * tm
    padded_end = jnp.cumsum(padded)
    pair_row = ((padded_end - padded)[pair_expert] + rank).astype(jnp.int32)
    rows = n_pairs + N_EXPERTS * tm
    row_token = jnp.zeros((rows,), jnp.int32).at[pair_row].set(jnp.arange(n_pairs, dtype=jnp.int32) // TOP_EXPERTS)
    tile_start = jnp.arange(rows // tm, dtype=jnp.int32) * tm
    tile_expert = jnp.minimum(jnp.searchsorted(padded_end, tile_start, side='right'), N_EXPERTS - 1).astype(jnp.int32)
    n_tiles_used = (padded_end[-1:] // tm).astype(jnp.int32)
    y_rows = moe_experts(tile_expert, n_tiles_used, x[row_token], wg, wu, wd, layer, tm)
    pair_row = pair_row.reshape(n_tok, TOP_EXPERTS)
    y = sum(top_p[:, s:s + 1] * y_rows[pair_row[:, s]] for s in range(TOP_EXPERTS))
    return y.reshape(lead + (d_model,))


def split_cols(a, sizes):
    return jnp.split(a, np.cumsum(sizes)[:-1].tolist(), axis=-1)


def gather_rows(a, idx):
    return jax.vmap(lambda a_b, i_b: a_b[i_b])(a, idx)


def layer_norm(x, g, b):
    mu = jnp.mean(x, axis=-1, keepdims=True)
    var = jnp.mean(jnp.square(x - mu), axis=-1, keepdims=True)
    return (x - mu) * lax.rsqrt(var + LN_EPS) * g + b


def t5_bucket(dist):
    dist = jnp.maximum(dist, 0)
    max_exact = T5_BUCKETS // 2
    d = jnp.maximum(dist, 1).astype(jnp.float32)
    large = max_exact + (jnp.log(d / max_exact) / math.log(T5_MAX_DIST / max_exact) * (T5_BUCKETS - max_exact)).astype(jnp.int32)
    large = jnp.minimum(large, T5_BUCKETS - 1)
    return jnp.where(dist < max_exact, dist, large)


def dsa_attend(q, qi, wi, q_pos, k_idx, gather_kv, rel_bias, topk):
    B, T = q.shape[:2]
    L = k_idx.shape[1]
    blk = min(Q_BLOCK, T)
    nb = -(-T // blk)
    pad = nb * blk - T

    def to_blocks(a):
        a = jnp.pad(a, [(0, 0), (0, pad)] + [(0, 0)] * (a.ndim - 2))
        return jnp.moveaxis(a.reshape((B, nb, blk) + a.shape[2:]), 1, 0)

    pos_b = jnp.pad(q_pos, (0, pad)).reshape(nb, blk)
    k_pos = jnp.arange(L, dtype=jnp.int32)

    def one_block(args):
        qb, qib, wib, pb = args
        s = jnp.einsum('bthd,bsd->bths', qib, k_idx) * IDX_DIM ** -0.5
        score = jnp.einsum('bth,bths->bts', wib, jax.nn.relu(s))
        score = jnp.where(k_pos[None, None, :] <= pb[None, :, None], score, -jnp.inf)
        _, sel = lax.top_k(score, topk)
        valid = sel <= pb[None, :, None]
        k_sel, v_sel = gather_kv(sel)
        logits = jnp.einsum('bthd,btkhd->bhtk', qb, k_sel) * B_HEAD ** -0.5
        bias = rel_bias[t5_bucket(pb[None, :, None] - sel)]
        logits = logits + jnp.transpose(bias, (0, 3, 1, 2))
        logits = jnp.where(valid[:, None], logits, -jnp.inf)
        p = jax.nn.softmax(logits, axis=-1)
        return jnp.einsum('bhtk,btkhd->bthd', p, v_sel)

    out = lax.map(one_block, (to_blocks(q), to_blocks(qi), to_blocks(wi), pos_b))
    return jnp.moveaxis(out, 0, 1).reshape((B, nb * blk) + q.shape[2:])[:, :T]


def attend_paged(q, k, v, qi, ki, wi, cache_k, cache_v, cache_kidx, layer, page_table, rel_bias):
    B, T = q.shape[:2]
    past = page_table.shape[1] * PAGE_SIZE
    pages = page_table + layer * cache_k.shape[1]
    kidx_past = cache_kidx.reshape((-1,) + cache_kidx.shape[2:])[pages].reshape(B, past, IDX_DIM)
    kidx_all = jnp.concatenate([kidx_past, ki], axis=1)
    flat_k = cache_k.reshape((-1,) + cache_k.shape[3:])
    flat_v = cache_v.reshape((-1,) + cache_v.shape[3:])

    def gather(sel):
        ps = jnp.minimum(sel, past - 1)
        row = gather_rows(pages, ps // PAGE_SIZE) * PAGE_SIZE + ps % PAGE_SIZE
        is_new = (sel >= past)[..., None, None]
        ns = jnp.clip(sel - past, 0, T - 1)
        k_sel = jnp.where(is_new, gather_rows(k, ns), flat_k[row])
        v_sel = jnp.where(is_new, gather_rows(v, ns), flat_v[row])
        return k_sel, v_sel

    q_pos = past + jnp.arange(T, dtype=jnp.int32)
    return dsa_attend(q, qi, wi, q_pos, kidx_all, gather, rel_bias, min(TOPK_MAX, (past + T) // 4))


def rwkv7_mix(p_a, shift_state, wkv_state, mu, w0, w2, a0, a2, g2, kk_s, ka_s, rk_s, lnx_g, lnx_b):
    B, T, _ = p_a.shape
    a_width = w0.shape[0]
    a_heads = a_width // A_HEAD
    prev = jnp.concatenate([shift_state[:, None, :], p_a[:, :-1]], axis=1)
    xm = p_a + (prev - p_a) * mu
    r, k, v, wd, ad, gd = split_cols(xm, [a_width, a_width, a_width, W_LORA, AA_LORA, G_LORA])
    w_pre = w0 + mm(jnp.tanh(wd), w2)
    decay = jnp.exp(-jnp.exp(-jax.nn.softplus(-w_pre) - 0.5))
    a = jax.nn.sigmoid(a0 + mm(ad, a2))
    g = mm(jax.nn.sigmoid(gd), g2)
    heads = lambda t: t.reshape(B, T, a_heads, A_HEAD)
    kk = heads(k * kk_s)
    kk = kk / jnp.maximum(jnp.sqrt(jnp.sum(jnp.square(kk), axis=-1, keepdims=True)), 1e-12)
    k_mod = heads(k * (1.0 + (a - 1.0) * ka_s))
    r_h, v_h = heads(r), heads(v)
    o, S = rwkv_scan(r, decay, k_mod.reshape(B, T, a_width), v, kk.reshape(B, T, a_width), a, wkv_state)
    o = heads(o)
    mu_o = jnp.mean(o, axis=-1, keepdims=True)
    var_o = jnp.mean(jnp.square(o - mu_o), axis=-1, keepdims=True)
    o = ((o - mu_o) * lax.rsqrt(var_o + RWKV_GN_EPS)).reshape(B, T, a_width) * lnx_g + lnx_b
    bonus = jnp.sum(r_h * k_mod * rk_s.reshape(a_heads, A_HEAD), axis=-1, keepdims=True) * v_h
    y = (o + bonus.reshape(B, T, a_width)) * g
    return y, p_a[:, -1], S


def ab_mixer(x, shift_state, wkv_state, attend, w_in, rw, w_out):
    B, T, d_model = x.shape
    a_width = rw[1].shape[0]
    a_proj = 3 * a_width + W_LORA + AA_LORA + G_LORA
    b_width = d_model - a_width
    b_heads = b_width // B_HEAD
    proj = mm(x, w_in)
    y_a, new_shift, new_wkv = rwkv7_mix(proj[..., :a_proj], shift_state, wkv_state, *rw)
    q, k, v, qi, ki, wi = split_cols(proj[..., a_proj:], [b_width, b_width, b_width, IDX_HEADS * IDX_DIM, IDX_DIM, IDX_HEADS])
    y_b = attend(q, k, v, qi, ki, wi)
    y = mm(jnp.concatenate([y_a, y_b], axis=-1), w_out)
    heads = lambda t: t.reshape(B, T, b_heads, B_HEAD)
    return y, new_shift, new_wkv, heads(k), heads(v), ki


def attend_sample(q, k, v, qi, ki, wi, **kw):
    B, T, b_width = q.shape
    heads = lambda t: t.reshape(B, T, b_width // B_HEAD, B_HEAD)
    y = attend_paged(heads(q), heads(k), heads(v), qi.reshape(B, T, IDX_HEADS, IDX_DIM), ki, wi * IDX_HEADS ** -0.5, **kw)
    return y.reshape(B, T, b_width)


def hgrn2_mixer(x, state, w_in, lb, gn_g, w_out):
    B, T, c_width = x.shape
    q, f_raw, i, g = split_cols(mm(x, w_in), [c_width, c_width, c_width, c_width])
    log_f = jnp.logaddexp(jnp.log(lb), jnp.log1p(-lb) + jax.nn.log_sigmoid(f_raw))
    k_in = (1.0 - lb) * jax.nn.sigmoid(-f_raw)
    y, new_state = gla_gated(jax.nn.silu(q), k_in, i, log_f, g, gn_g, state)
    return mm(y, w_out), new_state


def kernel(x_prompt, x_sample, cache_k, cache_v, cache_kidx, state_wkv, state_shift, state_hgrn, page_table, w_in_ab, rwkv_mu, rwkv_w0, rwkv_w2, rwkv_a0, rwkv_a2, rwkv_g2, rwkv_kk, rwkv_ka, rwkv_rk, rwkv_lnx_g, rwkv_lnx_b, w_out_ab, rel_bias, ffn_w_gate, ffn_w_up, ffn_w_down, w_in_c, hgrn_lb, hgrn_gnorm, w_out_c, router, moe_w_gate, moe_w_up, moe_w_down, ln1_g, ln1_b, ln2_g, ln2_b):
    depth = ln1_g.shape[0]
    alpha = (2 * depth) ** 0.25
    a_width = rwkv_w0.shape[1]
    a_heads = a_width // A_HEAD
    a_proj = state_shift.shape[-1]
    c_heads, c_dk, c_dv = state_hgrn.shape[2:]
    lb_all = jnp.cumsum(jax.nn.softmax(hgrn_lb, axis=0), axis=0)
    lb_all = lb_all - lb_all[0:1]
    attend_p = functools.partial(dsa_prompt, rel_bias=rel_bias)
    xp, xs = x_prompt, x_sample
    kp_l, vp_l, kip_l, wkvp_l, shp_l, hgp_l = [], [], [], [], [], []
    ks_l, vs_l, kis_l, wkvs_l, shs_l, hgs_l = [], [], [], [], [], []
    for l in range(depth):
        j = l // 2
        if l % 2 == 0:
            rw = (rwkv_mu[j], rwkv_w0[j], rwkv_w2[j], rwkv_a0[j], rwkv_a2[j], rwkv_g2[j],
                  rwkv_kk[j], rwkv_ka[j], rwkv_rk[j], rwkv_lnx_g[j], rwkv_lnx_b[j])
            attend_s = functools.partial(attend_sample, cache_k=cache_k, cache_v=cache_v, cache_kidx=cache_kidx,
                                         layer=j, page_table=page_table, rel_bias=rel_bias)
            shift0 = jnp.zeros((xp.shape[0], a_proj), xp.dtype)
            wkv0 = jnp.zeros((xp.shape[0], a_heads, A_HEAD, A_HEAD), jnp.float32)
            yp, shp, wkvp, kp, vp, kip = ab_mixer(xp, shift0, wkv0, attend_p, (w_in_ab, j), rw, (w_out_ab, j))
            ys, shs, wkvs, kss, vss, kis = ab_mixer(xs, state_shift[j], state_wkv[j], attend_s, (w_in_ab, j), rw, (w_out_ab, j))
            kp_l.append(kp); vp_l.append(vp); kip_l.append(kip); wkvp_l.append(wkvp); shp_l.append(shp)
            ks_l.append(kss); vs_l.append(vss); kis_l.append(kis); wkvs_l.append(wkvs); shs_l.append(shs)
        else:
            hg0 = jnp.zeros((xp.shape[0], c_heads, c_dk, c_dv), jnp.float32)
            yp, hgp = hgrn2_mixer(xp, hg0, (w_in_c, j), lb_all[j], hgrn_gnorm[j], (w_out_c, j))
            ys, hgs = hgrn2_mixer(xs, state_hgrn[j], (w_in_c, j), lb_all[j], hgrn_gnorm[j], (w_out_c, j))
            hgp_l.append(hgp); hgs_l.append(hgs)
        xp = layer_norm(alpha * xp + yp, ln1_g[l], ln1_b[l])
        xs = layer_norm(alpha * xs + ys, ln1_g[l], ln1_b[l])
        if l % 2 == 0:
            fp = swiglu(xp, (ffn_w_gate, j), (ffn_w_up, j), (ffn_w_down, j))
            fs = swiglu(xs, (ffn_w_gate, j), (ffn_w_up, j), (ffn_w_down, j))
        else:
            fp = moe_swiglu(xp, router, moe_w_gate, moe_w_up, moe_w_down, j)
            fs = moe_swiglu(xs, router, moe_w_gate, moe_w_up, moe_w_down, j)
        xp = layer_norm(alpha * xp + fp, ln2_g[l], ln2_b[l])
        xs = layer_norm(alpha * xs + fs, ln2_g[l], ln2_b[l])
    return (xp, xs, jnp.stack(kp_l), jnp.stack(vp_l), jnp.stack(kip_l), jnp.stack(wkvp_l), jnp.stack(shp_l),
            jnp.stack(hgp_l), jnp.stack(ks_l), jnp.stack(vs_l), jnp.stack(kis_l), jnp.stack(wkvs_l),
            jnp.stack(shs_l), jnp.stack(hgs_l))
```

```python
import functools
import math

import jax
import jax.numpy as jnp
import numpy as np
from jax import lax
from jax.experimental import pallas as pl
from jax.experimental.pallas import tpu as pltpu

A_HEAD = 64
W_LORA = 64
AA_LORA = 64
G_LORA = 128
RWKV_GN_EPS = 64e-5
B_HEAD = 128
IDX_HEADS = 16
IDX_DIM = 64
TOPK_MAX = 256
Q_BLOCK = 64
T5_BUCKETS = 32
T5_MAX_DIST = 128
C_EXPAND = 128
GLA_CHUNK = 64
GLA_HEAD_GROUP = 8
RMS_EPS = 1e-5
N_EXPERTS = 8
TOP_EXPERTS = 2
PAGE_SIZE = 128
LN_EPS = 1e-5

VMEM_LIMIT_BYTES = 56 * 1024 * 1024
BF16_ROWS_MIN = 256


def _pick_tile(n, target, align):
    if n <= target:
        return n
    t = (target // align) * align
    while t >= align:
        if n % t == 0:
            return t
        t -= align
    return n


def _layer(w):
    return w if isinstance(w, tuple) else (w[None], 0)


def _matmul_kernel(x_ref, w_ref, o_ref):
    @pl.when(pl.program_id(2) == 0)
    def _():
        o_ref[...] = jnp.zeros_like(o_ref)

    o_ref[...] += jnp.dot(x_ref[...].astype(jnp.bfloat16), w_ref[0].astype(jnp.bfloat16),
                          preferred_element_type=jnp.float32)


def matmul(x, w, tm=1024, tn=1024, tk=2048):
    m, k = x.shape
    w, layer = _layer(w)
    k2, n = w.shape[1:]
    assert k == k2
    tm = _pick_tile(m, tm, 8)
    tk = _pick_tile(k, tk, 128)
    tn = min(tn, n) if n % 128 == 0 or n < 128 else tn
    grid = (pl.cdiv(n, tn), pl.cdiv(m, tm), k // tk)
    return pl.pallas_call(
        _matmul_kernel,
        grid=grid,
        in_specs=[pl.BlockSpec((tm, tk), lambda j, i, kk: (i, kk)),
                  pl.BlockSpec((1, tk, tn), lambda j, i, kk: (layer, kk, j))],
        out_specs=pl.BlockSpec((tm, tn), lambda j, i, kk: (i, j)),
        out_shape=jax.ShapeDtypeStruct((m, n), jnp.float32),
        compiler_params=pltpu.CompilerParams(
            dimension_semantics=("parallel", "parallel", "arbitrary"),
            vmem_limit_bytes=VMEM_LIMIT_BYTES),
        name="matmul",
    )(x, w)


def _swiglu_up_kernel(x_ref, wg_ref, wu_ref, o_ref):
    xb = x_ref[...].astype(jnp.bfloat16)
    g = jnp.dot(xb, wg_ref[0].astype(jnp.bfloat16), preferred_element_type=jnp.float32)
    u = jnp.dot(xb, wu_ref[0].astype(jnp.bfloat16), preferred_element_type=jnp.float32)
    o_ref[...] = (g * jax.nn.sigmoid(g) * u).astype(o_ref.dtype)


def swiglu_up(x, wg, wu, tm=1024, tn=512):
    m, k = x.shape
    (wg, layer), (wu, _) = _layer(wg), _layer(wu)
    n = wg.shape[2]
    tm = _pick_tile(m, tm, 8)
    tn = _pick_tile(n, tn, 128)
    return pl.pallas_call(
        _swiglu_up_kernel,
        grid=(n // tn, m // tm),
        in_specs=[pl.BlockSpec((tm, k), lambda j, i: (i, 0)),
                  pl.BlockSpec((1, k, tn), lambda j, i: (layer, 0, j)),
                  pl.BlockSpec((1, k, tn), lambda j, i: (layer, 0, j))],
        out_specs=pl.BlockSpec((tm, tn), lambda j, i: (i, j)),
        out_shape=jax.ShapeDtypeStruct((m, n), x.dtype),
        compiler_params=pltpu.CompilerParams(
            dimension_semantics=("parallel", "parallel"),
            vmem_limit_bytes=VMEM_LIMIT_BYTES),
        name="swiglu_up",
    )(x, wg, wu)


def _mxu_operand(x):
    x = x.reshape(-1, x.shape[-1])
    return x.astype(jnp.bfloat16) if x.shape[0] >= BF16_ROWS_MIN else x


def mm(x, w):
    return matmul(_mxu_operand(x), w).reshape(x.shape[:-1] + (_layer(w)[0].shape[2],))


def swiglu(x, wg, wu, wd):
    h = swiglu_up(_mxu_operand(x), wg, wu)
    return matmul(h, wd).reshape(x.shape[:-1] + (_layer(wd)[0].shape[2],))


INT_MIN = -2 ** 31
NEG_BIG = -1e30


def _sortable_key(x):
    bits = pltpu.bitcast(jnp.where(x == 0.0, 0.0, x), jnp.int32)
    return bits ^ ((bits >> 31) & 0x7FFFFFFF)


def _dsa_prompt_kernel(qt_ref, kt_ref, qi_ref, ki_ref, wit_ref, q_ref, k_ref, v_ref, bias_ref, o_ref,
                       keys_ref, tau_ref, qib_ref, qb_ref, m_ref, l_ref, acc_ref, *, topk, n_idx_bits):
    i = qt_ref[pl.program_id(1)]
    j = kt_ref[pl.program_id(1)]
    tq = q_ref.shape[1]
    n_heads = q_ref.shape[2] // B_HEAD
    kf = float(topk)

    def key_sum(x):
        part = x[:SUBLANES]
        for u in range(1, tq // SUBLANES):
            part = part + x[u * SUBLANES:(u + 1) * SUBLANES]
        return part

    @pl.when(j == 0)
    def _select():
        qib_ref[...] = (qi_ref[0] * IDX_DIM ** -0.5).astype(jnp.bfloat16)
        qb_ref[...] = q_ref[0].astype(jnp.bfloat16)
        m_ref[...] = jnp.full_like(m_ref, NEG_BIG)
        l_ref[...] = jnp.zeros_like(l_ref)
        acc_ref[...] = jnp.zeros_like(acc_ref)
        wb = (wit_ref[0] * IDX_HEADS ** -0.5).astype(jnp.bfloat16).astype(jnp.float32)
        key_pos = lax.broadcasted_iota(jnp.int32, (tq, tq), 0)
        qry_pos = i * tq + lax.broadcasted_iota(jnp.int32, (tq, tq), 1)

        def score_chunk(c, carry):
            kc = ki_ref[0, pl.ds(pl.multiple_of(c * tq, tq), tq), :]
            score = jnp.zeros((tq, tq), jnp.float32)
            for h in range(IDX_HEADS):
                s = _dot_nt(kc, qib_ref[:, h * IDX_DIM:(h + 1) * IDX_DIM])
                r = jnp.maximum(s, 0.0).astype(jnp.bfloat16).astype(jnp.float32)
                score = score + wb[h:h + 1, :] * r
            keys_ref[c] = jnp.where(c * tq + key_pos <= qry_pos, _sortable_key(score), INT_MIN)
            return carry

        lax.fori_loop(0, i + 1, score_chunk, 0)

        def count(pred):
            def body(c, acc):
                return acc + key_sum(jnp.where(pred(keys_ref[c], c), 1.0, 0.0))
            acc = lax.fori_loop(0, i + 1, body, jnp.zeros((SUBLANES, tq), jnp.float32))
            return jnp.sum(acc, axis=0, keepdims=True)

        tau = jnp.where(count(lambda kc, c: kc >= 0) >= kf, 0, INT_MIN).astype(jnp.int32)

        def bit_step(n, tau):
            cand = tau | jnp.left_shift(jnp.int32(1), 30 - n)
            return jnp.where(count(lambda kc, c: kc >= cand) >= kf, cand, tau)

        tau = lax.fori_loop(0, 31, bit_step, tau)
        tau = jnp.maximum(tau, INT_MIN + 1)
        tau_ref[...] = tau
        n_ge = count(lambda kc, c: kc >= tau)

        @pl.when(jnp.max(n_ge) > kf)
        def _ties():
            need = kf - count(lambda kc, c: kc > tau)

            def idx_step(n, ans):
                cand = ans + jnp.left_shift(jnp.int32(1), n_idx_bits - 1 - n)
                f = count(lambda kc, c: (kc == tau) & (c * tq + key_pos < cand))
                return jnp.where(f < need, cand, ans)

            last = lax.fori_loop(0, n_idx_bits, idx_step, jnp.zeros((1, tq), jnp.int32))

            def demote(c, carry):
                kc = keys_ref[c]
                keys_ref[c] = jnp.where((kc == tau) & (c * tq + key_pos > last), INT_MIN, kc)
                return carry

            lax.fori_loop(0, i + 1, demote, 0)

    sel = keys_ref[j] >= tau_ref[...]
    heads = [slice(h * B_HEAD, (h + 1) * B_HEAD) for h in range(n_heads)]
    raw = [_dot_nt(k_ref[0, :, hs].astype(jnp.bfloat16), qb_ref[:, hs]) for hs in heads]
    probs, alphas = [], []
    for h in range(n_heads):
        logits = jnp.where(sel, raw[h] * B_HEAD ** -0.5 + bias_ref[0, h], NEG_BIG)
        m_old = m_ref[h]
        m_new = jnp.maximum(m_old, jnp.max(logits, axis=0, keepdims=True))
        alpha = jnp.exp(m_old - m_new)
        p = jnp.exp(logits - m_new)
        l_ref[h] = alpha * l_ref[h] + jnp.sum(p, axis=0, keepdims=True)
        m_ref[h] = m_new
        probs.append(p.astype(jnp.bfloat16))
        alphas.append(alpha)
    for h, hs in enumerate(heads):
        acc_ref[h] = alphas[h] * acc_ref[h] + lax.dot_general(
            v_ref[0, :, hs].astype(jnp.bfloat16), probs[h], (((0,), (0,)), ((), ())),
            preferred_element_type=jnp.float32)

    @pl.when(j == i)
    def _finish():
        for h in range(n_heads):
            o_ref[0, :, h * B_HEAD:(h + 1) * B_HEAD] = jnp.transpose(acc_ref[h] / l_ref[h])


def dsa_prompt(q, k, v, qi, ki, wi, rel_bias, tq=256):
    B, T, b_width = q.shape
    n_heads = b_width // B_HEAD
    tq = min(tq, T)
    assert T % tq == 0 and tq % 128 == 0 and tq + 1 >= T5_MAX_DIST
    nq = T // tq
    topk = min(TOPK_MAX, T // 4)
    r = jnp.arange(tq, dtype=jnp.int32)
    dist = jnp.arange(3, dtype=jnp.int32)[:, None, None] * tq + r[None, None, :] - r[None, :, None]
    bias3 = jnp.einsum('dkqc,ch->dhkq', jax.nn.one_hot(t5_bucket(dist), T5_BUCKETS, dtype=jnp.float32), rel_bias,
                       precision=lax.Precision.HIGHEST)
    kern = functools.partial(_dsa_prompt_kernel, topk=topk, n_idx_bits=max(1, (T - 1).bit_length()))
    pairs = np.array([(i, j) for i in range(nq) for j in range(i + 1)], dtype=np.int32)
    q_tile, k_tile = jnp.asarray(pairs[:, 0]), jnp.asarray(pairs[:, 1])
    return pl.pallas_call(
        kern,
        grid_spec=pltpu.PrefetchScalarGridSpec(
            num_scalar_prefetch=2, grid=(B, len(pairs)),
            in_specs=[pl.BlockSpec((1, tq, qi.shape[2]), lambda b, p, qt, kt: (b, qt[p], 0)),
                      pl.BlockSpec((1, T, IDX_DIM), lambda b, p, qt, kt: (b, 0, 0)),
                      pl.BlockSpec((1, IDX_HEADS, tq), lambda b, p, qt, kt: (b, 0, qt[p])),
                      pl.BlockSpec((1, tq, b_width), lambda b, p, qt, kt: (b, qt[p], 0)),
                      pl.BlockSpec((1, tq, b_width), lambda b, p, qt, kt: (b, kt[p], 0)),
                      pl.BlockSpec((1, tq, b_width), lambda b, p, qt, kt: (b, kt[p], 0)),
                      pl.BlockSpec((1, n_heads, tq, tq),
                                   lambda b, p, qt, kt: (jnp.minimum(qt[p] - kt[p], 2), 0, 0, 0))],
            out_specs=pl.BlockSpec((1, tq, b_width), lambda b, p, qt, kt: (b, qt[p], 0)),
            scratch_shapes=[pltpu.VMEM((nq, tq, tq), jnp.int32),
                            pltpu.VMEM((1, tq), jnp.int32),
                            pltpu.VMEM((tq, qi.shape[2]), jnp.bfloat16),
                            pltpu.VMEM((tq, b_width), jnp.bfloat16),
                            pltpu.VMEM((n_heads, 1, tq), jnp.float32),
                            pltpu.VMEM((n_heads, 1, tq), jnp.float32),
                            pltpu.VMEM((n_heads, B_HEAD, tq), jnp.float32)]),
        out_shape=jax.ShapeDtypeStruct((B, T, b_width), jnp.float32),
        compiler_params=pltpu.CompilerParams(
            dimension_semantics=("parallel", "arbitrary"),
            vmem_limit_bytes=VMEM_LIMIT_BYTES),
        name="dsa_prompt",
    )(q_tile, k_tile, qi, ki.astype(jnp.bfloat16), jnp.swapaxes(wi, 1, 2), q, k, v, bias3)


SCAN_CHUNK = A_HEAD
LANES = 128
SUBLANES = 8


def _rwkv_scan_kernel(vt_ref, kk_ref, kka_ref, w_ref, km_ref, r_ref, s0_ref, ot_ref, st_ref, s_scr, *, n_steps):
    c = pl.program_id(0)
    n_batch = vt_ref.shape[0]
    n_tiles = vt_ref.shape[3] // LANES
    bf = jnp.bfloat16

    @pl.when(c == 0)
    def _():
        s_scr[...] = s0_ref[...]

    if n_steps < SCAN_CHUNK:
        ot_ref[...] = jnp.zeros_like(ot_ref)
    lane = lax.broadcasted_iota(jnp.int32, (A_HEAD, LANES), 1)
    seg_base = lane & (LANES - A_HEAD)
    lane_t = lane & (A_HEAD - 1)
    seg_ones = ((lax.broadcasted_iota(jnp.int32, (LANES, LANES), 0) // A_HEAD)
                == (lax.broadcasted_iota(jnp.int32, (LANES, LANES), 1) // A_HEAD)).astype(bf)

    group = min(n_steps, SUBLANES)
    tiles = [(b, slice(p * LANES, (p + 1) * LANES)) for b in range(n_batch) for p in range(n_tiles)]

    def seg_sums(xs):
        lhs = jnp.concatenate([x.astype(bf) for x in xs], axis=0)
        out = jnp.dot(lhs, seg_ones, preferred_element_type=jnp.float32)
        return [out[i * A_HEAD:(i + 1) * A_HEAD] for i in range(len(xs))]

    def step_group(g, carry):
        t0 = g * group if isinstance(g, int) else pl.multiple_of(g * group, group)
        rows = [[ref[b, pl.ds(t0, group), ls] for (b, ls) in tiles] for ref in (kk_ref, kka_ref, w_ref, km_ref, r_ref)]
        for u in range(group):
            kk_r, kka_r, w_r, km_r, r_r = ([x[u:u + 1] for x in q] for q in rows)
            s_kk = seg_sums([s_scr[b, :, ls] * kk_r[i] for i, (b, ls) in enumerate(tiles)])
            s_new = []
            for i, (b, ls) in enumerate(tiles):
                v_col = jnp.take_along_axis(vt_ref[b, 0, :, ls], seg_base + (t0 + u), axis=1)
                s = s_scr[b, :, ls] * w_r[i] - s_kk[i] * kka_r[i] + v_col * km_r[i]
                s_scr[b, :, ls] = s
                s_new.append(s * r_r[i])
            o_col = seg_sums(s_new)
            for i, (b, ls) in enumerate(tiles):
                ot_ref[b, 0, :, ls] = jnp.where(lane_t == t0 + u, o_col[i], ot_ref[b, 0, :, ls])
        return carry

    if n_steps == group:
        step_group(0, 0)
    else:
        lax.fori_loop(0, n_steps // group, step_group, 0)

    @pl.when(c == pl.num_programs(0) - 1)
    def _():
        st_ref[...] = s_scr[...]


def _rwkv_step_kernel(s_ref, kk_ref, kka_ref, w_ref, km_ref, r_ref, v_ref, o_ref, st_ref):
    rnd = lambda x: x.astype(jnp.bfloat16).astype(jnp.float32)
    s = s_ref[...]
    s_kk = jnp.sum(rnd(s) * rnd(kk_ref[...]), axis=-1, keepdims=True)
    s = s * w_ref[...] - s_kk * kka_ref[...] + v_ref[...] * km_ref[...]
    o_ref[...] = jnp.sum(rnd(s) * rnd(r_ref[...]), axis=-1, keepdims=True)
    st_ref[...] = s


def rwkv_step(r, w, k, v, kk, a, state, groups_per_block=16):
    B, _, width = r.shape
    n_heads = width // A_HEAD
    n_groups = B * n_heads
    gb = _pick_tile(n_groups, groups_per_block, 1)
    row = lambda x: x.reshape(n_groups, 1, A_HEAD)
    row_spec = pl.BlockSpec((gb, 1, A_HEAD), lambda g: (g, 0, 0))
    col_spec = pl.BlockSpec((gb, A_HEAD, 1), lambda g: (g, 0, 0))
    st_spec = pl.BlockSpec((gb, A_HEAD, A_HEAD), lambda g: (g, 0, 0))
    o, st = pl.pallas_call(
        _rwkv_step_kernel,
        grid=(n_groups // gb,),
        in_specs=[st_spec] + [row_spec] * 5 + [col_spec],
        out_specs=[col_spec, st_spec],
        out_shape=[jax.ShapeDtypeStruct((n_groups, A_HEAD, 1), jnp.float32),
                   jax.ShapeDtypeStruct((n_groups, A_HEAD, A_HEAD), jnp.float32)],
        compiler_params=pltpu.CompilerParams(dimension_semantics=("parallel",), vmem_limit_bytes=VMEM_LIMIT_BYTES),
        name="rwkv_step",
    )(state.reshape(n_groups, A_HEAD, A_HEAD), row(kk), row(kk * a), row(w), row(k), row(r),
      v.reshape(n_groups, A_HEAD, 1))
    return o.reshape(B, 1, width), st.reshape(state.shape)


def rwkv_scan(r, w, k, v, kk, a, state):
    B, T, width = r.shape
    n_heads = width // A_HEAD
    if T == 1:
        return rwkv_step(r, w, k, v, kk, a, state)
    assert SCAN_CHUNK == A_HEAD and (T % SCAN_CHUNK == 0 or T < SCAN_CHUNK)
    n_chunks = pl.cdiv(T, SCAN_CHUNK)
    t_pad = n_chunks * SCAN_CHUNK
    pad = lambda x: jnp.pad(x, ((0, 0), (0, t_pad - T), (0, 0)))
    vt = jnp.transpose(pad(v).reshape(B, n_chunks, SCAN_CHUNK, n_heads, A_HEAD), (0, 1, 4, 3, 2))
    vt = vt.reshape(B, n_chunks, A_HEAD, width)
    st0 = jnp.transpose(state, (0, 2, 1, 3)).reshape(B, A_HEAD, width)
    row_spec = pl.BlockSpec((B, SCAN_CHUNK, width), lambda c: (0, c, 0))
    col_spec = pl.BlockSpec((B, 1, A_HEAD, width), lambda c: (0, c, 0, 0))
    st_spec = pl.BlockSpec((B, A_HEAD, width), lambda c: (0, 0, 0))
    ot, st = pl.pallas_call(
        functools.partial(_rwkv_scan_kernel, n_steps=min(T, SCAN_CHUNK)),
        grid=(n_chunks,),
        in_specs=[col_spec] + [row_spec] * 5 + [st_spec],
        out_specs=[col_spec, st_spec],
        out_shape=[jax.ShapeDtypeStruct((B, n_chunks, A_HEAD, width), jnp.float32),
                   jax.ShapeDtypeStruct((B, A_HEAD, width), jnp.float32)],
        scratch_shapes=[pltpu.VMEM((B, A_HEAD, width), jnp.float32)],
        compiler_params=pltpu.CompilerParams(
            dimension_semantics=("arbitrary",),
            vmem_limit_bytes=VMEM_LIMIT_BYTES),
        name="rwkv_scan",
    )(vt, pad(kk), pad(kk * a), pad(w), pad(k), pad(r), st0)
    o = jnp.transpose(ot.reshape(B, n_chunks, A_HEAD, n_heads, SCAN_CHUNK), (0, 1, 4, 3, 2)).reshape(B, t_pad, width)
    new_state = jnp.transpose(st.reshape(B, A_HEAD, n_heads, A_HEAD), (0, 2, 1, 3))
    return o[:, :T], new_state


def _split2(x):
    hi = x.astype(jnp.bfloat16)
    return hi, (x - hi.astype(jnp.float32)).astype(jnp.bfloat16)


def _dot_nt_split(a, b):
    a_hi, a_lo = _split2(a)
    b_hi, b_lo = _split2(b)
    return _dot_nt(jnp.concatenate([a_hi, a_hi, a_lo], axis=1), jnp.concatenate([b_hi, b_lo, b_hi], axis=1))


def _gla_segment_matrix(chunk):
    t = np.arange(chunk)[:, None]
    u = np.arange(chunk)[None, :]
    rows = []
    for lvl in range(chunk.bit_length() - 1):
        m = (chunk // 2) >> lvl
        blk = t // m
        right = np.logical_and(u >= blk * m, u <= t)
        left = np.logical_and(u > t, u < (blk + 1) * m)
        rows.append(np.where(blk % 2 == 1, right, left))
    rows.append(u <= t)
    rows.append(u > t)
    seg = np.concatenate(rows, axis=0).astype(np.float32)
    return np.concatenate([seg, seg, seg], axis=1)


def _dot_nt(a, b, **kw):
    return lax.dot_general(a, b, (((1,), (1,)), ((), ())), preferred_element_type=jnp.float32, **kw)


def _gla_kernel(seg_ref, q_ref, k_ref, v_ref, lf_ref, g_ref, gn_ref, s0_ref, y_ref, st_ref, s_scr):
    c = pl.program_id(1)
    chunk = q_ref.shape[1]
    n_levels = chunk.bit_length() - 1
    n_heads = s_scr.shape[0]
    bf = jnp.bfloat16

    @pl.when(c == 0)
    def _():
        s_scr[...] = s0_ref[0]

    row = lax.broadcasted_iota(jnp.int32, (chunk, chunk), 0)
    col = lax.broadcasted_iota(jnp.int32, (chunk, chunk), 1)
    row_w = lax.broadcasted_iota(jnp.int32, (chunk, C_EXPAND), 0)
    seg = seg_ref[...]
    gn = gn_ref[...]

    def head_group(hg, carry):
        hs = [hg * GLA_HEAD_GROUP + u for u in range(GLA_HEAD_GROUP)]
        lanes = [pl.ds(pl.multiple_of(h * C_EXPAND, C_EXPAND), C_EXPAND) for h in hs]
        qs = [q_ref[0, :, ls] for ls in lanes]
        ks = [k_ref[0, :, ls] for ls in lanes]
        es = []
        for ls in lanes:
            lf = lf_ref[0, :, ls]
            lf_hi = lf.astype(bf)
            lf_r = lf - lf_hi.astype(jnp.float32)
            lf_mid = lf_r.astype(bf)
            lf_lo = (lf_r - lf_mid.astype(jnp.float32)).astype(bf)
            es.append(jnp.exp(jnp.dot(seg, jnp.concatenate([lf_hi, lf_mid, lf_lo], axis=0),
                                      preferred_element_type=jnp.float32)))
        grams = []
        for q, k, e in zip(qs, ks, es):
            g_h = [_dot_nt_split(q, k)]
            for lvl in range(n_levels):
                sh = n_levels - 1 - lvl
                x = jnp.where(((row_w >> sh) & 1) == 1, q, k) * e[lvl * chunk:(lvl + 1) * chunk]
                g_h.append(_dot_nt_split(x, x))
            grams.append(g_h)
        outs = []
        for h, ls, q, k, e, g_h in zip(hs, lanes, qs, ks, es, grams):
            a = jnp.where(row == col, g_h[0], 0.0)
            for lvl in range(n_levels):
                sh = n_levels - 1 - lvl
                pair = (((row >> sh) & 1) == 1) & ((col >> sh) == (row >> sh) - 1)
                a = a + jnp.where(pair, g_h[lvl + 1], 0.0)
            e_b = e[n_levels * chunk:(n_levels + 1) * chunk]
            e_r = e[(n_levels + 1) * chunk:]
            v = v_ref[0, :, ls].astype(bf)
            st = s_scr[h]
            o = jnp.dot(a.astype(bf), v, preferred_element_type=jnp.float32)
            outs.append(o + _dot_nt((q * e_b).astype(bf), st.astype(bf)))
            s_scr[h] = st * e_b[chunk - 1:chunk, :] + lax.dot_general(
                v, (k * e_r).astype(bf), (((0,), (0,)), ((), ())), preferred_element_type=jnp.float32)
        for ls, o in zip(lanes, outs):
            o = o * lax.rsqrt(jnp.mean(o * o, axis=1, keepdims=True) + RMS_EPS) * gn
            gate = g_ref[0, :, ls]
            y_ref[0, :, ls] = o * (gate * jax.nn.sigmoid(gate))
        return carry

    lax.fori_loop(0, n_heads // GLA_HEAD_GROUP, head_group, 0)

    @pl.when(c == pl.num_programs(1) - 1)
    def _():
        st_ref[0] = s_scr[...]


def _gla_step_kernel(st_ref, q_ref, k_ref, lf_ref, v_ref, g_ref, gn_ref, y_ref, so_ref):
    rnd = lambda x: x.astype(jnp.bfloat16).astype(jnp.float32)
    q, k, v = q_ref[...], k_ref[...], v_ref[...]
    decay = jnp.exp(lf_ref[...])
    st = st_ref[...]
    o = jnp.sum(q * k, axis=-1, keepdims=True) * v + jnp.sum(rnd(q * decay) * rnd(st), axis=-1, keepdims=True)
    so_ref[...] = st * decay + v * k
    o = o * lax.rsqrt(jnp.mean(o * o, axis=1, keepdims=True) + RMS_EPS) * gn_ref[...]
    gate = g_ref[...]
    y_ref[...] = o * (gate * jax.nn.sigmoid(gate))


def gla_step(q, k, v, log_f, g, gn_g, state, groups_per_block=16):
    B, _, width = q.shape
    n_heads = width // C_EXPAND
    n_groups = B * n_heads
    gb = _pick_tile(n_groups, groups_per_block, 1)
    row = lambda x: x.reshape(n_groups, 1, C_EXPAND)
    col = lambda x: x.reshape(n_groups, C_EXPAND, 1)
    row_spec = pl.BlockSpec((gb, 1, C_EXPAND), lambda i: (i, 0, 0))
    col_spec = pl.BlockSpec((gb, C_EXPAND, 1), lambda i: (i, 0, 0))
    st_spec = pl.BlockSpec((gb, C_EXPAND, C_EXPAND), lambda i: (i, 0, 0))
    y, st = pl.pallas_call(
        _gla_step_kernel,
        grid=(n_groups // gb,),
        in_specs=[st_spec, row_spec, row_spec, row_spec, col_spec, col_spec,
                  pl.BlockSpec((1, C_EXPAND, 1), lambda i: (0, 0, 0))],
        out_specs=[col_spec, st_spec],
        out_shape=[jax.ShapeDtypeStruct((n_groups, C_EXPAND, 1), jnp.float32),
                   jax.ShapeDtypeStruct((n_groups, C_EXPAND, C_EXPAND), jnp.float32)],
        compiler_params=pltpu.CompilerParams(dimension_semantics=("parallel",), vmem_limit_bytes=VMEM_LIMIT_BYTES),
        name="gla_step",
    )(jnp.swapaxes(state, 2, 3).reshape(n_groups, C_EXPAND, C_EXPAND), row(q), row(k), row(log_f), col(v), col(g),
      gn_g.reshape(1, C_EXPAND, 1))
    return y.reshape(B, 1, width), jnp.swapaxes(st.reshape(B, n_heads, C_EXPAND, C_EXPAND), 2, 3)


def gla_gated(q, k, v, log_f, g, gn_g, state):
    B, T, width = q.shape
    n_heads = width // C_EXPAND
    assert v.shape[2] == width, "kernel assumes dk == dv == C_EXPAND"
    if T == 1:
        return gla_step(q, k, v, log_f, g, gn_g, state)
    n_chunks = pl.cdiv(T, GLA_CHUNK)
    t_pad = n_chunks * GLA_CHUNK
    pad = lambda x: jnp.pad(x, ((0, 0), (0, t_pad - T), (0, 0)))
    seg = jnp.asarray(_gla_segment_matrix(GLA_CHUNK), dtype=jnp.bfloat16)
    st0 = jnp.swapaxes(state, 2, 3)
    tok_spec = pl.BlockSpec((1, GLA_CHUNK, width), lambda b, c: (b, c, 0))
    st_spec = pl.BlockSpec((1, n_heads, C_EXPAND, C_EXPAND), lambda b, c: (b, 0, 0, 0))
    y, st = pl.pallas_call(
        _gla_kernel,
        grid=(B, n_chunks),
        in_specs=[pl.BlockSpec(seg.shape, lambda b, c: (0, 0))] + [tok_spec] * 5
                 + [pl.BlockSpec((1, C_EXPAND), lambda b, c: (0, 0)), st_spec],
        out_specs=[tok_spec, st_spec],
        out_shape=[jax.ShapeDtypeStruct((B, t_pad, width), jnp.float32),
                   jax.ShapeDtypeStruct(st0.shape, jnp.float32)],
        scratch_shapes=[pltpu.VMEM((n_heads, C_EXPAND, C_EXPAND), jnp.float32)],
        compiler_params=pltpu.CompilerParams(
            dimension_semantics=("parallel", "arbitrary"),
            vmem_limit_bytes=VMEM_LIMIT_BYTES),
        name="gla_gated",
    )(seg, pad(q), pad(k), pad(v), pad(log_f), pad(g), gn_g.reshape(1, C_EXPAND), st0)
    return y[:, :T], jnp.swapaxes(st, 2, 3)


def _moe_up_kernel(te_ref, nt_ref, x_ref, wg_ref, wu_ref, o_ref):
    del te_ref
    used = pl.program_id(1) < nt_ref[0]

    @pl.when(used)
    def _():
        xb = x_ref[...].astype(jnp.bfloat16)
        g = jnp.dot(xb, wg_ref[0, 0].astype(jnp.bfloat16), preferred_element_type=jnp.float32)
        u = jnp.dot(xb, wu_ref[0, 0].astype(jnp.bfloat16), preferred_element_type=jnp.float32)
        o_ref[...] = (g * jax.nn.sigmoid(g) * u).astype(o_ref.dtype)

    @pl.when(jnp.logical_not(used))
    def _():
        o_ref[...] = jnp.zeros_like(o_ref)


def _moe_down_kernel(te_ref, nt_ref, h_ref, wd_ref, o_ref):
    del te_ref
    used = pl.program_id(1) < nt_ref[0]

    @pl.when(used)
    def _():
        o_ref[...] = jnp.dot(h_ref[...].astype(jnp.bfloat16), wd_ref[0, 0].astype(jnp.bfloat16),
                             preferred_element_type=jnp.float32)

    @pl.when(jnp.logical_not(used))
    def _():
        o_ref[...] = jnp.zeros_like(o_ref)


def moe_experts(tile_expert, n_tiles_used, xs, wg, wu, wd, layer, tm, tf=704, tn=1024):
    rows, d_model = xs.shape
    d_ff = wg.shape[3]
    tf = _pick_tile(d_ff, tf, 128)
    tn = _pick_tile(d_model, tn, 128)
    params = pltpu.CompilerParams(dimension_semantics=("parallel", "arbitrary"), vmem_limit_bytes=VMEM_LIMIT_BYTES)
    h = pl.pallas_call(
        _moe_up_kernel,
        grid_spec=pltpu.PrefetchScalarGridSpec(
            num_scalar_prefetch=2, grid=(d_ff // tf, rows // tm),
            in_specs=[pl.BlockSpec((tm, d_model), lambda j, i, te, nt: (i, 0)),
                      pl.BlockSpec((1, 1, d_model, tf), lambda j, i, te, nt: (layer, te[i], 0, j)),
                      pl.BlockSpec((1, 1, d_model, tf), lambda j, i, te, nt: (layer, te[i], 0, j))],
            out_specs=pl.BlockSpec((tm, tf), lambda j, i, te, nt: (i, j))),
        out_shape=jax.ShapeDtypeStruct((rows, d_ff), jnp.bfloat16 if rows >= BF16_ROWS_MIN else jnp.float32),
        compiler_params=params, name="moe_up",
    )(tile_expert, n_tiles_used, xs, wg, wu)
    return pl.pallas_call(
        _moe_down_kernel,
        grid_spec=pltpu.PrefetchScalarGridSpec(
            num_scalar_prefetch=2, grid=(d_model // tn, rows // tm),
            in_specs=[pl.BlockSpec((tm, d_ff), lambda j, i, te, nt: (i, 0)),
                      pl.BlockSpec((1, 1, d_ff, tn), lambda j, i, te, nt: (layer, te[i], 0, j))],
            out_specs=pl.BlockSpec((tm, tn), lambda j, i, te, nt: (i, j))),
        out_shape=jax.ShapeDtypeStruct((rows, d_model), jnp.float32),
        compiler_params=params, name="moe_down",
    )(tile_expert, n_tiles_used, h, wd)


def moe_swiglu(x, w_router, wg, wu, wd, layer):
    lead, d_model = x.shape[:-1], x.shape[-1]
    x = x.reshape(-1, d_model)
    n_tok = x.shape[0]
    n_pairs = n_tok * TOP_EXPERTS
    tm = min(512, n_pairs)
    assert n_pairs % tm == 0
    probs = jax.nn.softmax(matmul(x, (w_router, layer)), axis=-1)
    top_p, top_i = lax.top_k(probs, TOP_EXPERTS)
    top_p = top_p / jnp.sum(top_p, axis=-1, keepdims=True)
    pair_expert = top_i.reshape(-1)
    member = jax.nn.one_hot(pair_expert, N_EXPERTS, dtype=jnp.int32)
    seen = jnp.cumsum(member, axis=0)
    rank = jnp.sum(member * seen, axis=1) - 1
    counts = seen[-1]
    padded = (counts + tm - 1) // tm * tm
    padded_end = jnp.cumsum(padded)
    pair_row = ((padded_end - padded)[pair_expert] + rank).astype(jnp.int32)
    rows = n_pairs + N_EXPERTS * tm
    row_token = jnp.zeros((rows,), jnp.int32).at[pair_row].set(jnp.arange(n_pairs, dtype=jnp.int32) // TOP_EXPERTS)
    tile_start = jnp.arange(rows // tm, dtype=jnp.int32) * tm
    tile_expert = jnp.minimum(jnp.searchsorted(padded_end, tile_start, side='right'), N_EXPERTS - 1).astype(jnp.int32)
    n_tiles_used = (padded_end[-1:] // tm).astype(jnp.int32)
    y_rows = moe_experts(tile_expert, n_tiles_used, x[row_token], wg, wu, wd, layer, tm)
    pair_row = pair_row.reshape(n_tok, TOP_EXPERTS)
    y = sum(top_p[:, s:s + 1] * y_rows[pair_row[:, s]] for s in range(TOP_EXPERTS))
    return y.reshape(lead + (d_model,))


def split_cols(a, sizes):
    return jnp.split(a, np.cumsum(sizes)[:-1].tolist(), axis=-1)


def gather_rows(a, idx):
    return jax.vmap(lambda a_b, i_b: a_b[i_b])(a, idx)


def layer_norm(x, g, b):
    mu = jnp.mean(x, axis=-1, keepdims=True)
    var = jnp.mean(jnp.square(x - mu), axis=-1, keepdims=True)
    return (x - mu) * lax.rsqrt(var + LN_EPS) * g + b


def t5_bucket(dist):
    dist = jnp.maximum(dist, 0)
    max_exact = T5_BUCKETS // 2
    d = jnp.maximum(dist, 1).astype(jnp.float32)
    large = max_exact + (jnp.log(d / max_exact) / math.log(T5_MAX_DIST / max_exact) * (T5_BUCKETS - max_exact)).astype(jnp.int32)
    large = jnp.minimum(large, T5_BUCKETS - 1)
    return jnp.where(dist < max_exact, dist, large)


def dsa_attend(q, qi, wi, q_pos, k_idx, gather_kv, rel_bias, topk):
    B, T = q.shape[:2]
    L = k_idx.shape[1]
    blk = min(Q_BLOCK, T)
    nb = -(-T // blk)
    pad = nb * blk - T

    def to_blocks(a):
        a = jnp.pad(a, [(0, 0), (0, pad)] + [(0, 0)] * (a.ndim - 2))
        return jnp.moveaxis(a.reshape((B, nb, blk) + a.shape[2:]), 1, 0)

    pos_b = jnp.pad(q_pos, (0, pad)).reshape(nb, blk)
    k_pos = jnp.arange(L, dtype=jnp.int32)

    def one_block(args):
        qb, qib, wib, pb = args
        s = jnp.einsum('bthd,bsd->bths', qib, k_idx) * IDX_DIM ** -0.5
        score = jnp.einsum('bth,bths->bts', wib, jax.nn.relu(s))
        score = jnp.where(k_pos[None, None, :] <= pb[None, :, None], score, -jnp.inf)
        _, sel = lax.top_k(score, topk)
        valid = sel <= pb[None, :, None]
        k_sel, v_sel = gather_kv(sel)
        logits = jnp.einsum('bthd,btkhd->bhtk', qb, k_sel) * B_HEAD ** -0.5
        bias = rel_bias[t5_bucket(pb[None, :, None] - sel)]
        logits = logits + jnp.transpose(bias, (0, 3, 1, 2))
        logits = jnp.where(valid[:, None], logits, -jnp.inf)
        p = jax.nn.softmax(logits, axis=-1)
        return jnp.einsum('bhtk,btkhd->bthd', p, v_sel)

    out = lax.map(one_block, (to_blocks(q), to_blocks(qi), to_blocks(wi), pos_b))
    return jnp.moveaxis(out, 0, 1).reshape((B, nb * blk) + q.shape[2:])[:, :T]


def attend_paged(q, k, v, qi, ki, wi, cache_k, cache_v, cache_kidx, layer, page_table, rel_bias):
    B, T = q.shape[:2]
    past = page_table.shape[1] * PAGE_SIZE
    pages = page_table + layer * cache_k.shape[1]
    kidx_past = cache_kidx.reshape((-1,) + cache_kidx.shape[2:])[pages].reshape(B, past, IDX_DIM)
    kidx_all = jnp.concatenate([kidx_past, ki], axis=1)
    flat_k = cache_k.reshape((-1,) + cache_k.shape[3:])
    flat_v = cache_v.reshape((-1,) + cache_v.shape[3:])

    def gather(sel):
        ps = jnp.minimum(sel, past - 1)
        row = gather_rows(pages, ps // PAGE_SIZE) * PAGE_SIZE + ps % PAGE_SIZE
        is_new = (sel >= past)[..., None, None]
        ns = jnp.clip(sel - past, 0, T - 1)
        k_sel = jnp.where(is_new, gather_rows(k, ns), flat_k[row])
        v_sel = jnp.where(is_new, gather_rows(v, ns), flat_v[row])
        return k_sel, v_sel

    q_pos = past + jnp.arange(T, dtype=jnp.int32)
    return dsa_attend(q, qi, wi, q_pos, kidx_all, gather, rel_bias, min(TOPK_MAX, (past + T) // 4))


def rwkv7_mix(p_a, shift_state, wkv_state, mu, w0, w2, a0, a2, g2, kk_s, ka_s, rk_s, lnx_g, lnx_b):
    B, T, _ = p_a.shape
    a_width = w0.shape[0]
    a_heads = a_width // A_HEAD
    prev = jnp.concatenate([shift_state[:, None, :], p_a[:, :-1]], axis=1)
    xm = p_a + (prev - p_a) * mu
    r, k, v, wd, ad, gd = split_cols(xm, [a_width, a_width, a_width, W_LORA, AA_LORA, G_LORA])
    w_pre = w0 + mm(jnp.tanh(wd), w2)
    decay = jnp.exp(-jnp.exp(-jax.nn.softplus(-w_pre) - 0.5))
    a = jax.nn.sigmoid(a0 + mm(ad, a2))
    g = mm(jax.nn.sigmoid(gd), g2)
    heads = lambda t: t.reshape(B, T, a_heads, A_HEAD)
    kk = heads(k * kk_s)
    kk = kk / jnp.maximum(jnp.sqrt(jnp.sum(jnp.square(kk), axis=-1, keepdims=True)), 1e-12)
    k_mod = heads(k * (1.0 + (a - 1.0) * ka_s))
    r_h, v_h = heads(r), heads(v)
    o, S = rwkv_scan(r, decay, k_mod.reshape(B, T, a_width), v, kk.reshape(B, T, a_width), a, wkv_state)
    o = heads(o)
    mu_o = jnp.mean(o, axis=-1, keepdims=True)
    var_o = jnp.mean(jnp.square(o - mu_o), axis=-1, keepdims=True)
    o = ((o - mu_o) * lax.rsqrt(var_o + RWKV_GN_EPS)).reshape(B, T, a_width) * lnx_g + lnx_b
    bonus = jnp.sum(r_h * k_mod * rk_s.reshape(a_heads, A_HEAD), axis=-1, keepdims=True) * v_h
    y = (o + bonus.reshape(B, T, a_width)) * g
    return y, p_a[:, -1], S


def ab_mixer(x, shift_state, wkv_state, attend, w_in, rw, w_out):
    B, T, d_model = x.shape
    a_width = rw[1].shape[0]
    a_proj = 3 * a_width + W_LORA + AA_LORA + G_LORA
    b_width = d_model - a_width
    b_heads = b_width // B_HEAD
    proj = mm(x, w_in)
    y_a, new_shift, new_wkv = rwkv7_mix(proj[..., :a_proj], shift_state, wkv_state, *rw)
    q, k, v, qi, ki, wi = split_cols(proj[..., a_proj:], [b_width, b_width, b_width, IDX_HEADS * IDX_DIM, IDX_DIM, IDX_HEADS])
    y_b = attend(q, k, v, qi, ki, wi)
    y = mm(jnp.concatenate([y_a, y_b], axis=-1), w_out)
    heads = lambda t: t.reshape(B, T, b_heads, B_HEAD)
    return y, new_shift, new_wkv, heads(k), heads(v), ki


def attend_sample(q, k, v, qi, ki, wi, **kw):
    B, T, b_width = q.shape
    heads = lambda t: t.reshape(B, T, b_width // B_HEAD, B_HEAD)
    y = attend_paged(heads(q), heads(k), heads(v), qi.reshape(B, T, IDX_HEADS, IDX_DIM), ki, wi * IDX_HEADS ** -0.5, **kw)
    return y.reshape(B, T, b_width)


def hgrn2_mixer(x, state, w_in, lb, gn_g, w_out):
    B, T, c_width = x.shape
    q, f_raw, i, g = split_cols(mm(x, w_in), [c_width, c_width, c_width, c_width])
    log_f = jnp.logaddexp(jnp.log(lb), jnp.log1p(-lb) + jax.nn.log_sigmoid(f_raw))
    k_in = (1.0 - lb) * jax.nn.sigmoid(-f_raw)
    y, new_state = gla_gated(jax.nn.silu(q), k_in, i, log_f, g, gn_g, state)
    return mm(y, w_out), new_state


def kernel(x_prompt, x_sample, cache_k, cache_v, cache_kidx, state_wkv, state_shift, state_hgrn, page_table, w_in_ab, rwkv_mu, rwkv_w0, rwkv_w2, rwkv_a0, rwkv_a2, rwkv_g2, rwkv_kk, rwkv_ka, rwkv_rk, rwkv_lnx_g, rwkv_lnx_b, w_out_ab, rel_bias, ffn_w_gate, ffn_w_up, ffn_w_down, w_in_c, hgrn_lb, hgrn_gnorm, w_out_c, router, moe_w_gate, moe_w_up, moe_w_down, ln1_g, ln1_b, ln2_g, ln2_b):
    depth = ln1_g.shape[0]
    alpha = (2 * depth) ** 0.25
    a_width = rwkv_w0.shape[1]
    a_heads = a_width // A_HEAD
    a_proj = state_shift.shape[-1]
    c_heads, c_dk, c_dv = state_hgrn.shape[2:]
    lb_all = jnp.cumsum(jax.nn.softmax(hgrn_lb, axis=0), axis=0)
    lb_all = lb_all - lb_all[0:1]
    attend_p = functools.partial(dsa_prompt, rel_bias=rel_bias)
    xp, xs = x_prompt, x_sample
    kp_l, vp_l, kip_l, wkvp_l, shp_l, hgp_l = [], [], [], [], [], []
    ks_l, vs_l, kis_l, wkvs_l, shs_l, hgs_l = [], [], [], [], [], []
    for l in range(depth):
        j = l // 2
        if l % 2 == 0:
            rw = (rwkv_mu[j], rwkv_w0[j], rwkv_w2[j], rwkv_a0[j], rwkv_a2[j], rwkv_g2[j],
                  rwkv_kk[j], rwkv_ka[j], rwkv_rk[j], rwkv_lnx_g[j], rwkv_lnx_b[j])
            attend_s = functools.partial(attend_sample, cache_k=cache_k, cache_v=cache_v, cache_kidx=cache_kidx,
                                         layer=j, page_table=page_table, rel_bias=rel_bias)
            shift0 = jnp.zeros((xp.shape[0], a_proj), xp.dtype)
            wkv0 = jnp.zeros((xp.shape[0], a_heads, A_HEAD, A_HEAD), jnp.float32)
            yp, shp, wkvp, kp, vp, kip = ab_mixer(xp, shift0, wkv0, attend_p, (w_in_ab, j), rw, (w_out_ab, j))
            ys, shs, wkvs, kss, vss, kis = ab_mixer(xs, state_shift[j], state_wkv[j], attend_s, (w_in_ab, j), rw, (w_out_ab, j))
            kp_l.append(kp); vp_l.append(vp); kip_l.append(kip); wkvp_l.append(wkvp); shp_l.append(shp)
            ks_l.append(kss); vs_l.append(vss); kis_l.append(kis); wkvs_l.append(wkvs); shs_l.append(shs)
        else:
            hg0 = jnp.zeros((xp.shape[0], c_heads, c_dk, c_dv), jnp.float32)
            yp, hgp = hgrn2_mixer(xp, hg0, (w_in_c, j), lb_all[j], hgrn_gnorm[j], (w_out_c, j))
            ys, hgs = hgrn2_mixer(xs, state_hgrn[j], (w_in_c, j), lb_all[j], hgrn_gnorm[j], (w_out_c, j))
            hgp_l.append(hgp); hgs_l.append(hgs)
        xp = layer_norm(alpha * xp + yp, ln1_g[l], ln1_b[l])
        xs = layer_norm(alpha * xs + ys, ln1_g[l], ln1_b[l])
        if l % 2 == 0:
            fp = swiglu(xp, (ffn_w_gate, j), (ffn_w_up, j), (ffn_w_down, j))
            fs = swiglu(xs, (ffn_w_gate, j), (ffn_w_up, j), (ffn_w_down, j))
        else:
            fp = moe_swiglu(xp, router, moe_w_gate, moe_w_up, moe_w_down, j)
            fs = moe_swiglu(xs, router, moe_w_gate, moe_w_up, moe_w_down, j)
        xp = layer_norm(alpha * xp + fp, ln2_g[l], ln2_b[l])
        xs = layer_norm(alpha * xs + fs, ln2_g[l], ln2_b[l])
    return (xp, xs, jnp.stack(kp_l), jnp.stack(vp_l), jnp.stack(kip_l), jnp.stack(wkvp_l), jnp.stack(shp_l),
            jnp.stack(hgp_l), jnp.stack(ks_l), jnp.stack(vs_l), jnp.stack(kis_l), jnp.stack(wkvs_l),
            jnp.stack(shs_l), jnp.stack(hgs_l))
```

```python
import functools
import math

import jax
import jax.numpy as jnp
import numpy as np
from jax import lax
from jax.experimental import pallas as pl
from jax.experimental.pallas import tpu as pltpu

A_HEAD = 64
W_LORA = 64
AA_LORA = 64
G_LORA = 128
RWKV_GN_EPS = 64e-5
B_HEAD = 128
IDX_HEADS = 16
IDX_DIM = 64
TOPK_MAX = 256
Q_BLOCK = 64
T5_BUCKETS = 32
T5_MAX_DIST = 128
C_EXPAND = 128
GLA_CHUNK = 64
GLA_HEAD_GROUP = 8
RMS_EPS = 1e-5
N_EXPERTS = 8
TOP_EXPERTS = 2
PAGE_SIZE = 128
LN_EPS = 1e-5

VMEM_LIMIT_BYTES = 56 * 1024 * 1024
BF16_ROWS_MIN = 256


def _pick_tile(n, target, align):
    if n <= target:
        return n
    t = (target // align) * align
    while t >= align:
        if n % t == 0:
            return t
        t -= align
    return n


def _layer(w):
    return w if isinstance(w, tuple) else (w[None], 0)


def _matmul_kernel(x_ref, w_ref, o_ref):
    @pl.when(pl.program_id(2) == 0)
    def _():
        o_ref[...] = jnp.zeros_like(o_ref)

    o_ref[...] += jnp.dot(x_ref[...].astype(jnp.bfloat16), w_ref[0].astype(jnp.bfloat16),
                          preferred_element_type=jnp.float32)


def matmul(x, w, tm=1024, tn=1024, tk=2048):
    m, k = x.shape
    w, layer = _layer(w)
    k2, n = w.shape[1:]
    assert k == k2
    tm = _pick_tile(m, tm, 8)
    tk = _pick_tile(k, tk, 128)
    tn = min(tn, n) if n % 128 == 0 or n < 128 else tn
    grid = (pl.cdiv(n, tn), pl.cdiv(m, tm), k // tk)
    return pl.pallas_call(
        _matmul_kernel,
        grid=grid,
        in_specs=[pl.BlockSpec((tm, tk), lambda j, i, kk: (i, kk)),
                  pl.BlockSpec((1, tk, tn), lambda j, i, kk: (layer, kk, j))],
        out_specs=pl.BlockSpec((tm, tn), lambda j, i, kk: (i, j)),
        out_shape=jax.ShapeDtypeStruct((m, n), jnp.float32),
        compiler_params=pltpu.CompilerParams(
            dimension_semantics=("parallel", "parallel", "arbitrary"),
            vmem_limit_bytes=VMEM_LIMIT_BYTES),
        name="matmul",
    )(x, w)


def _swiglu_up_kernel(x_ref, wg_ref, wu_ref, o_ref):
    xb = x_ref[...].astype(jnp.bfloat16)
    g = jnp.dot(xb, wg_ref[0].astype(jnp.bfloat16), preferred_element_type=jnp.float32)
    u = jnp.dot(xb, wu_ref[0].astype(jnp.bfloat16), preferred_element_type=jnp.float32)
    o_ref[...] = (g * jax.nn.sigmoid(g) * u).astype(o_ref.dtype)


def swiglu_up(x, wg, wu, tm=1024, tn=512):
    m, k = x.shape
    (wg, layer), (wu, _) = _layer(wg), _layer(wu)
    n = wg.shape[2]
    tm = _pick_tile(m, tm, 8)
    tn = _pick_tile(n, tn, 128)
    return pl.pallas_call(
        _swiglu_up_kernel,
        grid=(n // tn, m // tm),
        in_specs=[pl.BlockSpec((tm, k), lambda j, i: (i, 0)),
                  pl.BlockSpec((1, k, tn), lambda j, i: (layer, 0, j)),
                  pl.BlockSpec((1, k, tn), lambda j, i: (layer, 0, j))],
        out_specs=pl.BlockSpec((tm, tn), lambda j, i: (i, j)),
        out_shape=jax.ShapeDtypeStruct((m, n), x.dtype),
        compiler_params=pltpu.CompilerParams(
            dimension_semantics=("parallel", "parallel"),
            vmem_limit_bytes=VMEM_LIMIT_BYTES),
        name="swiglu_up",
    )(x, wg, wu)


def _mxu_operand(x):
    x = x.reshape(-1, x.shape[-1])
    return x.astype(jnp.bfloat16) if x.shape[0] >= BF16_ROWS_MIN else x


def mm(x, w):
    return matmul(_mxu_operand(x), w).reshape(x.shape[:-1] + (_layer(w)[0].shape[2],))


def swiglu(x, wg, wu, wd):
    h = swiglu_up(_mxu_operand(x), wg, wu)
    return matmul(h, wd).reshape(x.shape[:-1] + (_layer(wd)[0].shape[2],))


INT_MIN = -2 ** 31
NEG_BIG = -1e30


def _sortable_key(x):
    bits = pltpu.bitcast(jnp.where(x == 0.0, 0.0, x), jnp.int32)
    return bits ^ ((bits >> 31) & 0x7FFFFFFF)


def _dsa_prompt_kernel(qt_ref, kt_ref, qi_ref, ki_ref, wit_ref, q_ref, k_ref, v_ref, bias_ref, o_ref,
                       keys_ref, tau_ref, qib_ref, qb_ref, m_ref, l_ref, acc_ref, *, topk, n_idx_bits):
    i = qt_ref[pl.program_id(1)]
    j = kt_ref[pl.program_id(1)]
    tq = q_ref.shape[1]
    n_heads = q_ref.shape[2] // B_HEAD
    kf = float(topk)

    def key_sum(x):
        part = x[:SUBLANES]
        for u in range(1, tq // SUBLANES):
            part = part + x[u * SUBLANES:(u + 1) * SUBLANES]
        return part

    @pl.when(j == 0)
    def _select():
        qib_ref[...] = (qi_ref[0] * IDX_DIM ** -0.5).astype(jnp.bfloat16)
        qb_ref[...] = q_ref[0].astype(jnp.bfloat16)
        m_ref[...] = jnp.full_like(m_ref, NEG_BIG)
        l_ref[...] = jnp.zeros_like(l_ref)
        acc_ref[...] = jnp.zeros_like(acc_ref)
        wb = (wit_ref[0] * IDX_HEADS ** -0.5).astype(jnp.bfloat16).astype(jnp.float32)
        key_pos = lax.broadcasted_iota(jnp.int32, (tq, tq), 0)
        qry_pos = i * tq + lax.broadcasted_iota(jnp.int32, (tq, tq), 1)

        def score_chunk(c, carry):
            kc = ki_ref[0, pl.ds(pl.multiple_of(c * tq, tq), tq), :]
            score = jnp.zeros((tq, tq), jnp.float32)
            for h in range(IDX_HEADS):
                s = _dot_nt(kc, qib_ref[:, h * IDX_DIM:(h + 1) * IDX_DIM])
                r = jnp.maximum(s, 0.0).astype(jnp.bfloat16).astype(jnp.float32)
                score = score + wb[h:h + 1, :] * r
            keys_ref[c] = jnp.where(c * tq + key_pos <= qry_pos, _sortable_key(score), INT_MIN)
            return carry

        lax.fori_loop(0, i + 1, score_chunk, 0)

        def count(pred):
            def body(c, acc):
                return acc + key_sum(jnp.where(pred(keys_ref[c], c), 1.0, 0.0))
            acc = lax.fori_loop(0, i + 1, body, jnp.zeros((SUBLANES, tq), jnp.float32))
            return jnp.sum(acc, axis=0, keepdims=True)

        tau = jnp.where(count(lambda kc, c: kc >= 0) >= kf, 0, INT_MIN).astype(jnp.int32)

        def bit_step(n, tau):
            cand = tau | jnp.left_shift(jnp.int32(1), 30 - n)
            return jnp.where(count(lambda kc, c: kc >= cand) >= kf, cand, tau)

        tau = lax.fori_loop(0, 31, bit_step, tau)
        tau = jnp.maximum(tau, INT_MIN + 1)
        tau_ref[...] = tau
        n_ge = count(lambda kc, c: kc >= tau)

        @pl.when(jnp.max(n_ge) > kf)
        def _ties():
            need = kf - count(lambda kc, c: kc > tau)

            def idx_step(n, ans):
                cand = ans + jnp.left_shift(jnp.int32(1), n_idx_bits - 1 - n)
                f = count(lambda kc, c: (kc == tau) & (c * tq + key_pos < cand))
                return jnp.where(f < need, cand, ans)

            last = lax.fori_loop(0, n_idx_bits, idx_step, jnp.zeros((1, tq), jnp.int32))

            def demote(c, carry):
                kc = keys_ref[c]
                keys_ref[c] = jnp.where((kc == tau) & (c * tq + key_pos > last), INT_MIN, kc)
                return carry

            lax.fori_loop(0, i + 1, demote, 0)

    sel = keys_ref[j] >= tau_ref[...]
    heads = [slice(h * B_HEAD, (h + 1) * B_HEAD) for h in range(n_heads)]
    raw = [_dot_nt(k_ref[0, :, hs].astype(jnp.bfloat16), qb_ref[:, hs]) for hs in heads]
    probs, alphas = [], []
    for h in range(n_heads):
        logits = jnp.where(sel, raw[h] * B_HEAD ** -0.5 + bias_ref[0, h], NEG_BIG)
        m_old = m_ref[h]
        m_new = jnp.maximum(m_old, jnp.max(logits, axis=0, keepdims=True))
        alpha = jnp.exp(m_old - m_new)
        p = jnp.exp(logits - m_new)
        l_ref[h] = alpha * l_ref[h] + jnp.sum(p, axis=0, keepdims=True)
        m_ref[h] = m_new
        probs.append(p.astype(jnp.bfloat16))
        alphas.append(alpha)
    for h, hs in enumerate(heads):
        acc_ref[h] = alphas[h] * acc_ref[h] + lax.dot_general(
            v_ref[0, :, hs].astype(jnp.bfloat16), probs[h], (((0,), (0,)), ((), ())),
            preferred_element_type=jnp.float32)

    @pl.when(j == i)
    def _finish():
        for h in range(n_heads):
            o_ref[0, :, h * B_HEAD:(h + 1) * B_HEAD] = jnp.transpose(acc_ref[h] / l_ref[h])


def dsa_prompt(q, k, v, qi, ki, wi, rel_bias, tq=256):
    B, T, b_width = q.shape
    n_heads = b_width // B_HEAD
    tq = min(tq, T)
    assert T % tq == 0 and tq % 128 == 0 and tq + 1 >= T5_MAX_DIST
    nq = T // tq
    topk = min(TOPK_MAX, T // 4)
    r = jnp.arange(tq, dtype=jnp.int32)
    dist = jnp.arange(3, dtype=jnp.int32)[:, None, None] * tq + r[None, None, :] - r[None, :, None]
    bias3 = jnp.einsum('dkqc,ch->dhkq', jax.nn.one_hot(t5_bucket(dist), T5_BUCKETS, dtype=jnp.float32), rel_bias,
                       precision=lax.Precision.HIGHEST)
    kern = functools.partial(_dsa_prompt_kernel, topk=topk, n_idx_bits=max(1, (T - 1).bit_length()))
    pairs = np.array([(i, j) for i in range(nq) for j in range(i + 1)], dtype=np.int32)
    q_tile, k_tile = jnp.asarray(pairs[:, 0]), jnp.asarray(pairs[:, 1])
    return pl.pallas_call(
        kern,
        grid_spec=pltpu.PrefetchScalarGridSpec(
            num_scalar_prefetch=2, grid=(B, len(pairs)),
            in_specs=[pl.BlockSpec((1, tq, qi.shape[2]), lambda b, p, qt, kt: (b, qt[p], 0)),
                      pl.BlockSpec((1, T, IDX_DIM), lambda b, p, qt, kt: (b, 0, 0)),
                      pl.BlockSpec((1, IDX_HEADS, tq), lambda b, p, qt, kt: (b, 0, qt[p])),
                      pl.BlockSpec((1, tq, b_width), lambda b, p, qt, kt: (b, qt[p], 0)),
                      pl.BlockSpec((1, tq, b_width), lambda b, p, qt, kt: (b, kt[p], 0)),
                      pl.BlockSpec((1, tq, b_width), lambda b, p, qt, kt: (b, kt[p], 0)),
                      pl.BlockSpec((1, n_heads, tq, tq),
                                   lambda b, p, qt, kt: (jnp.minimum(qt[p] - kt[p], 2), 0, 0, 0))],
            out_specs=pl.BlockSpec((1, tq, b_width), lambda b, p, qt, kt: (b, qt[p], 0)),
            scratch_shapes=[pltpu.VMEM((nq, tq, tq), jnp.int32),
                            pltpu.VMEM((1, tq), jnp.int32),
                            pltpu.VMEM((tq, qi.shape[2]), jnp.bfloat16),
                            pltpu.VMEM((tq, b_width), jnp.bfloat16),
                            pltpu.VMEM((n_heads, 1, tq), jnp.float32),
                            pltpu.VMEM((n_heads, 1, tq), jnp.float32),
                            pltpu.VMEM((n_heads, B_HEAD, tq), jnp.float32)]),
        out_shape=jax.ShapeDtypeStruct((B, T, b_width), jnp.float32),
        compiler_params=pltpu.CompilerParams(
            dimension_semantics=("parallel", "arbitrary"),
            vmem_limit_bytes=VMEM_LIMIT_BYTES),
        name="dsa_prompt",
    )(q_tile, k_tile, qi, ki.astype(jnp.bfloat16), jnp.swapaxes(wi, 1, 2), q, k, v, bias3)


SCAN_CHUNK = A_HEAD
LANES = 128
SUBLANES = 8


def _rwkv_scan_kernel(vt_ref, kk_ref, kka_ref, w_ref, km_ref, r_ref, s0_ref, ot_ref, st_ref, s_scr, *, n_steps):
    c = pl.program_id(0)
    n_batch = vt_ref.shape[0]
    n_tiles = vt_ref.shape[3] // LANES
    bf = jnp.bfloat16

    @pl.when(c == 0)
    def _():
        s_scr[...] = s0_ref[...]

    if n_steps < SCAN_CHUNK:
        ot_ref[...] = jnp.zeros_like(ot_ref)
    lane = lax.broadcasted_iota(jnp.int32, (A_HEAD, LANES), 1)
    seg_base = lane & (LANES - A_HEAD)
    lane_t = lane & (A_HEAD - 1)
    seg_ones = ((lax.broadcasted_iota(jnp.int32, (LANES, LANES), 0) // A_HEAD)
                == (lax.broadcasted_iota(jnp.int32, (LANES, LANES), 1) // A_HEAD)).astype(bf)

    group = min(n_steps, SUBLANES)
    tiles = [(b, slice(p * LANES, (p + 1) * LANES)) for b in range(n_batch) for p in range(n_tiles)]

    def seg_sums(xs):
        lhs = jnp.concatenate([x.astype(bf) for x in xs], axis=0)
        out = jnp.dot(lhs, seg_ones, preferred_element_type=jnp.float32)
        return [out[i * A_HEAD:(i + 1) * A_HEAD] for i in range(len(xs))]

    def step_group(g, carry):
        t0 = g * group if isinstance(g, int) else pl.multiple_of(g * group, group)
        rows = [[ref[b, pl.ds(t0, group), ls] for (b, ls) in tiles] for ref in (kk_ref, kka_ref, w_ref, km_ref, r_ref)]
        for u in range(group):
            kk_r, kka_r, w_r, km_r, r_r = ([x[u:u + 1] for x in q] for q in rows)
            s_kk = seg_sums([s_scr[b, :, ls] * kk_r[i] for i, (b, ls) in enumerate(tiles)])
            s_new = []
            for i, (b, ls) in enumerate(tiles):
                v_col = jnp.take_along_axis(vt_ref[b, 0, :, ls], seg_base + (t0 + u), axis=1)
                s = s_scr[b, :, ls] * w_r[i] - s_kk[i] * kka_r[i] + v_col * km_r[i]
                s_scr[b, :, ls] = s
                s_new.append(s * r_r[i])
            o_col = seg_sums(s_new)
            for i, (b, ls) in enumerate(tiles):
                ot_ref[b, 0, :, ls] = jnp.where(lane_t == t0 + u, o_col[i], ot_ref[b, 0, :, ls])
        return carry

    if n_steps == group:
        step_group(0, 0)
    else:
        lax.fori_loop(0, n_steps // group, step_group, 0)

    @pl.when(c == pl.num_programs(0) - 1)
    def _():
        st_ref[...] = s_scr[...]


def _rwkv_step_kernel(s_ref, kk_ref, kka_ref, w_ref, km_ref, r_ref, v_ref, o_ref, st_ref):
    rnd = lambda x: x.astype(jnp.bfloat16).astype(jnp.float32)
    s = s_ref[...]
    s_kk = jnp.sum(rnd(s) * rnd(kk_ref[...]), axis=-1, keepdims=True)
    s = s * w_ref[...] - s_kk * kka_ref[...] + v_ref[...] * km_ref[...]
    o_ref[...] = jnp.sum(rnd(s) * rnd(r_ref[...]), axis=-1, keepdims=True)
    st_ref[...] = s


def rwkv_step(r, w, k, v, kk, a, state, groups_per_block=16):
    B, _, width = r.shape
    n_heads = width // A_HEAD
    n_groups = B * n_heads
    gb = _pick_tile(n_groups, groups_per_block, 1)
    row = lambda x: x.reshape(n_groups, 1, A_HEAD)
    row_spec = pl.BlockSpec((gb, 1, A_HEAD), lambda g: (g, 0, 0))
    col_spec = pl.BlockSpec((gb, A_HEAD, 1), lambda g: (g, 0, 0))
    st_spec = pl.BlockSpec((gb, A_HEAD, A_HEAD), lambda g: (g, 0, 0))
    o, st = pl.pallas_call(
        _rwkv_step_kernel,
        grid=(n_groups // gb,),
        in_specs=[st_spec] + [row_spec] * 5 + [col_spec],
        out_specs=[col_spec, st_spec],
        out_shape=[jax.ShapeDtypeStruct((n_groups, A_HEAD, 1), jnp.float32),
                   jax.ShapeDtypeStruct((n_groups, A_HEAD, A_HEAD), jnp.float32)],
        compiler_params=pltpu.CompilerParams(dimension_semantics=("parallel",), vmem_limit_bytes=VMEM_LIMIT_BYTES),
        name="rwkv_step",
    )(state.reshape(n_groups, A_HEAD, A_HEAD), row(kk), row(kk * a), row(w), row(k), row(r),
      v.reshape(n_groups, A_HEAD, 1))
    return o.reshape(B, 1, width), st.reshape(state.shape)


def rwkv_scan(r, w, k, v, kk, a, state):
    B, T, width = r.shape
    n_heads = width // A_HEAD
    if T == 1:
        return rwkv_step(r, w, k, v, kk, a, state)
    assert SCAN_CHUNK == A_HEAD and (T % SCAN_CHUNK == 0 or T < SCAN_CHUNK)
    n_chunks = pl.cdiv(T, SCAN_CHUNK)
    t_pad = n_chunks * SCAN_CHUNK
    pad = lambda x: jnp.pad(x, ((0, 0), (0, t_pad - T), (0, 0)))
    vt = jnp.transpose(pad(v).reshape(B, n_chunks, SCAN_CHUNK, n_heads, A_HEAD), (0, 1, 4, 3, 2))
    vt = vt.reshape(B, n_chunks, A_HEAD, width)
    st0 = jnp.transpose(state, (0, 2, 1, 3)).reshape(B, A_HEAD, width)
    row_spec = pl.BlockSpec((B, SCAN_CHUNK, width), lambda c: (0, c, 0))
    col_spec = pl.BlockSpec((B, 1, A_HEAD, width), lambda c: (0, c, 0, 0))
    st_spec = pl.BlockSpec((B, A_HEAD, width), lambda c: (0, 0, 0))
    ot, st = pl.pallas_call(
        functools.partial(_rwkv_scan_kernel, n_steps=min(T, SCAN_CHUNK)),
        grid=(n_chunks,),
        in_specs=[col_spec] + [row_spec] * 5 + [st_spec],
        out_specs=[col_spec, st_spec],
        out_shape=[jax.ShapeDtypeStruct((B, n_chunks, A_HEAD, width), jnp.float32),
                   jax.ShapeDtypeStruct((B, A_HEAD, width), jnp.float32)],
        scratch_shapes=[pltpu.VMEM((B, A_HEAD, width), jnp.float32)],
        compiler_params=pltpu.CompilerParams(
            dimension_semantics=("arbitrary",),
            vmem_limit_bytes=VMEM_LIMIT_BYTES),
        name="rwkv_scan",
    )(vt, pad(kk), pad(kk * a), pad(w), pad(k), pad(r), st0)
    o = jnp.transpose(ot.reshape(B, n_chunks, A_HEAD, n_heads, SCAN_CHUNK), (0, 1, 4, 3, 2)).reshape(B, t_pad, width)
    new_state = jnp.transpose(st.reshape(B, A_HEAD, n_heads, A_HEAD), (0, 2, 1, 3))
    return o[:, :T], new_state


def _split2(x):
    hi = x.astype(jnp.bfloat16)
    return hi, (x - hi.astype(jnp.float32)).astype(jnp.bfloat16)


def _dot_nt_split(a, b):
    a_hi, a_lo = _split2(a)
    b_hi, b_lo = _split2(b)
    return _dot_nt(jnp.concatenate([a_hi, a_hi, a_lo], axis=1), jnp.concatenate([b_hi, b_lo, b_hi], axis=1))


def _gla_segment_matrix(chunk):
    t = np.arange(chunk)[:, None]
    u = np.arange(chunk)[None, :]
    rows = []
    for lvl in range(chunk.bit_length() - 1):
        m = (chunk // 2) >> lvl
        blk = t // m
        right = np.logical_and(u >= blk * m, u <= t)
        left = np.logical_and(u > t, u < (blk + 1) * m)
        rows.append(np.where(blk % 2 == 1, right, left))
    rows.append(u <= t)
    rows.append(u > t)
    seg = np.concatenate(rows, axis=0).astype(np.float32)
    return np.concatenate([seg, seg, seg], axis=1)


def _dot_nt(a, b, **kw):
    return lax.dot_general(a, b, (((1,), (1,)), ((), ())), preferred_element_type=jnp.float32, **kw)


def _gla_kernel(seg_ref, q_ref, k_ref, v_ref, lf_ref, g_ref, gn_ref, s0_ref, y_ref, st_ref, s_scr):
    c = pl.program_id(1)
    chunk = q_ref.shape[1]
    n_levels = chunk.bit_length() - 1
    n_heads = s_scr.shape[0]
    bf = jnp.bfloat16

    @pl.when(c == 0)
    def _():
        s_scr[...] = s0_ref[0]

    row = lax.broadcasted_iota(jnp.int32, (chunk, chunk), 0)
    col = lax.broadcasted_iota(jnp.int32, (chunk, chunk), 1)
    row_w = lax.broadcasted_iota(jnp.int32, (chunk, C_EXPAND), 0)
    seg = seg_ref[...]
    gn = gn_ref[...]

    def head_group(hg, carry):
        hs = [hg * GLA_HEAD_GROUP + u for u in range(GLA_HEAD_GROUP)]
        lanes = [pl.ds(pl.multiple_of(h * C_EXPAND, C_EXPAND), C_EXPAND) for h in hs]
        qs = [q_ref[0, :, ls] for ls in lanes]
        ks = [k_ref[0, :, ls] for ls in lanes]
        es = []
        for ls in lanes:
            lf = lf_ref[0, :, ls]
            lf_hi = lf.astype(bf)
            lf_r = lf - lf_hi.astype(jnp.float32)
            lf_mid = lf_r.astype(bf)
            lf_lo = (lf_r - lf_mid.astype(jnp.float32)).astype(bf)
            es.append(jnp.exp(jnp.dot(seg, jnp.concatenate([lf_hi, lf_mid, lf_lo], axis=0),
                                      preferred_element_type=jnp.float32)))
        grams = []
        for q, k, e in zip(qs, ks, es):
            g_h = [_dot_nt_split(q, k)]
            for lvl in range(n_levels):
                sh = n_levels - 1 - lvl
                x = jnp.where(((row_w >> sh) & 1) == 1, q, k) * e[lvl * chunk:(lvl + 1) * chunk]
                g_h.append(_dot_nt_split(x, x))
            grams.append(g_h)
        outs = []
        for h, ls, q, k, e, g_h in zip(hs, lanes, qs, ks, es, grams):
            a = jnp.where(row == col, g_h[0], 0.0)
            for lvl in range(n_levels):
                sh = n_levels - 1 - lvl
                pair = (((row >> sh) & 1) == 1) & ((col >> sh) == (row >> sh) - 1)
                a = a + jnp.where(pair, g_h[lvl + 1], 0.0)
            e_b = e[n_levels * chunk:(n_levels + 1) * chunk]
            e_r = e[(n_levels + 1) * chunk:]
            v = v_ref[0, :, ls].astype(bf)
            st = s_scr[h]
            o = jnp.dot(a.astype(bf), v, preferred_element_type=jnp.float32)
            outs.append(o + _dot_nt((q * e_b).astype(bf), st.astype(bf)))
            s_scr[h] = st * e_b[chunk - 1:chunk, :] + lax.dot_general(
                v, (k * e_r).astype(bf), (((0,), (0,)), ((), ())), preferred_element_type=jnp.float32)
        for ls, o in zip(lanes, outs):
            o = o * lax.rsqrt(jnp.mean(o * o, axis=1, keepdims=True) + RMS_EPS) * gn
            gate = g_ref[0, :, ls]
            y_ref[0, :, ls] = o * (gate * jax.nn.sigmoid(gate))
        return carry

    lax.fori_loop(0, n_heads // GLA_HEAD_GROUP, head_group, 0)

    @pl.when(c == pl.num_programs(1) - 1)
    def _():
        st_ref[0] = s_scr[...]


def _gla_step_kernel(st_ref, q_ref, k_ref, lf_ref, v_ref, g_ref, gn_ref, y_ref, so_ref):
    rnd = lambda x: x.astype(jnp.bfloat16).astype(jnp.float32)
    q, k, v = q_ref[...], k_ref[...], v_ref[...]
    decay = jnp.exp(lf_ref[...])
    st = st_ref[...]
    o = jnp.sum(q * k, axis=-1, keepdims=True) * v + jnp.sum(rnd(q * decay) * rnd(st), axis=-1, keepdims=True)
    so_ref[...] = st * decay + v * k
    o = o * lax.rsqrt(jnp.mean(o * o, axis=1, keepdims=True) + RMS_EPS) * gn_ref[...]
    gate = g_ref[...]
    y_ref[...] = o * (gate * jax.nn.sigmoid(gate))


def gla_step(q, k, v, log_f, g, gn_g, state, groups_per_block=16):
    B, _, width = q.shape
    n_heads = width // C_EXPAND
    n_groups = B * n_heads
    gb = _pick_tile(n_groups, groups_per_block, 1)
    row = lambda x: x.reshape(n_groups, 1, C_EXPAND)
    col = lambda x: x.reshape(n_groups, C_EXPAND, 1)
    row_spec = pl.BlockSpec((gb, 1, C_EXPAND), lambda i: (i, 0, 0))
    col_spec = pl.BlockSpec((gb, C_EXPAND, 1), lambda i: (i, 0, 0))
    st_spec = pl.BlockSpec((gb, C_EXPAND, C_EXPAND), lambda i: (i, 0, 0))
    y, st = pl.pallas_call(
        _gla_step_kernel,
        grid=(n_groups // gb,),
        in_specs=[st_spec, row_spec, row_spec, row_spec, col_spec, col_spec,
                  pl.BlockSpec((1, C_EXPAND, 1), lambda i: (0, 0, 0))],
        out_specs=[col_spec, st_spec],
        out_shape=[jax.ShapeDtypeStruct((n_groups, C_EXPAND, 1), jnp.float32),
                   jax.ShapeDtypeStruct((n_groups, C_EXPAND, C_EXPAND), jnp.float32)],
        compiler_params=pltpu.CompilerParams(dimension_semantics=("parallel",), vmem_limit_bytes=VMEM_LIMIT_BYTES),
        name="gla_step",
    )(jnp.swapaxes(state, 2, 3).reshape(n_groups, C_EXPAND, C_EXPAND), row(q), row(k), row(log_f), col(v), col(g),
      gn_g.reshape(1, C_EXPAND, 1))
    return y.reshape(B, 1, width), jnp.swapaxes(st.reshape(B, n_heads, C_EXPAND, C_EXPAND), 2, 3)


def gla_gated(q, k, v, log_f, g, gn_g, state):
    B, T, width = q.shape
    n_heads = width // C_EXPAND
    assert v.shape[2] == width, "kernel assumes dk == dv == C_EXPAND"
    if T == 1:
        return gla_step(q, k, v, log_f, g, gn_g, state)
    n_chunks = pl.cdiv(T, GLA_CHUNK)
    t_pad = n_chunks * GLA_CHUNK
    pad = lambda x: jnp.pad(x, ((0, 0), (0, t_pad - T), (0, 0)))
    seg = jnp.asarray(_gla_segment_matrix(GLA_CHUNK), dtype=jnp.bfloat16)
    st0 = jnp.swapaxes(state, 2, 3)
    tok_spec = pl.BlockSpec((1, GLA_CHUNK, width), lambda b, c: (b, c, 0))
    st_spec = pl.BlockSpec((1, n_heads, C_EXPAND, C_EXPAND), lambda b, c: (b, 0, 0, 0))
    y, st = pl.pallas_call(
        _gla_kernel,
        grid=(B, n_chunks),
        in_specs=[pl.BlockSpec(seg.shape, lambda b, c: (0, 0))] + [tok_spec] * 5
                 + [pl.BlockSpec((1, C_EXPAND), lambda b, c: (0, 0)), st_spec],
        out_specs=[tok_spec, st_spec],
        out_shape=[jax.ShapeDtypeStruct((B, t_pad, width), jnp.float32),
                   jax.ShapeDtypeStruct(st0.shape, jnp.float32)],
        scratch_shapes=[pltpu.VMEM((n_heads, C_EXPAND, C_EXPAND), jnp.float32)],
        compiler_params=pltpu.CompilerParams(
            dimension_semantics=("parallel", "arbitrary"),
            vmem_limit_bytes=VMEM_LIMIT_BYTES),
        name="gla_gated",
    )(seg, pad(q), pad(k), pad(v), pad(log_f), pad(g), gn_g.reshape(1, C_EXPAND), st0)
    return y[:, :T], jnp.swapaxes(st, 2, 3)


def _moe_up_kernel(te_ref, nt_ref, x_ref, wg_ref, wu_ref, o_ref):
    del te_ref
    used = pl.program_id(1) < nt_ref[0]

    @pl.when(used)
    def _():
        xb = x_ref[...].astype(jnp.bfloat16)
        g = jnp.dot(xb, wg_ref[0, 0].astype(jnp.bfloat16), preferred_element_type=jnp.float32)
        u = jnp.dot(xb, wu_ref[0, 0].astype(jnp.bfloat16), preferred_element_type=jnp.float32)
        o_ref[...] = (g * jax.nn.sigmoid(g) * u).astype(o_ref.dtype)

    @pl.when(jnp.logical_not(used))
    def _():
        o_ref[...] = jnp.zeros_like(o_ref)


def _moe_down_kernel(te_ref, nt_ref, h_ref, wd_ref, o_ref):
    del te_ref
    used = pl.program_id(1) < nt_ref[0]

    @pl.when(used)
    def _():
        o_ref[...] = jnp.dot(h_ref[...].astype(jnp.bfloat16), wd_ref[0, 0].astype(jnp.bfloat16),
                             preferred_element_type=jnp.float32)

    @pl.when(jnp.logical_not(used))
    def _():
        o_ref[...] = jnp.zeros_like(o_ref)


def moe_experts(tile_expert, n_tiles_used, xs, wg, wu, wd, layer, tm, tf=1408, tn=1024):
    rows, d_model = xs.shape
    d_ff = wg.shape[3]
    tf = _pick_tile(d_ff, tf, 128)
    tn = _pick_tile(d_model, tn, 128)
    params = pltpu.CompilerParams(dimension_semantics=("parallel", "arbitrary"), vmem_limit_bytes=VMEM_LIMIT_BYTES)
    h = pl.pallas_call(
        _moe_up_kernel,
        grid_spec=pltpu.PrefetchScalarGridSpec(
            num_scalar_prefetch=2, grid=(d_ff // tf, rows // tm),
            in_specs=[pl.BlockSpec((tm, d_model), lambda j, i, te, nt: (i, 0)),
                      pl.BlockSpec((1, 1, d_model, tf), lambda j, i, te, nt: (layer, te[i], 0, j),
                                   pipeline_mode=pl.Buffered(1)),
                      pl.BlockSpec((1, 1, d_model, tf), lambda j, i, te, nt: (layer, te[i], 0, j),
                                   pipeline_mode=pl.Buffered(1))],
            out_specs=pl.BlockSpec((tm, tf), lambda j, i, te, nt: (i, j))),
        out_shape=jax.ShapeDtypeStruct((rows, d_ff), jnp.bfloat16 if rows >= BF16_ROWS_MIN else jnp.float32),
        compiler_params=params, name="moe_up",
    )(tile_expert, n_tiles_used, xs, wg, wu)
    return pl.pallas_call(
        _moe_down_kernel,
        grid_spec=pltpu.PrefetchScalarGridSpec(
            num_scalar_prefetch=2, grid=(d_model // tn, rows // tm),
            in_specs=[pl.BlockSpec((tm, d_ff), lambda j, i, te, nt: (i, 0)),
                      pl.BlockSpec((1, 1, d_ff, tn), lambda j, i, te, nt: (layer, te[i], 0, j))],
            out_specs=pl.BlockSpec((tm, tn), lambda j, i, te, nt: (i, j))),
        out_shape=jax.ShapeDtypeStruct((rows, d_model), jnp.float32),
        compiler_params=params, name="moe_down",
    )(tile_expert, n_tiles_used, h, wd)


def moe_swiglu(x, w_router, wg, wu, wd, layer):
    lead, d_model = x.shape[:-1], x.shape[-1]
    x = x.reshape(-1, d_model)
    n_tok = x.shape[0]
    n_pairs = n_tok * TOP_EXPERTS
    tm = min(512, n_pairs)
    assert n_pairs % tm == 0
    probs = jax.nn.softmax(matmul(x, (w_router, layer)), axis=-1)
    top_p, top_i = lax.top_k(probs, TOP_EXPERTS)
    top_p = top_p / jnp.sum(top_p, axis=-1, keepdims=True)
    pair_expert = top_i.reshape(-1)
    member = jax.nn.one_hot(pair_expert, N_EXPERTS, dtype=jnp.int32)
    seen = jnp.cumsum(member, axis=0)
    rank = jnp.sum(member * seen, axis=1) - 1
    counts = seen[-1]
    padded = (counts + tm - 1) // tm * tm
    padded_end = jnp.cumsum(padded)
    pair_row = ((padded_end - padded)[pair_expert] + rank).astype(jnp.int32)
    rows = n_pairs + N_EXPERTS * tm
    row_token = jnp.zeros((rows,), jnp.int32).at[pair_row].set(jnp.arange(n_pairs, dtype=jnp.int32) // TOP_EXPERTS)
    tile_start = jnp.arange(rows // tm, dtype=jnp.int32) * tm
    tile_expert = jnp.minimum(jnp.searchsorted(padded_end, tile_start, side='right'), N_EXPERTS - 1).astype(jnp.int32)
    n_tiles_used = (padded_end[-1:] // tm).astype(jnp.int32)
    y_rows = moe_experts(tile_expert, n_tiles_used, x[row_token], wg, wu, wd, layer, tm)
    pair_row = pair_row.reshape(n_tok, TOP_EXPERTS)
    y = sum(top_p[:, s:s + 1] * y_rows[pair_row[:, s]] for s in range(TOP_EXPERTS))
    return y.reshape(lead + (d_model,))


def split_cols(a, sizes):
    return jnp.split(a, np.cumsum(sizes)[:-1].tolist(), axis=-1)


def gather_rows(a, idx):
    return jax.vmap(lambda a_b, i_b: a_b[i_b])(a, idx)


def layer_norm(x, g, b):
    mu = jnp.mean(x, axis=-1, keepdims=True)
    var = jnp.mean(jnp.square(x - mu), axis=-1, keepdims=True)
    return (x - mu) * lax.rsqrt(var + LN_EPS) * g + b


def t5_bucket(dist):
    dist = jnp.maximum(dist, 0)
    max_exact = T5_BUCKETS // 2
    d = jnp.maximum(dist, 1).astype(jnp.float32)
    large = max_exact + (jnp.log(d / max_exact) / math.log(T5_MAX_DIST / max_exact) * (T5_BUCKETS - max_exact)).astype(jnp.int32)
    large = jnp.minimum(large, T5_BUCKETS - 1)
    return jnp.where(dist < max_exact, dist, large)


def dsa_attend(q, qi, wi, q_pos, k_idx, gather_kv, rel_bias, topk):
    B, T = q.shape[:2]
    L = k_idx.shape[1]
    blk = min(Q_BLOCK, T)
    nb = -(-T // blk)
    pad = nb * blk - T

    def to_blocks(a):
        a = jnp.pad(a, [(0, 0), (0, pad)] + [(0, 0)] * (a.ndim - 2))
        return jnp.moveaxis(a.reshape((B, nb, blk) + a.shape[2:]), 1, 0)

    pos_b = jnp.pad(q_pos, (0, pad)).reshape(nb, blk)
    k_pos = jnp.arange(L, dtype=jnp.int32)

    def one_block(args):
        qb, qib, wib, pb = args
        s = jnp.einsum('bthd,bsd->bths', qib, k_idx) * IDX_DIM ** -0.5
        score = jnp.einsum('bth,bths->bts', wib, jax.nn.relu(s))
        score = jnp.where(k_pos[None, None, :] <= pb[None, :, None], score, -jnp.inf)
        _, sel = lax.top_k(score, topk)
        valid = sel <= pb[None, :, None]
        k_sel, v_sel = gather_kv(sel)
        logits = jnp.einsum('bthd,btkhd->bhtk', qb, k_sel) * B_HEAD ** -0.5
        bias = rel_bias[t5_bucket(pb[None, :, None] - sel)]
        logits = logits + jnp.transpose(bias, (0, 3, 1, 2))
        logits = jnp.where(valid[:, None], logits, -jnp.inf)
        p = jax.nn.softmax(logits, axis=-1)
        return jnp.einsum('bhtk,btkhd->bthd', p, v_sel)

    out = lax.map(one_block, (to_blocks(q), to_blocks(qi), to_blocks(wi), pos_b))
    return jnp.moveaxis(out, 0, 1).reshape((B, nb * blk) + q.shape[2:])[:, :T]


def attend_paged(q, k, v, qi, ki, wi, cache_k, cache_v, cache_kidx, layer, page_table, rel_bias):
    B, T = q.shape[:2]
    past = page_table.shape[1] * PAGE_SIZE
    pages = page_table + layer * cache_k.shape[1]
    kidx_past = cache_kidx.reshape((-1,) + cache_kidx.shape[2:])[pages].reshape(B, past, IDX_DIM)
    kidx_all = jnp.concatenate([kidx_past, ki], axis=1)
    flat_k = cache_k.reshape((-1,) + cache_k.shape[3:])
    flat_v = cache_v.reshape((-1,) + cache_v.shape[3:])

    def gather(sel):
        ps = jnp.minimum(sel, past - 1)
        row = gather_rows(pages, ps // PAGE_SIZE) * PAGE_SIZE + ps % PAGE_SIZE
        is_new = (sel >= past)[..., None, None]
        ns = jnp.clip(sel - past, 0, T - 1)
        k_sel = jnp.where(is_new, gather_rows(k, ns), flat_k[row])
        v_sel = jnp.where(is_new, gather_rows(v, ns), flat_v[row])
        return k_sel, v_sel

    q_pos = past + jnp.arange(T, dtype=jnp.int32)
    return dsa_attend(q, qi, wi, q_pos, kidx_all, gather, rel_bias, min(TOPK_MAX, (past + T) // 4))


def rwkv7_mix(p_a, shift_state, wkv_state, mu, w0, w2, a0, a2, g2, kk_s, ka_s, rk_s, lnx_g, lnx_b):
    B, T, _ = p_a.shape
    a_width = w0.shape[0]
    a_heads = a_width // A_HEAD
    prev = jnp.concatenate([shift_state[:, None, :], p_a[:, :-1]], axis=1)
    xm = p_a + (prev - p_a) * mu
    r, k, v, wd, ad, gd = split_cols(xm, [a_width, a_width, a_width, W_LORA, AA_LORA, G_LORA])
    w_pre = w0 + mm(jnp.tanh(wd), w2)
    decay = jnp.exp(-jnp.exp(-jax.nn.softplus(-w_pre) - 0.5))
    a = jax.nn.sigmoid(a0 + mm(ad, a2))
    g = mm(jax.nn.sigmoid(gd), g2)
    heads = lambda t: t.reshape(B, T, a_heads, A_HEAD)
    kk = heads(k * kk_s)
    kk = kk / jnp.maximum(jnp.sqrt(jnp.sum(jnp.square(kk), axis=-1, keepdims=True)), 1e-12)
    k_mod = heads(k * (1.0 + (a - 1.0) * ka_s))
    r_h, v_h = heads(r), heads(v)
    o, S = rwkv_scan(r, decay, k_mod.reshape(B, T, a_width), v, kk.reshape(B, T, a_width), a, wkv_state)
    o = heads(o)
    mu_o = jnp.mean(o, axis=-1, keepdims=True)
    var_o = jnp.mean(jnp.square(o - mu_o), axis=-1, keepdims=True)
    o = ((o - mu_o) * lax.rsqrt(var_o + RWKV_GN_EPS)).reshape(B, T, a_width) * lnx_g + lnx_b
    bonus = jnp.sum(r_h * k_mod * rk_s.reshape(a_heads, A_HEAD), axis=-1, keepdims=True) * v_h
    y = (o + bonus.reshape(B, T, a_width)) * g
    return y, p_a[:, -1], S


def ab_mixer(x, shift_state, wkv_state, attend, w_in, rw, w_out):
    B, T, d_model = x.shape
    a_width = rw[1].shape[0]
    a_proj = 3 * a_width + W_LORA + AA_LORA + G_LORA
    b_width = d_model - a_width
    b_heads = b_width // B_HEAD
    proj = mm(x, w_in)
    y_a, new_shift, new_wkv = rwkv7_mix(proj[..., :a_proj], shift_state, wkv_state, *rw)
    q, k, v, qi, ki, wi = split_cols(proj[..., a_proj:], [b_width, b_width, b_width, IDX_HEADS * IDX_DIM, IDX_DIM, IDX_HEADS])
    y_b = attend(q, k, v, qi, ki, wi)
    y = mm(jnp.concatenate([y_a, y_b], axis=-1), w_out)
    heads = lambda t: t.reshape(B, T, b_heads, B_HEAD)
    return y, new_shift, new_wkv, heads(k), heads(v), ki


def attend_sample(q, k, v, qi, ki, wi, **kw):
    B, T, b_width = q.shape
    heads = lambda t: t.reshape(B, T, b_width // B_HEAD, B_HEAD)
    y = attend_paged(heads(q), heads(k), heads(v), qi.reshape(B, T, IDX_HEADS, IDX_DIM), ki, wi * IDX_HEADS ** -0.5, **kw)
    return y.reshape(B, T, b_width)


def hgrn2_mixer(x, state, w_in, lb, gn_g, w_out):
    B, T, c_width = x.shape
    q, f_raw, i, g = split_cols(mm(x, w_in), [c_width, c_width, c_width, c_width])
    log_f = jnp.logaddexp(jnp.log(lb), jnp.log1p(-lb) + jax.nn.log_sigmoid(f_raw))
    k_in = (1.0 - lb) * jax.nn.sigmoid(-f_raw)
    y, new_state = gla_gated(jax.nn.silu(q), k_in, i, log_f, g, gn_g, state)
    return mm(y, w_out), new_state


def kernel(x_prompt, x_sample, cache_k, cache_v, cache_kidx, state_wkv, state_shift, state_hgrn, page_table, w_in_ab, rwkv_mu, rwkv_w0, rwkv_w2, rwkv_a0, rwkv_a2, rwkv_g2, rwkv_kk, rwkv_ka, rwkv_rk, rwkv_lnx_g, rwkv_lnx_b, w_out_ab, rel_bias, ffn_w_gate, ffn_w_up, ffn_w_down, w_in_c, hgrn_lb, hgrn_gnorm, w_out_c, router, moe_w_gate, moe_w_up, moe_w_down, ln1_g, ln1_b, ln2_g, ln2_b):
    depth = ln1_g.shape[0]
    alpha = (2 * depth) ** 0.25
    a_width = rwkv_w0.shape[1]
    a_heads = a_width // A_HEAD
    a_proj = state_shift.shape[-1]
    c_heads, c_dk, c_dv = state_hgrn.shape[2:]
    lb_all = jnp.cumsum(jax.nn.softmax(hgrn_lb, axis=0), axis=0)
    lb_all = lb_all - lb_all[0:1]
    attend_p = functools.partial(dsa_prompt, rel_bias=rel_bias)
    xp, xs = x_prompt, x_sample
    kp_l, vp_l, kip_l, wkvp_l, shp_l, hgp_l = [], [], [], [], [], []
    ks_l, vs_l, kis_l, wkvs_l, shs_l, hgs_l = [], [], [], [], [], []
    for l in range(depth):
        j = l // 2
        if l % 2 == 0:
            rw = (rwkv_mu[j], rwkv_w0[j], rwkv_w2[j], rwkv_a0[j], rwkv_a2[j], rwkv_g2[j],
                  rwkv_kk[j], rwkv_ka[j], rwkv_rk[j], rwkv_lnx_g[j], rwkv_lnx_b[j])
            attend_s = functools.partial(attend_sample, cache_k=cache_k, cache_v=cache_v, cache_kidx=cache_kidx,
                                         layer=j, page_table=page_table, rel_bias=rel_bias)
            shift0 = jnp.zeros((xp.shape[0], a_proj), xp.dtype)
            wkv0 = jnp.zeros((xp.shape[0], a_heads, A_HEAD, A_HEAD), jnp.float32)
            yp, shp, wkvp, kp, vp, kip = ab_mixer(xp, shift0, wkv0, attend_p, (w_in_ab, j), rw, (w_out_ab, j))
            ys, shs, wkvs, kss, vss, kis = ab_mixer(xs, state_shift[j], state_wkv[j], attend_s, (w_in_ab, j), rw, (w_out_ab, j))
            kp_l.append(kp); vp_l.append(vp); kip_l.append(kip); wkvp_l.append(wkvp); shp_l.append(shp)
            ks_l.append(kss); vs_l.append(vss); kis_l.append(kis); wkvs_l.append(wkvs); shs_l.append(shs)
        else:
            hg0 = jnp.zeros((xp.shape[0], c_heads, c_dk, c_dv), jnp.float32)
            yp, hgp = hgrn2_mixer(xp, hg0, (w_in_c, j), lb_all[j], hgrn_gnorm[j], (w_out_c, j))
            ys, hgs = hgrn2_mixer(xs, state_hgrn[j], (w_in_c, j), lb_all[j], hgrn_gnorm[j], (w_out_c, j))
            hgp_l.append(hgp); hgs_l.append(hgs)
        xp = layer_norm(alpha * xp + yp, ln1_g[l], ln1_b[l])
        xs = layer_norm(alpha * xs + ys, ln1_g[l], ln1_b[l])
        if l % 2 == 0:
            fp = swiglu(xp, (ffn_w_gate, j), (ffn_w_up, j), (ffn_w_down, j))
            fs = swiglu(xs, (ffn_w_gate, j), (ffn_w_up, j), (ffn_w_down, j))
        else:
            fp = moe_swiglu(xp, router, moe_w_gate, moe_w_up, moe_w_down, j)
            fs = moe_swiglu(xs, router, moe_w_gate, moe_w_up, moe_w_down, j)
        xp = layer_norm(alpha * xp + fp, ln2_g[l], ln2_b[l])
        xs = layer_norm(alpha * xs + fs, ln2_g[l], ln2_b[l])
    return (xp, xs, jnp.stack(kp_l), jnp.stack(vp_l), jnp.stack(kip_l), jnp.stack(wkvp_l), jnp.stack(shp_l),
            jnp.stack(hgp_l), jnp.stack(ks_l), jnp.stack(vs_l), jnp.stack(kis_l), jnp.stack(wkvs_l),
            jnp.stack(shs_l), jnp.stack(hgs_l))
```
